```python
import math
import jax
import jax.numpy as jnp
from jax import lax
import numpy as np

D_MODEL = 1024
BATCH = 8
SEQ = 2048
DEPTH = 2
DEC_BATCH = 128
DEC_SEQ = 4
PAST_LEN = 16384
PAGE_SIZE = 128

D_BRANCH = D_MODEL // 2
N_BRANCH = 3
A_HEAD = 64
D_A = D_BRANCH
A_HEADS = D_A // A_HEAD
A_LORA_W = 64
A_LORA_A = 64
A_LORA_G = 128
A_PROJ = 3 * D_A + A_LORA_W + A_LORA_A + A_LORA_G
A_NORM_EPS = 64e-5
A_KK_EPS = 1e-12
R_HEAD = 128
D_R = D_BRANCH
R_HEADS = D_R // R_HEAD
R_CHUNK = 128
R_NORM_EPS = 1e-6
ROPE_BASE = 10000.0
D_C = D_BRANCH
C_BLOCK = 64
C_BLOCKS = D_C // C_BLOCK
CONV_W = 4
LRU_C = 8.0
D_IN = A_PROJ + 4 * D_R + 2 * D_C + N_BRANCH * D_MODEL
N_EXPERTS = 16
N_GROUPS = 4
EXPERTS_PER_GROUP = N_EXPERTS // N_GROUPS
TOP_K = 2
D_EXPERT = 512
NORM_EPS = 1e-6

kernel_name = 'hybrid_rwkv7_retention_rglru_moe_step'


def _rms_norm(x, g):
    xf = x.astype(jnp.float32)
    y = xf * lax.rsqrt(jnp.mean(xf * xf, axis=-1, keepdims=True) + NORM_EPS)
    return (y * g.astype(jnp.float32)).astype(x.dtype)


def _head_norm(x, eps):
    xc = x - jnp.mean(x, axis=-1, keepdims=True)
    return xc * lax.rsqrt(jnp.mean(xc * xc, axis=-1, keepdims=True) + eps)


def _rotary(x, pos):
    half = x.shape[-1] // 2
    inv = ROPE_BASE ** (-jnp.arange(half, dtype=jnp.float32) / half)
    ang = pos.astype(jnp.float32)[:, None] * inv[None, :]
    cos = jnp.cos(ang)[None, :, None, :]
    sin = jnp.sin(ang)[None, :, None, :]
    x1, x2 = x[..., :half], x[..., half:]
    return jnp.concatenate([x1 * cos - x2 * sin, x1 * sin + x2 * cos], axis=-1)


def _rwkv7(pa, shift0, wkv0, lp):
    bsz, L, _ = pa.shape
    f32 = jnp.float32
    prev = jnp.concatenate([shift0[:, None, :].astype(pa.dtype), pa[:, :-1]], axis=1)
    pm = (pa + (prev - pa) * lp['a_mu']).astype(f32)
    cuts = [D_A, 2 * D_A, 3 * D_A, 3 * D_A + A_LORA_W, 3 * D_A + A_LORA_W + A_LORA_A]
    r, k, v, xw, xa, xg = jnp.split(pm, cuts, axis=-1)
    w_log = -jax.nn.softplus(-(lp['a_w0'] + jnp.tanh(xw) @ lp['a_w2'])) - 0.5
    decay = jnp.exp(-jnp.exp(w_log))
    a = jax.nn.sigmoid(lp['a_a0'] + xa @ lp['a_a2'])
    g = jax.nn.sigmoid(xg) @ lp['a_g2']
    hd = lambda t: t.reshape(bsz, L, A_HEADS, A_HEAD)
    kk = hd(k * lp['a_kk'])
    kk = kk * lax.rsqrt(jnp.sum(kk * kk, axis=-1, keepdims=True) + A_KK_EPS)
    k = k * (1.0 + (a - 1.0) * lp['a_ka'])
    r_h, w_h, k_h, v_h, a_h = hd(r), hd(decay), hd(k), hd(v), hd(a)

    def step(S, inp):
        r_t, w_t, k_t, v_t, a_t, b_t = inp
        sa = jnp.einsum('bhvk,bhk->bhv', S, a_t)
        S = S * w_t[:, :, None, :] + sa[..., None] * b_t[:, :, None, :] + v_t[..., None] * k_t[:, :, None, :]
        return S, jnp.einsum('bhvk,bhk->bhv', S, r_t)

    xs = tuple(jnp.moveaxis(t, 1, 0) for t in (r_h, w_h, k_h, v_h, -kk, kk * a_h))
    s_fin, o = lax.scan(step, wkv0.astype(f32), xs)
    o = jnp.moveaxis(o, 0, 1)
    o = _head_norm(o, A_NORM_EPS).reshape(bsz, L, D_A) * lp['a_ln_g'] + lp['a_ln_b']
    bonus = jnp.sum(r_h * k_h * lp['a_rk'], axis=-1, keepdims=True) * v_h
    o = (o + bonus.reshape(bsz, L, D_A)) * g
    return o.astype(pa.dtype), pa[:, -1], s_fin


def _retention(q, k, v, gate, pos, s0):
    bsz, L, _ = q.shape
    f32 = jnp.float32
    C = min(R_CHUNK, L)
    N = L // C
    log_g = jnp.log1p(-jnp.exp2(-5.0 - jnp.arange(R_HEADS, dtype=f32)))
    hd = lambda t: t.reshape(bsz, L, R_HEADS, R_HEAD)
    qh = _rotary(hd(q.astype(f32)), pos)
    kh = _rotary(hd(k.astype(f32)), pos) * (R_HEAD ** -0.5)
    ch = lambda t: t.reshape(bsz, N, C, R_HEADS, R_HEAD)
    qc, kc, vc = ch(qh), ch(kh), ch(hd(v.astype(f32)))
    idx = jnp.arange(C, dtype=f32)
    diff = idx[:, None] - idx[None, :]
    dmask = jnp.where(diff >= 0, jnp.exp(log_g[:, None, None] * jnp.maximum(diff, 0.0)), 0.0)
    scores = jnp.einsum('bnihd,bnjhd->bnhij', qc, kc) * dmask
    intra = jnp.einsum('bnhij,bnjhd->bnihd', scores, vc)
    k_decay = jnp.exp(log_g[None, :] * (C - 1.0 - idx)[:, None])
    upd = jnp.einsum('bnjhk,bnjhv->bnhkv', kc * k_decay[:, :, None], vc)
    chunk_decay = jnp.exp(log_g * C)[None, :, None, None]

    def carry(S, u):
        return S * chunk_decay + u, S

    s_fin, s_start = lax.scan(carry, s0.astype(f32), jnp.moveaxis(upd, 1, 0))
    s_start = jnp.moveaxis(s_start, 0, 1)
    q_decay = jnp.exp(log_g[None, :] * (idx + 1.0)[:, None])
    inter = jnp.einsum('bnihk,bnhkv->bnihv', qc * q_decay[:, :, None], s_start)
    o = (intra + inter).reshape(bsz, L, R_HEADS, R_HEAD)
    o = _head_norm(o, R_NORM_EPS).reshape(bsz, L, D_R)
    o = jax.nn.silu(gate.astype(f32)) * o
    return o.astype(q.dtype), s_fin


def _rglru(xb, gb, conv0, h0, lp):
    bsz, L, _ = xb.shape
    f32 = jnp.float32
    xp = jnp.concatenate([conv0.astype(xb.dtype), xb], axis=1)
    w = lp['c_conv_w']
    xc = lp['c_conv_b'] + sum(xp[:, j:j + L] * w[j] for j in range(CONV_W))
    xf = xc.astype(f32)
    blk = xf.reshape(bsz, L, C_BLOCKS, C_BLOCK)
    r = jax.nn.sigmoid(jnp.einsum('blhi,hij->blhj', blk, lp['c_wr']).reshape(bsz, L, D_C) + lp['c_br'])
    i = jax.nn.sigmoid(jnp.einsum('blhi,hij->blhj', blk, lp['c_wi']).reshape(bsz, L, D_C) + lp['c_bi'])
    log_a = LRU_C * r * jax.nn.log_sigmoid(lp['c_lam'].astype(f32))
    a = jnp.exp(log_a)
    b = jnp.sqrt(-jnp.expm1(2.0 * log_a)) * (i * xf)
    b = b.at[:, 0].add(a[:, 0] * h0.astype(f32))

    def comb(lhs, rhs):
        a1, b1 = lhs
        a2, b2 = rhs
        return a1 * a2, a2 * b1 + b2

    _, h = lax.associative_scan(comb, (a, b), axis=1)
    y = h * jax.nn.gelu(gb.astype(f32))
    return y.astype(xb.dtype), xp[:, L:], h[:, -1]


def _moe(h, w_router, router_bias, wg, wu, wd):
    bsz, L, D = h.shape
    t = h.reshape(bsz * L, D)
    T = t.shape[0]
    probs = jax.nn.softmax((t @ w_router).astype(jnp.float32), axis=-1)
    sel = (probs + router_bias.astype(jnp.float32)).reshape(T, N_GROUPS, EXPERTS_PER_GROUP)
    grp_score = jnp.sum(lax.top_k(sel, TOP_K)[0], axis=-1)
    grp = jnp.argmax(grp_score, axis=-1)
    in_grp = sel[jnp.arange(T), grp]
    _, loc = lax.top_k(in_grp, TOP_K)
    eidx = grp[:, None] * EXPERTS_PER_GROUP + loc
    pk = jnp.take_along_axis(probs, eidx, axis=-1)
    wk = pk / jnp.sum(pk, axis=-1, keepdims=True)
    gates = jnp.sum(jax.nn.one_hot(eidx, N_EXPERTS, dtype=jnp.float32) * wk[..., None], axis=1)
    hg = jnp.einsum('td,edf->tef', t, wg)
    hu = jnp.einsum('td,edf->tef', t, wu)
    act = jax.nn.silu(hg) * hu * gates[:, :, None].astype(t.dtype)
    return jnp.einsum('tef,efd->td', act, wd).reshape(bsz, L, D)


def _trunk(x, c, pos, st, P, w_router, router_bias, norm_final):
    bsz, L, _ = x.shape
    cuts = np.cumsum([A_PROJ, D_R, D_R, D_R, D_R, D_C, D_C]).tolist()
    sh_l, wkv_l, ret_l, h_l, conv_l = [], [], [], [], []
    for l in range(DEPTH):
        lp = {name: arr[l] for name, arr in P.items()}
        mod = (jax.nn.silu(c.astype(jnp.float32)) @ lp['ada_w'] + lp['ada_b']).astype(x.dtype)
        sh1, sc1, gt1, sh2, sc2, gt2 = jnp.split(mod[:, None, :], 6, axis=-1)
        h = _rms_norm(x, lp['norm_mix']) * (1.0 + sc1) + sh1
        proj = h @ lp['w_in']
        pa, q, k, v, g_r, x_c, g_c, g_m = jnp.split(proj, cuts, axis=-1)
        o_a, n_sh, n_wkv = _rwkv7(pa, st[0][l], st[1][l], lp)
        o_b, n_ret = _retention(q, k, v, g_r, pos, st[2][l])
        o_c, n_conv, n_h = _rglru(x_c, g_c, st[4][l], st[3][l], lp)
        br = jnp.einsum('nbld,nde->nble', jnp.stack([o_a, o_b, o_c]), lp['w_branch'])
        gates = jax.nn.sigmoid(g_m.reshape(bsz, L, N_BRANCH, D_MODEL))
        merged = jnp.einsum('blne,nble->ble', gates, br)
        x = x + gt1 * (merged @ lp['w_out'])
        h = _rms_norm(x, lp['norm_ffn']) * (1.0 + sc2) + sh2
        x = x + gt2 * _moe(h, w_router, router_bias, lp['moe_wg'], lp['moe_wu'], lp['moe_wd'])
        sh_l.append(n_sh)
        wkv_l.append(n_wkv)
        ret_l.append(n_ret)
        h_l.append(n_h)
        conv_l.append(n_conv)
    y = _rms_norm(x, norm_final)
    new = tuple(jnp.stack(s).astype(x.dtype) for s in (sh_l, wkv_l, ret_l, h_l, conv_l))
    return y, new


def setup_inputs(seed: int = 0) -> dict:
    key = jax.random.key(seed)
    keys = iter(jax.random.split(key, 64))
    f32 = jnp.float32

    def nrm(shape, scale):
        return jax.random.normal(next(keys), shape, f32) * scale

    def unif(shape, lo, hi):
        return jax.random.uniform(next(keys), shape, f32, lo, hi)

    d = D_MODEL
    inp = {}
    inp['x_prompt'] = nrm((BATCH, SEQ, d), 1.0)
    inp['x_sample'] = nrm((DEC_BATCH, DEC_SEQ, d), 1.0)
    inp['c_prompt'] = nrm((BATCH, d), 1.0)
    inp['c_sample'] = nrm((DEC_BATCH, d), 1.0)
    inp['state_rwkv_shift'] = nrm((DEPTH, DEC_BATCH, A_PROJ), 1.0)
    inp['state_rwkv_wkv'] = nrm((DEPTH, DEC_BATCH, A_HEADS, A_HEAD, A_HEAD), 0.5)
    inp['state_ret'] = nrm((DEPTH, DEC_BATCH, R_HEADS, R_HEAD, R_HEAD), 1.0)
    inp['state_lru_h'] = nrm((DEPTH, DEC_BATCH, D_C), 0.5)
    inp['state_lru_conv'] = nrm((DEPTH, DEC_BATCH, CONV_W - 1, D_C), 1.0)
    inp['norm_mix'] = 1.0 + nrm((DEPTH, d), 0.02)
    inp['norm_ffn'] = 1.0 + nrm((DEPTH, d), 0.02)
    inp['norm_final'] = 1.0 + nrm((d,), 0.02)
    inp['ada_w'] = nrm((DEPTH, d, 6 * d), 0.5 * d ** -0.5)
    inp['ada_b'] = nrm((DEPTH, 6 * d), 0.02)
    inp['w_in'] = nrm((DEPTH, d, D_IN), d ** -0.5)
    inp['a_mu'] = unif((DEPTH, A_PROJ), 0.0, 1.0)
    inp['a_w0'] = unif((DEPTH, D_A), -6.0, -1.0)
    inp['a_w2'] = nrm((DEPTH, A_LORA_W, D_A), 0.1)
    inp['a_a0'] = nrm((DEPTH, D_A), 0.1)
    inp['a_a2'] = nrm((DEPTH, A_LORA_A, D_A), A_LORA_A ** -0.5)
    inp['a_g2'] = nrm((DEPTH, A_LORA_G, D_A), A_LORA_G ** -0.5)
    inp['a_kk'] = 0.85 + nrm((DEPTH, D_A), 0.02)
    inp['a_ka'] = 1.0 + nrm((DEPTH, D_A), 0.02)
    inp['a_rk'] = nrm((DEPTH, A_HEADS, A_HEAD), 0.1)
    inp['a_ln_g'] = 1.0 + nrm((DEPTH, D_A), 0.02)
    inp['a_ln_b'] = nrm((DEPTH, D_A), 0.02)
    inp['c_conv_w'] = nrm((DEPTH, CONV_W, D_C), CONV_W ** -0.5)
    inp['c_conv_b'] = nrm((DEPTH, D_C), 0.02)
    inp['c_wr'] = nrm((DEPTH, C_BLOCKS, C_BLOCK, C_BLOCK), C_BLOCK ** -0.5)
    inp['c_br'] = nrm((DEPTH, D_C), 0.02)
    inp['c_wi'] = nrm((DEPTH, C_BLOCKS, C_BLOCK, C_BLOCK), C_BLOCK ** -0.5)
    inp['c_bi'] = nrm((DEPTH, D_C), 0.02)
    s = unif((DEPTH, D_C), 0.9, 0.999) ** (1.0 / LRU_C)
    inp['c_lam'] = jnp.log(s) - jnp.log1p(-s)
    inp['w_branch'] = nrm((DEPTH, N_BRANCH, D_BRANCH, d), D_BRANCH ** -0.5)
    inp['w_out'] = nrm((DEPTH, d, d), d ** -0.5)
    inp['w_router'] = nrm((d, N_EXPERTS), d ** -0.5)
    inp['router_bias'] = nrm((N_EXPERTS,), 0.01)
    inp['moe_wg'] = nrm((DEPTH, N_EXPERTS, d, D_EXPERT), d ** -0.5)
    inp['moe_wu'] = nrm((DEPTH, N_EXPERTS, d, D_EXPERT), d ** -0.5)
    inp['moe_wd'] = nrm((DEPTH, N_EXPERTS, D_EXPERT, d), D_EXPERT ** -0.5)
    return inp


def reference(x_prompt, x_sample, c_prompt, c_sample, state_rwkv_shift, state_rwkv_wkv, state_ret,
              state_lru_h, state_lru_conv, norm_mix, norm_ffn, norm_final, ada_w, ada_b, w_in,
              a_mu, a_w0, a_w2, a_a0, a_a2, a_g2, a_kk, a_ka, a_rk, a_ln_g, a_ln_b,
              c_conv_w, c_conv_b, c_wr, c_br, c_wi, c_bi, c_lam, w_branch, w_out,
              w_router, router_bias, moe_wg, moe_wu, moe_wd):
    P = dict(norm_mix=norm_mix, norm_ffn=norm_ffn, ada_w=ada_w, ada_b=ada_b, w_in=w_in,
             a_mu=a_mu, a_w0=a_w0, a_w2=a_w2, a_a0=a_a0, a_a2=a_a2, a_g2=a_g2, a_kk=a_kk,
             a_ka=a_ka, a_rk=a_rk, a_ln_g=a_ln_g, a_ln_b=a_ln_b, c_conv_w=c_conv_w,
             c_conv_b=c_conv_b, c_wr=c_wr, c_br=c_br, c_wi=c_wi, c_bi=c_bi, c_lam=c_lam,
             w_branch=w_branch, w_out=w_out, moe_wg=moe_wg, moe_wu=moe_wu, moe_wd=moe_wd)
    bp = x_prompt.shape[0]

    def zeros_like_state(s):
        return jnp.zeros((DEPTH, bp) + s.shape[2:], x_prompt.dtype)

    st_prompt = (zeros_like_state(state_rwkv_shift), zeros_like_state(state_rwkv_wkv),
                 zeros_like_state(state_ret), zeros_like_state(state_lru_h),
                 zeros_like_state(state_lru_conv))
    st_sample = (state_rwkv_shift, state_rwkv_wkv, state_ret, state_lru_h, state_lru_conv)
    pos_prompt = jnp.arange(x_prompt.shape[1], dtype=jnp.int32)
    pos_sample = PAST_LEN + jnp.arange(x_sample.shape[1], dtype=jnp.int32)
    y_prompt, (p_sh, p_wkv, p_ret, p_h, p_conv) = _trunk(
        x_prompt, c_prompt, pos_prompt, st_prompt, P, w_router, router_bias, norm_final)
    y_sample, (s_sh, s_wkv, s_ret, s_h, s_conv) = _trunk(
        x_sample, c_sample, pos_sample, st_sample, P, w_router, router_bias, norm_final)
    return (y_prompt, y_sample, p_sh, p_wkv, p_ret, p_h, p_conv, s_sh, s_wkv, s_ret, s_h, s_conv)
```

```python
import functools
import math

import numpy as np
import jax
import jax.numpy as jnp
from jax import lax
from jax.experimental import pallas as pl
from jax.experimental.pallas import tpu as pltpu

F32 = jnp.float32
BF16 = jnp.bfloat16

D_MODEL = 1024
DEPTH = 2
PAST_LEN = 16384
D_BRANCH = 512
A_HEAD = 64
A_HEADS = 8
A_PROJ = 1792
A_NORM_EPS = 64e-5
A_KK_EPS = 1e-12
R_HEAD = 128
R_HEADS = 4
R_CHUNK = 128
R_NORM_EPS = 1e-6
ROPE_BASE = 10000.0
C_BLOCK = 64
C_BLOCKS = 8
CONV_W = 4
LRU_C = 8.0
N_EXPERTS = 16
N_GROUPS = 4
EXPERTS_PER_GROUP = 4
D_EXPERT = 512
NORM_EPS = 1e-6

LANES = 128
RET_ROWS = 128
VMEM_LIMIT = 48 * 1024 * 1024


def _cparams(sem):
    return pltpu.CompilerParams(dimension_semantics=sem, vmem_limit_bytes=VMEM_LIMIT)


def _dot(a, b):
    return jnp.dot(a, b, preferred_element_type=F32)


def _bdot(a, b):
    return jnp.dot(a.astype(BF16), b.astype(BF16), preferred_element_type=F32)


def _segsum(x, ones_bf16):
    hi = x.astype(BF16)
    r1 = x - hi.astype(F32)
    mid = r1.astype(BF16)
    lo = (r1 - mid.astype(F32)).astype(BF16)
    return (_dot(lo, ones_bf16) + _dot(mid, ones_bf16)) + _dot(hi, ones_bf16)


def _ada_kernel(c_ref, w_ref, b_ref, o_ref):
    c = c_ref[...]
    s = c * jax.nn.sigmoid(c)
    o_ref[...] = _bdot(s, w_ref[...]) + b_ref[...]


def _ada(c, ada_w, ada_b):
    rows = c.shape[0]
    tn = 1536
    return pl.pallas_call(
        _ada_kernel,
        out_shape=jax.ShapeDtypeStruct((DEPTH, rows, 6 * D_MODEL), F32),
        grid=(DEPTH, 6 * D_MODEL // tn),
        in_specs=[
            pl.BlockSpec((rows, D_MODEL), lambda l, j: (0, 0)),
            pl.BlockSpec((None, D_MODEL, tn), lambda l, j: (l, 0, j)),
            pl.BlockSpec((None, 1, tn), lambda l, j: (l, 0, j)),
        ],
        out_specs=pl.BlockSpec((None, rows, tn), lambda l, j: (l, 0, j)),
        compiler_params=_cparams(("arbitrary", "arbitrary")),
    )(c, ada_w, ada_b.reshape(DEPTH, 1, 6 * D_MODEL))


class _Group:
    def __init__(self, batch, seq, mod4):
        self.batch = batch
        self.seq = seq
        self.rows = batch * seq
        self.mod4 = mod4
        self.per_token = mod4.shape[0] == 1 and mod4.shape[2] != 1

    def mod_spec(self, k, tm):
        if self.per_token:
            return pl.BlockSpec((None, None, tm, D_MODEL), lambda i, *_: (0, k, i, 0))
        seq = self.seq
        return pl.BlockSpec((None, None, 1, D_MODEL), lambda i, *_: ((i * tm) // seq, k, 0, 0))


def _rms(x, g):
    return x * lax.rsqrt(jnp.mean(x * x, axis=-1, keepdims=True) + NORM_EPS) * g


def _norm_mod_kernel(x_ref, g_ref, sc_ref, sh_ref, o_ref):
    y = _rms(x_ref[...], g_ref[...])
    o_ref[...] = (y * (1.0 + sc_ref[...]) + sh_ref[...]).astype(o_ref.dtype)


def _norm_kernel(x_ref, g_ref, o_ref):
    o_ref[...] = _rms(x_ref[...], g_ref[...]).astype(o_ref.dtype)


def _norm_mod(x, g, grp, k_sc, k_sh, tm):
    return pl.pallas_call(
        _norm_mod_kernel,
        out_shape=jax.ShapeDtypeStruct((grp.rows, D_MODEL), BF16),
        grid=(grp.rows // tm,),
        in_specs=[
            pl.BlockSpec((tm, D_MODEL), lambda i: (i, 0)),
            pl.BlockSpec((1, D_MODEL), lambda i: (0, 0)),
            grp.mod_spec(k_sc, tm),
            grp.mod_spec(k_sh, tm),
        ],
        out_specs=pl.BlockSpec((tm, D_MODEL), lambda i: (i, 0)),
        compiler_params=_cparams(("arbitrary",)),
    )(x, g.reshape(1, D_MODEL), grp.mod4, grp.mod4)


def _final_norm(x, g, tm):
    rows = x.shape[0]
    return pl.pallas_call(
        _norm_kernel,
        out_shape=jax.ShapeDtypeStruct((rows, D_MODEL), F32),
        grid=(rows // tm,),
        in_specs=[
            pl.BlockSpec((tm, D_MODEL), lambda i: (i, 0)),
            pl.BlockSpec((1, D_MODEL), lambda i: (0, 0)),
        ],
        out_specs=pl.BlockSpec((tm, D_MODEL), lambda i: (i, 0)),
        compiler_params=_cparams(("arbitrary",)),
    )(x, g.reshape(1, D_MODEL))


def _mm_kernel(a_ref, w_ref, o_ref):
    o_ref[...] = _dot(a_ref[...], w_ref[...])


def _matmul(a, w, tm, tn):
    rows, k = a.shape
    n = w.shape[1]
    return pl.pallas_call(
        _mm_kernel,
        out_shape=jax.ShapeDtypeStruct((rows, n), F32),
        grid=(n // tn, rows // tm),
        in_specs=[
            pl.BlockSpec((tm, k), lambda j, i: (i, 0)),
            pl.BlockSpec((k, tn), lambda j, i: (0, j)),
        ],
        out_specs=pl.BlockSpec((tm, tn), lambda j, i: (i, j)),
        compiler_params=_cparams(("arbitrary", "arbitrary")),
    )(a, w)


def _rwkv_kernel(nb, lb, three_d,
                 pa_ref, sh0_ref, s0_ref, mu_ref, w0_ref, w2_ref, a0_ref, a2_ref, g2_ref,
                 kk_ref, ka_ref, rk_ref, lng_ref, lnb_ref, ones_ref, eye_ref,
                 o_ref, sout_ref,
                 carry_ref, prev_ref, s_ref, r_s, w_s, k_s, v_s, an_s, bn_s, o_s):
    c = pl.program_id(1)
    rows = nb * lb
    ones = ones_ref[...]

    @pl.when(c == 0)
    def _():
        carry_ref[...] = sh0_ref[...]
        s_ref[...] = s0_ref[...]

    pa = pa_ref[...]
    if three_d:
        pa = pa.reshape(rows, A_PROJ)
    prev_ref[...] = pltpu.roll(pa, 1, axis=0)
    for b in range(nb):
        prev_ref[pl.ds(b * lb, 1), :] = carry_ref[pl.ds(b, 1), :]
        carry_ref[pl.ds(b, 1), :] = pa[b * lb + lb - 1:b * lb + lb, :]
    pm = pa + (prev_ref[...] - pa) * mu_ref[...]

    r = pm[:, 0:512]
    k = pm[:, 512:1024]
    v = pm[:, 1024:1536]
    xwa = pm[:, 1536:1664]
    xg = pm[:, 1664:1792]
    w_log = -jax.nn.softplus(-(w0_ref[...] + _bdot(jnp.tanh(xwa), w2_ref[...]))) - 0.5
    decay = jnp.exp(-jnp.exp(w_log))
    a = jax.nn.sigmoid(a0_ref[...] + _bdot(xwa, a2_ref[...]))
    g = _bdot(jax.nn.sigmoid(xg), g2_ref[...])
    kk = k * kk_ref[...]
    kk = kk * lax.rsqrt(_segsum(kk * kk, ones) + A_KK_EPS)
    k2 = k * (1.0 + (a - 1.0) * ka_ref[...])
    r_s[...] = r
    w_s[...] = decay
    k_s[...] = k2
    v_s[...] = v
    an_s[...] = -kk
    bn_s[...] = kk * a

    eye = eye_ref[...]

    def step(t, carry):
        for b in range(nb):
            row = b * lb + t
            a_t = an_s[pl.ds(row, 1), :]
            b_t = bn_s[pl.ds(row, 1), :]
            w_t = w_s[pl.ds(row, 1), :]
            k_t = k_s[pl.ds(row, 1), :]
            v_t = v_s[pl.ds(row, 1), :]
            r_t = r_s[pl.ds(row, 1), :]
            s = s_ref[b]
            sa = _segsum(s * a_t, ones)
            vcol = _segsum(eye * v_t, ones)
            s = s * w_t + sa * b_t + vcol * k_t
            s_ref[b] = s
            out = _segsum(s * r_t, ones)
            o_s[pl.ds(row, 1), :] = jnp.sum(eye * out, axis=0, keepdims=True)
        return carry

    lax.fori_loop(0, lb, step, 0)

    o = o_s[...]
    mean = _segsum(o, ones) * (1.0 / A_HEAD)
    oc = o - mean
    var = _segsum(oc * oc, ones) * (1.0 / A_HEAD)
    o = oc * lax.rsqrt(var + A_NORM_EPS) * lng_ref[...] + lnb_ref[...]
    bonus = _segsum(r * k2 * rk_ref[...], ones) * v
    res = ((o + bonus) * g).astype(o_ref.dtype)
    if three_d:
        res = res.reshape(nb, lb, D_BRANCH)
    o_ref[...] = res
    sout_ref[...] = s_ref[...]


def _rwkv(pa, shift0, s0, prm, batch, seq, nb, lb):
    three_d = nb > 1 and lb % 8 == 0 and seq != lb
    nchunk = seq // lb
    rows = nb * lb
    if three_d:
        pa_in = pa.reshape(batch, seq, A_PROJ)
        pa_spec = pl.BlockSpec((nb, lb, A_PROJ), lambda i, c: (i, c, 0))
        o_shape = jax.ShapeDtypeStruct((batch, seq, D_BRANCH), BF16)
        o_spec = pl.BlockSpec((nb, lb, D_BRANCH), lambda i, c: (i, c, 0))
    else:
        assert nb == 1 or nchunk == 1
        pa_in = pa
        pa_spec = pl.BlockSpec((rows, A_PROJ), lambda i, c: (i * nchunk + c, 0))
        o_shape = jax.ShapeDtypeStruct((batch * seq, D_BRANCH), BF16)
        o_spec = pl.BlockSpec((rows, D_BRANCH), lambda i, c: (i * nchunk + c, 0))

    def full(arr):
        nd = arr.ndim
        return pl.BlockSpec(arr.shape, lambda i, c: (0,) * nd)

    params = [prm[n] for n in ("mu", "w0", "w2", "a0", "a2", "g2", "kk", "ka", "rk", "lng", "lnb",
                               "ones", "eye")]
    o, s_out = pl.pallas_call(
        functools.partial(_rwkv_kernel, nb, lb, three_d),
        out_shape=(o_shape, jax.ShapeDtypeStruct((batch, A_HEAD, D_BRANCH), F32)),
        grid=(batch // nb, nchunk),
        in_specs=[pa_spec,
                  pl.BlockSpec((nb, A_PROJ), lambda i, c: (i, 0)),
                  pl.BlockSpec((nb, A_HEAD, D_BRANCH), lambda i, c: (i, 0, 0))]
                 + [full(p) for p in params],
        out_specs=(o_spec, pl.BlockSpec((nb, A_HEAD, D_BRANCH), lambda i, c: (i, 0, 0))),
        scratch_shapes=[pltpu.VMEM((nb, A_PROJ), F32),
                        pltpu.VMEM((rows, A_PROJ), F32),
                        pltpu.VMEM((nb, A_HEAD, D_BRANCH), F32)]
                       + [pltpu.VMEM((rows, D_BRANCH), F32) for _ in range(7)],
        compiler_params=_cparams(("arbitrary", "arbitrary")),
    )(pa_in, shift0, s0, *params)
    return o.reshape(batch * seq, D_BRANCH), s_out


def _ret_kernel(nb, lb,
                rq_ref, cos_ref, sin_ref, dm_ref, kd_ref, qd_ref, cd_ref, s0_ref,
                o_ref, sout_ref, s_ref):
    c = pl.program_id(1)

    @pl.when(c == 0)
    def _():
        s_ref[...] = s0_ref[...]

    cos = cos_ref[...]
    sin = sin_ref[...]
    scale = R_HEAD ** -0.5
    half = R_HEAD // 2
    row8 = lax.broadcasted_iota(jnp.int32, (8, R_HEAD), 0)
    for h in range(R_HEADS):
        lo, hi = h * R_HEAD, (h + 1) * R_HEAD
        q = rq_ref[:, lo:hi]
        k = rq_ref[:, 512 + lo:512 + hi]
        v = rq_ref[:, 1024 + lo:1024 + hi]
        gate = rq_ref[:, 1536 + lo:1536 + hi]
        qh = q * cos + pltpu.roll(q, half, axis=1) * sin
        kh = (k * cos + pltpu.roll(k, half, axis=1) * sin) * scale
        scores = lax.dot_general(qh.astype(BF16), kh.astype(BF16), (((1,), (1,)), ((), ())),
                                 preferred_element_type=F32) * dm_ref[h]
        o = _bdot(scores, v)
        qd = qh * qd_ref[:, lo:hi]
        ku = kh * kd_ref[:, lo:hi]
        cd = cd_ref[h]
        if nb == 1:
            s = s_ref[0, h]
            o = o + _bdot(qd, s)
            upd = lax.dot_general(ku.astype(BF16), v.astype(BF16), (((0,), (0,)), ((), ())),
                                  preferred_element_type=F32)
            s_ref[0, h] = s * cd + upd
        else:
            per_tile = 8 // lb
            inter = []
            for i in range(RET_ROWS // 8):
                qd_t = qd[i * 8:(i + 1) * 8, :]
                ku_t = ku[i * 8:(i + 1) * 8, :]
                v_t = v[i * 8:(i + 1) * 8, :].astype(BF16)
                acc = None
                for j in range(per_tile):
                    b = i * per_tile + j
                    m = (row8 >= j * lb) & (row8 < (j + 1) * lb)
                    s = s_ref[b, h]
                    part = _bdot(jnp.where(m, qd_t, 0.0), s)
                    acc = part if acc is None else acc + part
                    upd = lax.dot_general(jnp.where(m, ku_t, 0.0).astype(BF16), v_t,
                                          (((0,), (0,)), ((), ())), preferred_element_type=F32)
                    s_ref[b, h] = s * cd + upd
                inter.append(acc)
            o = o + jnp.concatenate(inter, axis=0)
        oc = o - jnp.mean(o, axis=-1, keepdims=True)
        on = oc * lax.rsqrt(jnp.mean(oc * oc, axis=-1, keepdims=True) + R_NORM_EPS)
        o_ref[:, lo:hi] = (gate * jax.nn.sigmoid(gate) * on).astype(o_ref.dtype)
    sout_ref[...] = s_ref[...]


def _retention(rq, s0, tabs, batch, seq):
    lb = min(R_CHUNK, seq)
    nb = RET_ROWS // lb
    nchunk = seq // lb
    cos, sin, dm, kd, qd, cd = tabs
    ntab = cos.shape[0] // RET_ROWS

    def const(arr):
        nd = arr.ndim
        return pl.BlockSpec(arr.shape, lambda i, c: (0,) * nd)

    tab_idx = (lambda i, c: (c, 0)) if ntab > 1 else (lambda i, c: (0, 0))
    return pl.pallas_call(
        functools.partial(_ret_kernel, nb, lb),
        out_shape=(jax.ShapeDtypeStruct((batch * seq, D_BRANCH), BF16),
                   jax.ShapeDtypeStruct((batch, R_HEADS, R_HEAD, R_HEAD), F32)),
        grid=(batch // nb, nchunk),
        in_specs=[pl.BlockSpec((RET_ROWS, 4 * D_BRANCH), lambda i, c: (i * nchunk + c, 0)),
                  pl.BlockSpec((RET_ROWS, R_HEAD), tab_idx),
                  pl.BlockSpec((RET_ROWS, R_HEAD), tab_idx),
                  const(dm), const(kd), const(qd), const(cd),
                  pl.BlockSpec((nb, R_HEADS, R_HEAD, R_HEAD), lambda i, c: (i, 0, 0, 0))],
        out_specs=(pl.BlockSpec((RET_ROWS, D_BRANCH), lambda i, c: (i * nchunk + c, 0)),
                   pl.BlockSpec((nb, R_HEADS, R_HEAD, R_HEAD), lambda i, c: (i, 0, 0, 0))),
        scratch_shapes=[pltpu.VMEM((nb, R_HEADS, R_HEAD, R_HEAD), F32)],
        compiler_params=_cparams(("arbitrary", "arbitrary")),
    )(rq, cos, sin, dm, kd, qd, cd, s0)


def _ret_tables(seq, pos0):
    lb = min(R_CHUNK, seq)
    nb = RET_ROWS // lb
    half = R_HEAD // 2
    pos = pos0 + jnp.arange(seq, dtype=jnp.int32)
    inv = ROPE_BASE ** (-jnp.arange(half, dtype=F32) / half)
    ang = pos.astype(F32)[:, None] * inv[None, :]
    cos, sin = jnp.cos(ang), jnp.sin(ang)
    cos2 = jnp.concatenate([cos, cos], axis=-1)
    sin2 = jnp.concatenate([-sin, sin], axis=-1)
    if nb > 1:
        cos2 = jnp.tile(cos2, (nb, 1))
        sin2 = jnp.tile(sin2, (nb, 1))
    log_g = jnp.log1p(-jnp.exp2(-5.0 - jnp.arange(R_HEADS, dtype=F32)))
    idx = jnp.arange(lb, dtype=F32)
    diff = idx[:, None] - idx[None, :]
    dmask = jnp.where(diff >= 0, jnp.exp(log_g[:, None, None] * jnp.maximum(diff, 0.0)), 0.0)
    if nb > 1:
        dmask = jnp.einsum("ab,hij->haibj", jnp.eye(nb, dtype=F32), dmask).reshape(
            R_HEADS, RET_ROWS, RET_ROWS)
    k_decay = jnp.exp(log_g[None, :] * (lb - 1.0 - idx)[:, None])
    q_decay = jnp.exp(log_g[None, :] * (idx + 1.0)[:, None])
    kd = jnp.tile(jnp.repeat(k_decay, R_HEAD, axis=1), (nb, 1))
    qd = jnp.tile(jnp.repeat(q_decay, R_HEAD, axis=1), (nb, 1))
    cd = jnp.broadcast_to(jnp.exp(log_g * lb)[:, None, None], (R_HEADS, 1, R_HEAD))
    return cos2, sin2, dmask, kd, qd, cd


def _lru_kernel(nb, lb, three_d,
                cx_ref, conv0_ref, h0_ref, cw_ref, cb_ref, wri_ref, br_ref, bi_ref, lam_ref,
                o_ref, hout_ref,
                cc_ref, hc_ref, x1_ref, x2_ref, x3_ref, a_ref, b_ref):
    c = pl.program_id(1)
    rows = nb * lb

    @pl.when(c == 0)
    def _():
        cc_ref[...] = conv0_ref[...]
        hc_ref[...] = h0_ref[...]

    cx = cx_ref[...]
    if three_d:
        cx = cx.reshape(rows, 2 * D_BRANCH)
    xb = cx[:, :D_BRANCH]
    gb = cx[:, D_BRANCH:]
    x1_ref[...] = pltpu.roll(xb, 1, axis=0)
    x2_ref[...] = pltpu.roll(xb, 2, axis=0)
    x3_ref[...] = pltpu.roll(xb, 3, axis=0)
    for b in range(nb):
        r0 = b * lb
        c0 = cc_ref[b, 0:1, :]
        c1 = cc_ref[b, 1:2, :]
        c2 = cc_ref[b, 2:3, :]
        x1_ref[pl.ds(r0, 1), :] = c2
        x2_ref[pl.ds(r0, 1), :] = c1
        x2_ref[pl.ds(r0 + 1, 1), :] = c2
        x3_ref[pl.ds(r0, 1), :] = c0
        x3_ref[pl.ds(r0 + 1, 1), :] = c1
        x3_ref[pl.ds(r0 + 2, 1), :] = c2
        cc_ref[b] = xb[r0 + lb - 3:r0 + lb, :]
    cw = cw_ref[...]
    xc = cb_ref[...] + (((x3_ref[...] * cw[0:1, :] + x2_ref[...] * cw[1:2, :]) + x1_ref[...] * cw[2:3, :])
                        + xb * cw[3:4, :])
    ri = _bdot(xc, wri_ref[...])
    r = jax.nn.sigmoid(ri[:, :D_BRANCH] + br_ref[...])
    i = jax.nn.sigmoid(ri[:, D_BRANCH:] + bi_ref[...])
    log_a = LRU_C * r * jax.nn.log_sigmoid(lam_ref[...])
    a = jnp.exp(log_a)
    bb = jnp.sqrt(-jnp.tanh(log_a) * (a * a + 1.0)) * (i * xc)
    a_ref[...] = a
    b_ref[...] = bb
    for b in range(nb):
        r0 = b * lb
        b_ref[pl.ds(r0, 1), :] = bb[r0:r0 + 1, :] + a[r0:r0 + 1, :] * hc_ref[b]
    a = a_ref[...]
    bb = b_ref[...]
    t_idx = lax.broadcasted_iota(jnp.int32, (rows, D_BRANCH), 0) % lb
    s = 1
    while s < lb:
        keep = t_idx >= s
        a_sh = jnp.where(keep, pltpu.roll(a, s, axis=0), 1.0)
        b_sh = jnp.where(keep, pltpu.roll(bb, s, axis=0), 0.0)
        bb = a * b_sh + bb
        a = a * a_sh
        s *= 2
    h = bb
    for b in range(nb):
        r0 = b * lb
        hc_ref[b] = h[r0 + lb - 1:r0 + lb, :]
    res = (h * jax.nn.gelu(gb)).astype(o_ref.dtype)
    if three_d:
        res = res.reshape(nb, lb, D_BRANCH)
    o_ref[...] = res
    hout_ref[...] = hc_ref[...]


def _rglru(cx, conv0, h0, prm, batch, seq, nb, lb):
    nchunk = seq // lb
    rows = nb * lb
    three_d = nb > 1 and nchunk > 1
    if three_d:
        cx_in = cx.reshape(batch, seq, 2 * D_BRANCH)
        cx_spec = pl.BlockSpec((nb, lb, 2 * D_BRANCH), lambda i, c: (i, c, 0))
        o_shape = jax.ShapeDtypeStruct((batch, seq, D_BRANCH), BF16)
        o_spec = pl.BlockSpec((nb, lb, D_BRANCH), lambda i, c: (i, c, 0))
    else:
        cx_in = cx
        cx_spec = pl.BlockSpec((rows, 2 * D_BRANCH), lambda i, c: (i * nchunk + c, 0))
        o_shape = jax.ShapeDtypeStruct((batch * seq, D_BRANCH), BF16)
        o_spec = pl.BlockSpec((rows, D_BRANCH), lambda i, c: (i * nchunk + c, 0))

    def full(arr):
        nd = arr.ndim
        return pl.BlockSpec(arr.shape, lambda i, c: (0,) * nd)

    params = [prm[n] for n in ("cw", "cb", "wri", "br", "bi", "lam")]
    o, h_out = pl.pallas_call(
        functools.partial(_lru_kernel, nb, lb, three_d),
        out_shape=(o_shape, jax.ShapeDtypeStruct((batch, 1, D_BRANCH), F32)),
        grid=(batch // nb, nchunk),
        in_specs=[cx_spec,
                  pl.BlockSpec((nb, CONV_W - 1, D_BRANCH), lambda i, c: (i, 0, 0)),
                  pl.BlockSpec((nb, 1, D_BRANCH), lambda i, c: (i, 0, 0))]
                 + [full(p) for p in params],
        out_specs=(o_spec, pl.BlockSpec((nb, 1, D_BRANCH), lambda i, c: (i, 0, 0))),
        scratch_shapes=[pltpu.VMEM((nb, CONV_W - 1, D_BRANCH), F32),
                        pltpu.VMEM((nb, 1, D_BRANCH), F32)]
                       + [pltpu.VMEM((rows, D_BRANCH), F32) for _ in range(5)],
        compiler_params=_cparams(("arbitrary", "arbitrary")),
    )(cx_in, conv0, h0.reshape(batch, 1, D_BRANCH), *params)
    return o.reshape(batch * seq, D_BRANCH), h_out.reshape(batch, D_BRANCH)


def _merge_kernel(h_ref, oa_ref, ob_ref, oc_ref, x_ref, gt_ref, wgm_ref, wb_ref, wout_ref, o_ref):
    gm = _dot(h_ref[...], wgm_ref[...])
    merged = None
    for n, br_ref in enumerate((oa_ref, ob_ref, oc_ref)):
        br = _dot(br_ref[...], wb_ref[n])
        term = jax.nn.sigmoid(gm[:, n * D_MODEL:(n + 1) * D_MODEL]) * br
        merged = term if merged is None else merged + term
    y = _bdot(merged, wout_ref[...])
    o_ref[...] = x_ref[...] + gt_ref[...] * y


def _merge_out(h, oa, ob, oc, x, grp, wgm, wb, wout, tm):
    row = lambda i: (i, 0)
    return pl.pallas_call(
        _merge_kernel,
        out_shape=jax.ShapeDtypeStruct((grp.rows, D_MODEL), F32),
        grid=(grp.rows // tm,),
        in_specs=[pl.BlockSpec((tm, D_MODEL), row),
                  pl.BlockSpec((tm, D_BRANCH), row),
                  pl.BlockSpec((tm, D_BRANCH), row),
                  pl.BlockSpec((tm, D_BRANCH), row),
                  pl.BlockSpec((tm, D_MODEL), row),
                  grp.mod_spec(2, tm),
                  pl.BlockSpec(wgm.shape, lambda i: (0, 0)),
                  pl.BlockSpec(wb.shape, lambda i: (0, 0, 0)),
                  pl.BlockSpec(wout.shape, lambda i: (0, 0))],
        out_specs=pl.BlockSpec((tm, D_MODEL), row),
        compiler_params=_cparams(("arbitrary",)),
    )(h, oa, ob, oc, x, grp.mod4, wgm, wb, wout)


def _route(probs, sel):
    def beats(vj, vi, j, i):
        return (vj > vi) | ((vj == vi) & (j < i)) if j < i else (vj > vi)

    grp_score = []
    for g in range(N_GROUPS):
        vals = sel[g * 4:(g + 1) * 4]
        best = None
        for i in range(4):
            for j in range(i + 1, 4):
                sm = vals[i] + vals[j]
                best = sm if best is None else jnp.maximum(best, sm)
        grp_score.append(best)
    top = grp_score[0]
    gidx = jnp.zeros_like(top, dtype=jnp.int32)
    for g in range(1, N_GROUPS):
        better = grp_score[g] > top
        top = jnp.where(better, grp_score[g], top)
        gidx = jnp.where(better, g, gidx)
    chosen = []
    for e in range(N_EXPERTS):
        g, i = divmod(e, 4)
        vals = sel[g * 4:(g + 1) * 4]
        n_beat = None
        for j in range(4):
            if j == i:
                continue
            bj = beats(vals[j], vals[i], j, i).astype(jnp.int32)
            n_beat = bj if n_beat is None else n_beat + bj
        chosen.append((gidx == g) & (n_beat < 2))
    psum = None
    for e in range(N_EXPERTS):
        pe = jnp.where(chosen[e], probs[e], 0.0)
        psum = pe if psum is None else psum + pe
    return [jnp.where(chosen[e], probs[e] / psum, 0.0) for e in range(N_EXPERTS)]


def _moe_kernel(x_ref, g_ref, sc_ref, sh_ref, gt_ref, wr_ref, rb_ref, wg_ref, wu_ref, wd_ref,
                o_ref, h_s, gate_s, acc_s):
    e = pl.program_id(1)
    tm = x_ref.shape[0]
    lane = lax.broadcasted_iota(jnp.int32, (tm, LANES), 1)

    @pl.when(e == 0)
    def _():
        y = _rms(x_ref[...], g_ref[...])
        h = (y * (1.0 + sc_ref[...]) + sh_ref[...]).astype(BF16)
        h_s[...] = h
        logits = _dot(h, wr_ref[...])
        logits = jnp.where(lane < N_EXPERTS, logits, -jnp.inf)
        mx = jnp.max(logits, axis=-1, keepdims=True)
        ex = jnp.exp(logits - mx)
        probs = ex / jnp.sum(ex, axis=-1, keepdims=True)
        selv = probs + rb_ref[...]
        pcols = [probs[:, i:i + 1] for i in range(N_EXPERTS)]
        scols = [selv[:, i:i + 1] for i in range(N_EXPERTS)]
        gcols = _route(pcols, scols)
        gates = jnp.zeros((tm, LANES), F32)
        for i in range(N_EXPERTS):
            gates = jnp.where(lane == i, gcols[i], gates)
        gate_s[...] = gates
        acc_s[...] = jnp.zeros_like(acc_s)

    h = h_s[...]
    gcol = jnp.sum(jnp.where(lane == e, gate_s[...], 0.0), axis=-1, keepdims=True)
    hg = _dot(h, wg_ref[...])
    hu = _dot(h, wu_ref[...])
    act = hg * jax.nn.sigmoid(hg) * hu * gcol
    acc_s[...] += _bdot(act, wd_ref[...])

    @pl.when(e == N_EXPERTS - 1)
    def _():
        o_ref[...] = x_ref[...] + gt_ref[...] * acc_s[...]


def _moe(x, g, grp, wr, rb, wg, wu, wd, tm):
    row = lambda i, e: (i, 0)
    return pl.pallas_call(
        _moe_kernel,
        out_shape=jax.ShapeDtypeStruct((grp.rows, D_MODEL), F32),
        grid=(grp.rows // tm, N_EXPERTS),
        in_specs=[pl.BlockSpec((tm, D_MODEL), row),
                  pl.BlockSpec((1, D_MODEL), lambda i, e: (0, 0)),
                  grp.mod_spec(4, tm),
                  grp.mod_spec(3, tm),
                  grp.mod_spec(5, tm),
                  pl.BlockSpec((D_MODEL, LANES), lambda i, e: (0, 0)),
                  pl.BlockSpec((1, LANES), lambda i, e: (0, 0)),
                  pl.BlockSpec((None, D_MODEL, D_EXPERT), lambda i, e: (e, 0, 0)),
                  pl.BlockSpec((None, D_MODEL, D_EXPERT), lambda i, e: (e, 0, 0)),
                  pl.BlockSpec((None, D_EXPERT, D_MODEL), lambda i, e: (e, 0, 0))],
        out_specs=pl.BlockSpec((tm, D_MODEL), row),
        scratch_shapes=[pltpu.VMEM((tm, D_MODEL), BF16),
                        pltpu.VMEM((tm, LANES), F32),
                        pltpu.VMEM((tm, D_MODEL), F32)],
        compiler_params=_cparams(("arbitrary", "arbitrary")),
    )(x, g.reshape(1, D_MODEL), grp.mod4, grp.mod4, grp.mod4, wr, rb, wg, wu, wd)


def _block_diag(w):
    eye = jnp.eye(C_BLOCKS, dtype=w.dtype)
    return jnp.einsum("hg,hij->higj", eye, w).reshape(D_BRANCH, D_BRANCH)


def _layer_params(l, p):
    row = lambda a: a[l].reshape(1, -1)
    w_in = p["w_in"][l]
    ones = _block_diag(jnp.ones((A_HEADS, A_HEAD, A_HEAD), F32)).astype(BF16)
    eye = jnp.tile(jnp.eye(A_HEAD, dtype=F32), (1, A_HEADS))
    pad_lo = lambda w: jnp.pad(w, ((0, 64), (0, 0)))
    pad_hi = lambda w: jnp.pad(w, ((64, 0), (0, 0)))
    rw = dict(mu=row(p["a_mu"]), w0=row(p["a_w0"]), w2=pad_lo(p["a_w2"][l]).astype(BF16),
              a0=row(p["a_a0"]), a2=pad_hi(p["a_a2"][l]).astype(BF16), g2=p["a_g2"][l].astype(BF16),
              kk=row(p["a_kk"]), ka=row(p["a_ka"]), rk=row(p["a_rk"]), lng=row(p["a_ln_g"]),
              lnb=row(p["a_ln_b"]), ones=ones, eye=eye)
    lru = dict(cw=p["c_conv_w"][l], cb=row(p["c_conv_b"]),
               wri=jnp.concatenate([_block_diag(p["c_wr"][l]), _block_diag(p["c_wi"][l])], axis=1).astype(BF16),
               br=row(p["c_br"]), bi=row(p["c_bi"]), lam=row(p["c_lam"]))
    return dict(
        w_pa=w_in[:, 0:1792].astype(BF16),
        w_rq=w_in[:, 1792:3840].astype(BF16),
        w_cx=w_in[:, 3840:4864].astype(BF16),
        w_gm=w_in[:, 4864:7936].astype(BF16),
        rw=rw, lru=lru,
        w_branch=p["w_branch"][l].astype(BF16),
        w_out=p["w_out"][l].astype(BF16),
        wg=p["moe_wg"][l].astype(BF16), wu=p["moe_wu"][l].astype(BF16), wd=p["moe_wd"][l].astype(BF16),
        norm_mix=p["norm_mix"][l], norm_ffn=p["norm_ffn"][l])


def _wkv_to_kernel(s):
    b = s.shape[0]
    return s.transpose(0, 2, 1, 3).reshape(b, A_HEAD, D_BRANCH)


def _wkv_from_kernel(s):
    b = s.shape[0]
    return s.reshape(b, A_HEAD, A_HEADS, A_HEAD).transpose(0, 2, 1, 3)


def kernel(x_prompt, x_sample, c_prompt, c_sample, state_rwkv_shift, state_rwkv_wkv, state_ret, state_lru_h, state_lru_conv, norm_mix, norm_ffn, norm_final, ada_w, ada_b, w_in, a_mu, a_w0, a_w2, a_a0, a_a2, a_g2, a_kk, a_ka, a_rk, a_ln_g, a_ln_b, c_conv_w, c_conv_b, c_wr, c_br, c_wi, c_bi, c_lam, w_branch, w_out, w_router, router_bias, moe_wg, moe_wu, moe_wd):
    p = dict(norm_mix=norm_mix, norm_ffn=norm_ffn, w_in=w_in, a_mu=a_mu, a_w0=a_w0, a_w2=a_w2,
             a_a0=a_a0, a_a2=a_a2, a_g2=a_g2, a_kk=a_kk, a_ka=a_ka, a_rk=a_rk, a_ln_g=a_ln_g,
             a_ln_b=a_ln_b, c_conv_w=c_conv_w, c_conv_b=c_conv_b, c_wr=c_wr, c_br=c_br, c_wi=c_wi,
             c_bi=c_bi, c_lam=c_lam, w_branch=w_branch, w_out=w_out, moe_wg=moe_wg, moe_wu=moe_wu,
             moe_wd=moe_wd)
    bp, lp_, _ = x_prompt.shape
    bs, ls, _ = x_sample.shape
    layers = [_layer_params(l, p) for l in range(DEPTH)]
    router = (jnp.pad(w_router, ((0, 0), (0, LANES - N_EXPERTS))).astype(BF16),
              jnp.pad(router_bias, (0, LANES - N_EXPERTS)).reshape(1, LANES))

    n_c = bp + bs
    pad_c = (-n_c) % 16
    c_all = jnp.pad(jnp.concatenate([c_prompt, c_sample], axis=0), ((0, pad_c), (0, 0)))
    mods = _ada(c_all, ada_w, ada_b)

    def run(x, batch, seq, mod, states, pos0, cfg):
        xs = x.reshape(batch * seq, D_MODEL)
        grps = []
        for l in range(DEPTH):
            if cfg["per_token"]:
                m = jnp.repeat(mod[l], seq, axis=0).reshape(batch * seq, 6, D_MODEL)
                m4 = m.transpose(1, 0, 2)[None]
            else:
                m4 = mod[l].reshape(batch, 6, 1, D_MODEL)
            grps.append(_Group(batch, seq, m4))
        return _trunk_layers(xs, grps, states, layers, router, norm_final, pos0, cfg)

    zeros = lambda s: jnp.zeros((DEPTH, bp) + s.shape[2:], x_prompt.dtype)
    st_prompt = (zeros(state_rwkv_shift), zeros(state_rwkv_wkv), zeros(state_ret),
                 zeros(state_lru_h), zeros(state_lru_conv))
    st_sample = (state_rwkv_shift, state_rwkv_wkv, state_ret, state_lru_h, state_lru_conv)
    cfg_p = dict(per_token=False, tm=min(512, lp_), tm_mg=min(256, lp_), tm_mm=min(1024, lp_),
                 rwkv_nb=8, rwkv_lb=64, lru_nb=1, lru_lb=min(256, lp_))
    cfg_s = dict(per_token=True, tm=min(512, bs * ls), tm_mg=min(256, bs * ls), tm_mm=min(512, bs * ls),
                 rwkv_nb=16, rwkv_lb=ls, lru_nb=16, lru_lb=ls)
    y_p, new_p = run(x_prompt, bp, lp_, mods[:, :bp], st_prompt, 0, cfg_p)
    y_s, new_s = run(x_sample, bs, ls, mods[:, bp:bp + bs], st_sample, PAST_LEN, cfg_s)
    return (y_p, y_s) + new_p + new_s


def _trunk_layers(x, grps, states, layers, router, norm_final, pos0, cfg):
    batch, seq = grps[0].batch, grps[0].seq
    tabs = _ret_tables(seq, pos0)
    wr, rb = router
    outs = [[] for _ in range(5)]
    for l, lp in enumerate(layers):
        grp = grps[l]
        h = _norm_mod(x, lp["norm_mix"], grp, 1, 0, cfg["tm"])
        pa = _matmul(h, lp["w_pa"], cfg["tm_mm"], 896)
        rq = _matmul(h, lp["w_rq"], cfg["tm_mm"], 1024)
        cx = _matmul(h, lp["w_cx"], cfg["tm_mm"], 1024)
        o_a, wkv = _rwkv(pa, states[0][l], _wkv_to_kernel(states[1][l]), lp["rw"], batch, seq,
                         cfg["rwkv_nb"], cfg["rwkv_lb"])
        o_b, ret = _retention(rq, states[2][l], tabs, batch, seq)
        o_c, lru_h = _rglru(cx, states[4][l], states[3][l], lp["lru"], batch, seq,
                            cfg["lru_nb"], cfg["lru_lb"])
        x = _merge_out(h, o_a, o_b, o_c, x, grp, lp["w_gm"], lp["w_branch"], lp["w_out"], cfg["tm_mg"])
        x = _moe(x, lp["norm_ffn"], grp, wr, rb, lp["wg"], lp["wu"], lp["wd"], cfg["tm"])
        outs[0].append(pa.reshape(batch, seq, A_PROJ)[:, -1])
        outs[1].append(_wkv_from_kernel(wkv))
        outs[2].append(ret)
        outs[3].append(lru_h)
        outs[4].append(cx.reshape(batch, seq, 2 * D_BRANCH)[:, seq - (CONV_W - 1):, :D_BRANCH])
    y = _final_norm(x, norm_final, cfg["tm"])
    return y.reshape(batch, seq, D_MODEL), tuple(jnp.stack(o) for o in outs)
```

```python
import functools
import math

import numpy as np
import jax
import jax.numpy as jnp
from jax import lax
from jax.experimental import pallas as pl
from jax.experimental.pallas import tpu as pltpu

F32 = jnp.float32
BF16 = jnp.bfloat16

D_MODEL = 1024
DEPTH = 2
PAST_LEN = 16384
D_BRANCH = 512
A_HEAD = 64
A_HEADS = 8
A_PROJ = 1792
A_NORM_EPS = 64e-5
A_KK_EPS = 1e-12
R_HEAD = 128
R_HEADS = 4
R_CHUNK = 128
R_NORM_EPS = 1e-6
ROPE_BASE = 10000.0
C_BLOCK = 64
C_BLOCKS = 8
CONV_W = 4
LRU_C = 8.0
N_EXPERTS = 16
N_GROUPS = 4
EXPERTS_PER_GROUP = 4
D_EXPERT = 512
NORM_EPS = 1e-6

LANES = 128
RET_ROWS = 128
VMEM_LIMIT = 48 * 1024 * 1024


def _cparams(sem):
    return pltpu.CompilerParams(dimension_semantics=sem, vmem_limit_bytes=VMEM_LIMIT)


def _dot(a, b):
    return jnp.dot(a, b, preferred_element_type=F32)


def _bdot(a, b):
    return jnp.dot(a.astype(BF16), b.astype(BF16), preferred_element_type=F32)


def _segsum(x, ones_bf16):
    hi = x.astype(BF16)
    r1 = x - hi.astype(F32)
    mid = r1.astype(BF16)
    lo = (r1 - mid.astype(F32)).astype(BF16)
    return (_dot(lo, ones_bf16) + _dot(mid, ones_bf16)) + _dot(hi, ones_bf16)


def _ada_kernel(c_ref, w_ref, b_ref, o_ref):
    c = c_ref[...]
    s = c * jax.nn.sigmoid(c)
    o_ref[...] = _bdot(s, w_ref[...]) + b_ref[...]


def _ada(c, ada_w, ada_b):
    rows = c.shape[0]
    tn = 1536
    return pl.pallas_call(
        _ada_kernel,
        out_shape=jax.ShapeDtypeStruct((DEPTH, rows, 6 * D_MODEL), F32),
        grid=(DEPTH, 6 * D_MODEL // tn),
        in_specs=[
            pl.BlockSpec((rows, D_MODEL), lambda l, j: (0, 0)),
            pl.BlockSpec((None, D_MODEL, tn), lambda l, j: (l, 0, j)),
            pl.BlockSpec((None, 1, tn), lambda l, j: (l, 0, j)),
        ],
        out_specs=pl.BlockSpec((None, rows, tn), lambda l, j: (l, 0, j)),
        compiler_params=_cparams(("arbitrary", "arbitrary")),
    )(c, ada_w, ada_b.reshape(DEPTH, 1, 6 * D_MODEL))


class _Group:
    def __init__(self, batch, seq, mod4):
        self.batch = batch
        self.seq = seq
        self.rows = batch * seq
        self.mod4 = mod4
        self.per_token = mod4.shape[0] == 1 and mod4.shape[2] != 1

    def mod_spec(self, k, tm):
        if self.per_token:
            return pl.BlockSpec((None, None, tm, D_MODEL), lambda i, *_: (0, k, i, 0))
        seq = self.seq
        return pl.BlockSpec((None, None, 1, D_MODEL), lambda i, *_: ((i * tm) // seq, k, 0, 0))


def _rms(x, g):
    return x * lax.rsqrt(jnp.mean(x * x, axis=-1, keepdims=True) + NORM_EPS) * g


def _norm_mod_kernel(x_ref, g_ref, sc_ref, sh_ref, o_ref):
    y = _rms(x_ref[...], g_ref[...])
    o_ref[...] = (y * (1.0 + sc_ref[...]) + sh_ref[...]).astype(o_ref.dtype)


def _norm_kernel(x_ref, g_ref, o_ref):
    o_ref[...] = _rms(x_ref[...], g_ref[...]).astype(o_ref.dtype)


def _norm_mod(x, g, grp, k_sc, k_sh, tm):
    return pl.pallas_call(
        _norm_mod_kernel,
        out_shape=jax.ShapeDtypeStruct((grp.rows, D_MODEL), BF16),
        grid=(grp.rows // tm,),
        in_specs=[
            pl.BlockSpec((tm, D_MODEL), lambda i: (i, 0)),
            pl.BlockSpec((1, D_MODEL), lambda i: (0, 0)),
            grp.mod_spec(k_sc, tm),
            grp.mod_spec(k_sh, tm),
        ],
        out_specs=pl.BlockSpec((tm, D_MODEL), lambda i: (i, 0)),
        compiler_params=_cparams(("arbitrary",)),
    )(x, g.reshape(1, D_MODEL), grp.mod4, grp.mod4)


def _final_norm(x, g, tm):
    rows = x.shape[0]
    return pl.pallas_call(
        _norm_kernel,
        out_shape=jax.ShapeDtypeStruct((rows, D_MODEL), F32),
        grid=(rows // tm,),
        in_specs=[
            pl.BlockSpec((tm, D_MODEL), lambda i: (i, 0)),
            pl.BlockSpec((1, D_MODEL), lambda i: (0, 0)),
        ],
        out_specs=pl.BlockSpec((tm, D_MODEL), lambda i: (i, 0)),
        compiler_params=_cparams(("arbitrary",)),
    )(x, g.reshape(1, D_MODEL))


def _mm_kernel(a_ref, w_ref, o_ref):
    o_ref[...] = _dot(a_ref[...], w_ref[...])


def _matmul(a, w, tm, tn):
    rows, k = a.shape
    n = w.shape[1]
    return pl.pallas_call(
        _mm_kernel,
        out_shape=jax.ShapeDtypeStruct((rows, n), F32),
        grid=(n // tn, rows // tm),
        in_specs=[
            pl.BlockSpec((tm, k), lambda j, i: (i, 0)),
            pl.BlockSpec((k, tn), lambda j, i: (0, j)),
        ],
        out_specs=pl.BlockSpec((tm, tn), lambda j, i: (i, j)),
        compiler_params=_cparams(("arbitrary", "arbitrary")),
    )(a, w)


def _rwkv_pre(pa, prev, mu_ref, w0_ref, w2_ref, a0_ref, a2_ref, g2_ref, kk_ref, ka_ref, ones):
    pm = pa + (prev - pa) * mu_ref[...]
    r = pm[:, 0:512]
    k = pm[:, 512:1024]
    v = pm[:, 1024:1536]
    xwa = pm[:, 1536:1664]
    xg = pm[:, 1664:1792]
    w_log = -jax.nn.softplus(-(w0_ref[...] + _bdot(jnp.tanh(xwa), w2_ref[...]))) - 0.5
    logw = -jnp.exp(w_log)
    a = jax.nn.sigmoid(a0_ref[...] + _bdot(xwa, a2_ref[...]))
    g = _bdot(jax.nn.sigmoid(xg), g2_ref[...])
    kk = k * kk_ref[...]
    kk = kk * lax.rsqrt(_segsum(kk * kk, ones) + A_KK_EPS)
    k2 = k * (1.0 + (a - 1.0) * ka_ref[...])
    return r, logw, k2, v, -kk, kk * a, g


def _rwkv_post(o, r, k2, v, g, rk_ref, lng_ref, lnb_ref, ones):
    mean = _segsum(o, ones) * (1.0 / A_HEAD)
    oc = o - mean
    var = _segsum(oc * oc, ones) * (1.0 / A_HEAD)
    o = oc * lax.rsqrt(var + A_NORM_EPS) * lng_ref[...] + lnb_ref[...]
    bonus = _segsum(r * k2 * rk_ref[...], ones) * v
    return (o + bonus) * g


_NN = (((1,), (0,)), ((), ()))
_NT = (((1,), (1,)), ((), ()))
_TN = (((0,), (0,)), ((), ()))
RW_CHUNK = 64
RW_PAIRS = A_HEADS // 2


def _bdg(a, b, dims):
    return lax.dot_general(a.astype(BF16), b.astype(BF16), dims, preferred_element_type=F32)


def _rwkv_chunk_kernel(nck,
                       pa_ref, sh0_ref, g0_ref, mu_ref, w0_ref, w2_ref, a0_ref, a2_ref, g2_ref,
                       kk_ref, ka_ref, rk_ref, lng_ref, lnb_ref, ones_ref, tri_ref,
                       o_ref, gout_ref,
                       carry_ref, prev_ref, g_s, wu_s, m_s, o_s):
    c = pl.program_id(1)
    ck = RW_CHUNK
    ones = ones_ref[...]

    @pl.when(c == 0)
    def _():
        carry_ref[...] = sh0_ref[...]
        g_s[...] = g0_ref[...]

    pa = pa_ref[...]
    rows = pa.shape[0]
    prev_ref[...] = pltpu.roll(pa, 1, axis=0)
    prev_ref[pl.ds(0, 1), :] = carry_ref[...]
    carry_ref[...] = pa[rows - 1:rows, :]
    r, logw, k2, v, an, bn, g = _rwkv_pre(pa, prev_ref[...], mu_ref, w0_ref, w2_ref, a0_ref, a2_ref,
                                          g2_ref, kk_ref, ka_ref, ones)

    tri = tri_ref[...]
    hi = logw.astype(BF16)
    r1 = logw - hi.astype(F32)
    mid = r1.astype(BF16)
    lo = (r1 - mid.astype(F32)).astype(BF16)
    cum = (_dot(tri, lo) + _dot(tri, mid)) + _dot(tri, hi)
    cum_last = jnp.concatenate(
        [jnp.broadcast_to(cum[(i + 1) * ck - 1:(i + 1) * ck, :], (ck, D_BRANCH)) for i in range(nck)], axis=0)
    gam = jnp.exp(cum)
    inv = jnp.exp(-cum)
    to_end = jnp.exp(cum_last - cum)
    a_t = an * jnp.exp(cum - logw)
    b_t = bn * inv
    k_t = k2 * inv
    r_t = r * gam
    b_e = bn * to_end
    k_e = k2 * to_end

    lane = lax.broadcasted_iota(jnp.int32, (ck, LANES), 1)
    rowi = lax.broadcasted_iota(jnp.int32, (ck, LANES), 0)
    m0 = lane < A_HEAD
    coli = lane & (A_HEAD - 1)
    strict = rowi > coli
    incl = rowi >= coli
    eye_p = (rowi == coli).astype(F32)
    r128 = lax.broadcasted_iota(jnp.int32, (LANES, LANES), 0)
    c128 = lax.broadcasted_iota(jnp.int32, (LANES, LANES), 1)
    blockmask = (r128 < A_HEAD) == (c128 < A_HEAD)

    def bd(q):
        q = q.astype(BF16)
        z = jnp.zeros_like(q)
        return jnp.concatenate([jnp.where(m0, q, z), jnp.where(m0, z, q)], axis=0)

    for i in range(nck):
        rs = slice(i * ck, (i + 1) * ck)
        for p in range(RW_PAIRS):
            ls = slice(p * LANES, (p + 1) * LANES)
            a_p, r_p, v_p = a_t[rs, ls], r_t[rs, ls], v[rs, ls]
            lhs = jnp.concatenate([a_p, r_p], axis=0).astype(BF16)
            ab = lax.dot_general(lhs, bd(b_t[rs, ls]), _NT, preferred_element_type=F32)
            ak = lax.dot_general(lhs, bd(k_t[rs, ls]), _NT, preferred_element_type=F32)
            l_ab = jnp.where(strict, ab[:ck], 0.0)
            l_ak = jnp.where(strict, ak[:ck], 0.0)
            m_s[i, p, :, 0:LANES] = jnp.where(incl, ab[ck:], 0.0)
            m_s[i, p, :, LANES:2 * LANES] = jnp.where(incl, ak[ck:], 0.0)
            t_inv = eye_p + l_ab
            l_pow = l_ab
            n = 1
            while 2 * n < ck:
                l_pow = _bdg(l_pow, bd(l_pow), _NN)
                t_inv = t_inv + _bdg(t_inv, bd(l_pow), _NN)
                n *= 2
            lak_v = _bdg(l_ak, bd(v_p), _NN)
            wu_s[i, p] = _bdg(t_inv, jnp.concatenate([bd(a_p), bd(lak_v)], axis=1), _NN)

    for i in range(nck):
        rs = slice(i * ck, (i + 1) * ck)
        for p in range(RW_PAIRS):
            ls = slice(p * LANES, (p + 1) * LANES)
            g_p = g_s[p]
            g_bf = g_p.astype(BF16)
            v_p = v[rs, ls]
            wu = wu_s[i, p]
            u = lax.dot_general(wu[:, 0:LANES].astype(BF16), g_bf, _NT, preferred_element_type=F32) \
                + wu[:, LANES:2 * LANES]
            o_p = lax.dot_general(r_t[rs, ls].astype(BF16), g_bf, _NT, preferred_element_type=F32) \
                + _bdg(m_s[i, p], jnp.concatenate([bd(u), bd(v_p)], axis=0), _NN)
            upd = _bdg(jnp.concatenate([u, v_p], axis=0),
                       jnp.concatenate([b_e[rs, ls], k_e[rs, ls]], axis=0), _TN)
            g_s[p] = jnp.where(blockmask, g_p * gam[(i + 1) * ck - 1:(i + 1) * ck, ls] + upd, 0.0)
            o_s[rs, ls] = o_p

    o_ref[...] = _rwkv_post(o_s[...], r, k2, v, g, rk_ref, lng_ref, lnb_ref, ones).astype(o_ref.dtype)
    gout_ref[...] = g_s[...]


def _rwkv_chunked(pa, shift0, s0, prm, batch, seq, nck):
    lb = nck * RW_CHUNK
    nstep = seq // lb
    s0p = s0.reshape(batch, RW_PAIRS, 2, A_HEAD, A_HEAD)
    eye2 = jnp.eye(2, dtype=s0.dtype)
    g0 = jnp.einsum("bpjvk,ji->bpjvik", s0p, eye2).reshape(batch, RW_PAIRS, LANES, LANES)
    tri = np.kron(np.eye(nck), np.tril(np.ones((RW_CHUNK, RW_CHUNK)))).astype(np.float32)
    params = [prm[n] for n in ("mu", "w0", "w2", "a0", "a2", "g2", "kk", "ka", "rk", "lng", "lnb", "ones")]
    params.append(jnp.asarray(tri, BF16))

    def full(arr):
        nd = arr.ndim
        return pl.BlockSpec(arr.shape, lambda i, c: (0,) * nd)

    o, g_out = pl.pallas_call(
        functools.partial(_rwkv_chunk_kernel, nck),
        out_shape=(jax.ShapeDtypeStruct((batch * seq, D_BRANCH), BF16),
                   jax.ShapeDtypeStruct((batch, RW_PAIRS, LANES, LANES), F32)),
        grid=(batch, nstep),
        in_specs=[pl.BlockSpec((lb, A_PROJ), lambda i, c: (i * nstep + c, 0)),
                  pl.BlockSpec((None, 1, A_PROJ), lambda i, c: (i, 0, 0)),
                  pl.BlockSpec((None, RW_PAIRS, LANES, LANES), lambda i, c: (i, 0, 0, 0))]
                 + [full(p) for p in params],
        out_specs=(pl.BlockSpec((lb, D_BRANCH), lambda i, c: (i * nstep + c, 0)),
                   pl.BlockSpec((None, RW_PAIRS, LANES, LANES), lambda i, c: (i, 0, 0, 0))),
        scratch_shapes=[pltpu.VMEM((1, A_PROJ), F32),
                        pltpu.VMEM((lb, A_PROJ), F32),
                        pltpu.VMEM((RW_PAIRS, LANES, LANES), F32),
                        pltpu.VMEM((nck, RW_PAIRS, RW_CHUNK, 2 * LANES), F32),
                        pltpu.VMEM((nck, RW_PAIRS, RW_CHUNK, 2 * LANES), F32),
                        pltpu.VMEM((lb, D_BRANCH), F32)],
        compiler_params=_cparams(("arbitrary", "arbitrary")),
    )(pa, shift0.reshape(batch, 1, A_PROJ), g0, *params)
    g5 = g_out.reshape(batch, RW_PAIRS, 2, A_HEAD, 2, A_HEAD)
    s_new = jnp.stack([g5[:, :, 0, :, 0, :], g5[:, :, 1, :, 1, :]], axis=2)
    return o, s_new.reshape(batch, A_HEADS, A_HEAD, A_HEAD)


def _rwkv_kernel(nb, lb, three_d,
                 pa_ref, sh0_ref, s0_ref, mu_ref, w0_ref, w2_ref, a0_ref, a2_ref, g2_ref,
                 kk_ref, ka_ref, rk_ref, lng_ref, lnb_ref, ones_ref, eye_ref,
                 o_ref, sout_ref,
                 carry_ref, prev_ref, s_ref, r_s, w_s, k_s, v_s, an_s, bn_s, o_s):
    c = pl.program_id(1)
    rows = nb * lb
    ones = ones_ref[...]

    @pl.when(c == 0)
    def _():
        carry_ref[...] = sh0_ref[...]
        s_ref[...] = s0_ref[...]

    pa = pa_ref[...]
    if three_d:
        pa = pa.reshape(rows, A_PROJ)
    prev_ref[...] = pltpu.roll(pa, 1, axis=0)
    for b in range(nb):
        prev_ref[pl.ds(b * lb, 1), :] = carry_ref[pl.ds(b, 1), :]
        carry_ref[pl.ds(b, 1), :] = pa[b * lb + lb - 1:b * lb + lb, :]
    r, logw, k2, v, an, bn, g = _rwkv_pre(pa, prev_ref[...], mu_ref, w0_ref, w2_ref, a0_ref, a2_ref,
                                          g2_ref, kk_ref, ka_ref, ones)
    r_s[...] = r
    w_s[...] = jnp.exp(logw)
    k_s[...] = k2
    v_s[...] = v
    an_s[...] = an
    bn_s[...] = bn

    eye = eye_ref[...]

    def step(t, carry):
        for b in range(nb):
            row = b * lb + t
            a_t = an_s[pl.ds(row, 1), :]
            b_t = bn_s[pl.ds(row, 1), :]
            w_t = w_s[pl.ds(row, 1), :]
            k_t = k_s[pl.ds(row, 1), :]
            v_t = v_s[pl.ds(row, 1), :]
            r_t = r_s[pl.ds(row, 1), :]
            s = s_ref[b]
            sa = _segsum(s * a_t, ones)
            vcol = _segsum(eye * v_t, ones)
            s = s * w_t + sa * b_t + vcol * k_t
            s_ref[b] = s
            out = _segsum(s * r_t, ones)
            o_s[pl.ds(row, 1), :] = jnp.sum(eye * out, axis=0, keepdims=True)
        return carry

    lax.fori_loop(0, lb, step, 0)

    res = _rwkv_post(o_s[...], r, k2, v, g, rk_ref, lng_ref, lnb_ref, ones).astype(o_ref.dtype)
    if three_d:
        res = res.reshape(nb, lb, D_BRANCH)
    o_ref[...] = res
    sout_ref[...] = s_ref[...]


def _rwkv(pa, shift0, s0, prm, batch, seq, nb, lb):
    three_d = nb > 1 and lb % 8 == 0 and seq != lb
    nchunk = seq // lb
    rows = nb * lb
    if three_d:
        pa_in = pa.reshape(batch, seq, A_PROJ)
        pa_spec = pl.BlockSpec((nb, lb, A_PROJ), lambda i, c: (i, c, 0))
        o_shape = jax.ShapeDtypeStruct((batch, seq, D_BRANCH), BF16)
        o_spec = pl.BlockSpec((nb, lb, D_BRANCH), lambda i, c: (i, c, 0))
    else:
        assert nb == 1 or nchunk == 1
        pa_in = pa
        pa_spec = pl.BlockSpec((rows, A_PROJ), lambda i, c: (i * nchunk + c, 0))
        o_shape = jax.ShapeDtypeStruct((batch * seq, D_BRANCH), BF16)
        o_spec = pl.BlockSpec((rows, D_BRANCH), lambda i, c: (i * nchunk + c, 0))

    def full(arr):
        nd = arr.ndim
        return pl.BlockSpec(arr.shape, lambda i, c: (0,) * nd)

    params = [prm[n] for n in ("mu", "w0", "w2", "a0", "a2", "g2", "kk", "ka", "rk", "lng", "lnb",
                               "ones", "eye")]
    o, s_out = pl.pallas_call(
        functools.partial(_rwkv_kernel, nb, lb, three_d),
        out_shape=(o_shape, jax.ShapeDtypeStruct((batch, A_HEAD, D_BRANCH), F32)),
        grid=(batch // nb, nchunk),
        in_specs=[pa_spec,
                  pl.BlockSpec((nb, A_PROJ), lambda i, c: (i, 0)),
                  pl.BlockSpec((nb, A_HEAD, D_BRANCH), lambda i, c: (i, 0, 0))]
                 + [full(p) for p in params],
        out_specs=(o_spec, pl.BlockSpec((nb, A_HEAD, D_BRANCH), lambda i, c: (i, 0, 0))),
        scratch_shapes=[pltpu.VMEM((nb, A_PROJ), F32),
                        pltpu.VMEM((rows, A_PROJ), F32),
                        pltpu.VMEM((nb, A_HEAD, D_BRANCH), F32)]
                       + [pltpu.VMEM((rows, D_BRANCH), F32) for _ in range(7)],
        compiler_params=_cparams(("arbitrary", "arbitrary")),
    )(pa_in, shift0, s0, *params)
    return o.reshape(batch * seq, D_BRANCH), s_out


def _ret_kernel(nb, lb,
                rq_ref, cos_ref, sin_ref, dm_ref, kd_ref, qd_ref, cd_ref, s0_ref,
                o_ref, sout_ref, s_ref):
    c = pl.program_id(1)

    @pl.when(c == 0)
    def _():
        s_ref[...] = s0_ref[...]

    cos = cos_ref[...]
    sin = sin_ref[...]
    scale = R_HEAD ** -0.5
    half = R_HEAD // 2
    row8 = lax.broadcasted_iota(jnp.int32, (8, R_HEAD), 0)
    for h in range(R_HEADS):
        lo, hi = h * R_HEAD, (h + 1) * R_HEAD
        q = rq_ref[:, lo:hi]
        k = rq_ref[:, 512 + lo:512 + hi]
        v = rq_ref[:, 1024 + lo:1024 + hi]
        gate = rq_ref[:, 1536 + lo:1536 + hi]
        qh = q * cos + pltpu.roll(q, half, axis=1) * sin
        kh = (k * cos + pltpu.roll(k, half, axis=1) * sin) * scale
        scores = lax.dot_general(qh.astype(BF16), kh.astype(BF16), (((1,), (1,)), ((), ())),
                                 preferred_element_type=F32) * dm_ref[h]
        o = _bdot(scores, v)
        qd = qh * qd_ref[:, lo:hi]
        ku = kh * kd_ref[:, lo:hi]
        cd = cd_ref[h]
        if nb == 1:
            s = s_ref[0, h]
            o = o + _bdot(qd, s)
            upd = lax.dot_general(ku.astype(BF16), v.astype(BF16), (((0,), (0,)), ((), ())),
                                  preferred_element_type=F32)
            s_ref[0, h] = s * cd + upd
        else:
            per_tile = 8 // lb
            inter = []
            for i in range(RET_ROWS // 8):
                qd_t = qd[i * 8:(i + 1) * 8, :]
                ku_t = ku[i * 8:(i + 1) * 8, :]
                v_t = v[i * 8:(i + 1) * 8, :].astype(BF16)
                acc = None
                for j in range(per_tile):
                    b = i * per_tile + j
                    m = (row8 >= j * lb) & (row8 < (j + 1) * lb)
                    s = s_ref[b, h]
                    part = _bdot(jnp.where(m, qd_t, 0.0), s)
                    acc = part if acc is None else acc + part
                    upd = lax.dot_general(jnp.where(m, ku_t, 0.0).astype(BF16), v_t,
                                          (((0,), (0,)), ((), ())), preferred_element_type=F32)
                    s_ref[b, h] = s * cd + upd
                inter.append(acc)
            o = o + jnp.concatenate(inter, axis=0)
        oc = o - jnp.mean(o, axis=-1, keepdims=True)
        on = oc * lax.rsqrt(jnp.mean(oc * oc, axis=-1, keepdims=True) + R_NORM_EPS)
        o_ref[:, lo:hi] = (gate * jax.nn.sigmoid(gate) * on).astype(o_ref.dtype)
    sout_ref[...] = s_ref[...]


def _retention(rq, s0, tabs, batch, seq):
    lb = min(R_CHUNK, seq)
    nb = RET_ROWS // lb
    nchunk = seq // lb
    cos, sin, dm, kd, qd, cd = tabs
    ntab = cos.shape[0] // RET_ROWS

    def const(arr):
        nd = arr.ndim
        return pl.BlockSpec(arr.shape, lambda i, c: (0,) * nd)

    tab_idx = (lambda i, c: (c, 0)) if ntab > 1 else (lambda i, c: (0, 0))
    return pl.pallas_call(
        functools.partial(_ret_kernel, nb, lb),
        out_shape=(jax.ShapeDtypeStruct((batch * seq, D_BRANCH), BF16),
                   jax.ShapeDtypeStruct((batch, R_HEADS, R_HEAD, R_HEAD), F32)),
        grid=(batch // nb, nchunk),
        in_specs=[pl.BlockSpec((RET_ROWS, 4 * D_BRANCH), lambda i, c: (i * nchunk + c, 0)),
                  pl.BlockSpec((RET_ROWS, R_HEAD), tab_idx),
                  pl.BlockSpec((RET_ROWS, R_HEAD), tab_idx),
                  const(dm), const(kd), const(qd), const(cd),
                  pl.BlockSpec((nb, R_HEADS, R_HEAD, R_HEAD), lambda i, c: (i, 0, 0, 0))],
        out_specs=(pl.BlockSpec((RET_ROWS, D_BRANCH), lambda i, c: (i * nchunk + c, 0)),
                   pl.BlockSpec((nb, R_HEADS, R_HEAD, R_HEAD), lambda i, c: (i, 0, 0, 0))),
        scratch_shapes=[pltpu.VMEM((nb, R_HEADS, R_HEAD, R_HEAD), F32)],
        compiler_params=_cparams(("arbitrary", "arbitrary")),
    )(rq, cos, sin, dm, kd, qd, cd, s0)


def _ret_tables(seq, pos0):
    lb = min(R_CHUNK, seq)
    nb = RET_ROWS // lb
    half = R_HEAD // 2
    pos = pos0 + jnp.arange(seq, dtype=jnp.int32)
    inv = ROPE_BASE ** (-jnp.arange(half, dtype=F32) / half)
    ang = pos.astype(F32)[:, None] * inv[None, :]
    cos, sin = jnp.cos(ang), jnp.sin(ang)
    cos2 = jnp.concatenate([cos, cos], axis=-1)
    sin2 = jnp.concatenate([-sin, sin], axis=-1)
    if nb > 1:
        cos2 = jnp.tile(cos2, (nb, 1))
        sin2 = jnp.tile(sin2, (nb, 1))
    log_g = jnp.log1p(-jnp.exp2(-5.0 - jnp.arange(R_HEADS, dtype=F32)))
    idx = jnp.arange(lb, dtype=F32)
    diff = idx[:, None] - idx[None, :]
    dmask = jnp.where(diff >= 0, jnp.exp(log_g[:, None, None] * jnp.maximum(diff, 0.0)), 0.0)
    if nb > 1:
        dmask = jnp.einsum("ab,hij->haibj", jnp.eye(nb, dtype=F32), dmask).reshape(
            R_HEADS, RET_ROWS, RET_ROWS)
    k_decay = jnp.exp(log_g[None, :] * (lb - 1.0 - idx)[:, None])
    q_decay = jnp.exp(log_g[None, :] * (idx + 1.0)[:, None])
    kd = jnp.tile(jnp.repeat(k_decay, R_HEAD, axis=1), (nb, 1))
    qd = jnp.tile(jnp.repeat(q_decay, R_HEAD, axis=1), (nb, 1))
    cd = jnp.broadcast_to(jnp.exp(log_g * lb)[:, None, None], (R_HEADS, 1, R_HEAD))
    return cos2, sin2, dmask, kd, qd, cd


def _lru_kernel(nb, lb, three_d,
                cx_ref, conv0_ref, h0_ref, cw_ref, cb_ref, wri_ref, br_ref, bi_ref, lam_ref,
                o_ref, hout_ref,
                cc_ref, hc_ref, x1_ref, x2_ref, x3_ref, a_ref, b_ref):
    c = pl.program_id(1)
    rows = nb * lb

    @pl.when(c == 0)
    def _():
        cc_ref[...] = conv0_ref[...]
        hc_ref[...] = h0_ref[...]

    cx = cx_ref[...]
    if three_d:
        cx = cx.reshape(rows, 2 * D_BRANCH)
    xb = cx[:, :D_BRANCH]
    gb = cx[:, D_BRANCH:]
    x1_ref[...] = pltpu.roll(xb, 1, axis=0)
    x2_ref[...] = pltpu.roll(xb, 2, axis=0)
    x3_ref[...] = pltpu.roll(xb, 3, axis=0)
    for b in range(nb):
        r0 = b * lb
        c0 = cc_ref[b, 0:1, :]
        c1 = cc_ref[b, 1:2, :]
        c2 = cc_ref[b, 2:3, :]
        x1_ref[pl.ds(r0, 1), :] = c2
        x2_ref[pl.ds(r0, 1), :] = c1
        x2_ref[pl.ds(r0 + 1, 1), :] = c2
        x3_ref[pl.ds(r0, 1), :] = c0
        x3_ref[pl.ds(r0 + 1, 1), :] = c1
        x3_ref[pl.ds(r0 + 2, 1), :] = c2
        cc_ref[b] = xb[r0 + lb - 3:r0 + lb, :]
    cw = cw_ref[...]
    xc = cb_ref[...] + (((x3_ref[...] * cw[0:1, :] + x2_ref[...] * cw[1:2, :]) + x1_ref[...] * cw[2:3, :])
                        + xb * cw[3:4, :])
    ri = _bdot(xc, wri_ref[...])
    r = jax.nn.sigmoid(ri[:, :D_BRANCH] + br_ref[...])
    i = jax.nn.sigmoid(ri[:, D_BRANCH:] + bi_ref[...])
    log_a = LRU_C * r * jax.nn.log_sigmoid(lam_ref[...])
    a = jnp.exp(log_a)
    bb = jnp.sqrt(-jnp.tanh(log_a) * (a * a + 1.0)) * (i * xc)
    a_ref[...] = a
    b_ref[...] = bb
    for b in range(nb):
        r0 = b * lb
        b_ref[pl.ds(r0, 1), :] = bb[r0:r0 + 1, :] + a[r0:r0 + 1, :] * hc_ref[b]
    a = a_ref[...]
    bb = b_ref[...]
    t_idx = lax.broadcasted_iota(jnp.int32, (rows, D_BRANCH), 0) % lb
    s = 1
    while s < lb:
        keep = t_idx >= s
        a_sh = jnp.where(keep, pltpu.roll(a, s, axis=0), 1.0)
        b_sh = jnp.where(keep, pltpu.roll(bb, s, axis=0), 0.0)
        bb = a * b_sh + bb
        a = a * a_sh
        s *= 2
    h = bb
    for b in range(nb):
        r0 = b * lb
        hc_ref[b] = h[r0 + lb - 1:r0 + lb, :]
    res = (h * jax.nn.gelu(gb)).astype(o_ref.dtype)
    if three_d:
        res = res.reshape(nb, lb, D_BRANCH)
    o_ref[...] = res
    hout_ref[...] = hc_ref[...]


def _rglru(cx, conv0, h0, prm, batch, seq, nb, lb):
    nchunk = seq // lb
    rows = nb * lb
    three_d = nb > 1 and nchunk > 1
    if three_d:
        cx_in = cx.reshape(batch, seq, 2 * D_BRANCH)
        cx_spec = pl.BlockSpec((nb, lb, 2 * D_BRANCH), lambda i, c: (i, c, 0))
        o_shape = jax.ShapeDtypeStruct((batch, seq, D_BRANCH), BF16)
        o_spec = pl.BlockSpec((nb, lb, D_BRANCH), lambda i, c: (i, c, 0))
    else:
        cx_in = cx
        cx_spec = pl.BlockSpec((rows, 2 * D_BRANCH), lambda i, c: (i * nchunk + c, 0))
        o_shape = jax.ShapeDtypeStruct((batch * seq, D_BRANCH), BF16)
        o_spec = pl.BlockSpec((rows, D_BRANCH), lambda i, c: (i * nchunk + c, 0))

    def full(arr):
        nd = arr.ndim
        return pl.BlockSpec(arr.shape, lambda i, c: (0,) * nd)

    params = [prm[n] for n in ("cw", "cb", "wri", "br", "bi", "lam")]
    o, h_out = pl.pallas_call(
        functools.partial(_lru_kernel, nb, lb, three_d),
        out_shape=(o_shape, jax.ShapeDtypeStruct((batch, 1, D_BRANCH), F32)),
        grid=(batch // nb, nchunk),
        in_specs=[cx_spec,
                  pl.BlockSpec((nb, CONV_W - 1, D_BRANCH), lambda i, c: (i, 0, 0)),
                  pl.BlockSpec((nb, 1, D_BRANCH), lambda i, c: (i, 0, 0))]
                 + [full(p) for p in params],
        out_specs=(o_spec, pl.BlockSpec((nb, 1, D_BRANCH), lambda i, c: (i, 0, 0))),
        scratch_shapes=[pltpu.VMEM((nb, CONV_W - 1, D_BRANCH), F32),
                        pltpu.VMEM((nb, 1, D_BRANCH), F32)]
                       + [pltpu.VMEM((rows, D_BRANCH), F32) for _ in range(5)],
        compiler_params=_cparams(("arbitrary", "arbitrary")),
    )(cx_in, conv0, h0.reshape(batch, 1, D_BRANCH), *params)
    return o.reshape(batch * seq, D_BRANCH), h_out.reshape(batch, D_BRANCH)


def _merge_kernel(h_ref, oa_ref, ob_ref, oc_ref, x_ref, gt_ref, wgm_ref, wb_ref, wout_ref, o_ref):
    gm = _dot(h_ref[...], wgm_ref[...])
    merged = None
    for n, br_ref in enumerate((oa_ref, ob_ref, oc_ref)):
        br = _dot(br_ref[...], wb_ref[n])
        term = jax.nn.sigmoid(gm[:, n * D_MODEL:(n + 1) * D_MODEL]) * br
        merged = term if merged is None else merged + term
    y = _bdot(merged, wout_ref[...])
    o_ref[...] = x_ref[...] + gt_ref[...] * y


def _merge_out(h, oa, ob, oc, x, grp, wgm, wb, wout, tm):
    row = lambda i: (i, 0)
    return pl.pallas_call(
        _merge_kernel,
        out_shape=jax.ShapeDtypeStruct((grp.rows, D_MODEL), F32),
        grid=(grp.rows // tm,),
        in_specs=[pl.BlockSpec((tm, D_MODEL), row),
                  pl.BlockSpec((tm, D_BRANCH), row),
                  pl.BlockSpec((tm, D_BRANCH), row),
                  pl.BlockSpec((tm, D_BRANCH), row),
                  pl.BlockSpec((tm, D_MODEL), row),
                  grp.mod_spec(2, tm),
                  pl.BlockSpec(wgm.shape, lambda i: (0, 0)),
                  pl.BlockSpec(wb.shape, lambda i: (0, 0, 0)),
                  pl.BlockSpec(wout.shape, lambda i: (0, 0))],
        out_specs=pl.BlockSpec((tm, D_MODEL), row),
        compiler_params=_cparams(("arbitrary",)),
    )(h, oa, ob, oc, x, grp.mod4, wgm, wb, wout)


def _route(probs, sel):
    def beats(vj, vi, j, i):
        return (vj > vi) | ((vj == vi) & (j < i)) if j < i else (vj > vi)

    grp_score = []
    for g in range(N_GROUPS):
        vals = sel[g * 4:(g + 1) * 4]
        best = None
        for i in range(4):
            for j in range(i + 1, 4):
                sm = vals[i] + vals[j]
                best = sm if best is None else jnp.maximum(best, sm)
        grp_score.append(best)
    top = grp_score[0]
    gidx = jnp.zeros_like(top, dtype=jnp.int32)
    for g in range(1, N_GROUPS):
        better = grp_score[g] > top
        top = jnp.where(better, grp_score[g], top)
        gidx = jnp.where(better, g, gidx)
    chosen = []
    for e in range(N_EXPERTS):
        g, i = divmod(e, 4)
        vals = sel[g * 4:(g + 1) * 4]
        n_beat = None
        for j in range(4):
            if j == i:
                continue
            bj = beats(vals[j], vals[i], j, i).astype(jnp.int32)
            n_beat = bj if n_beat is None else n_beat + bj
        chosen.append((gidx == g) & (n_beat < 2))
    psum = None
    for e in range(N_EXPERTS):
        pe = jnp.where(chosen[e], probs[e], 0.0)
        psum = pe if psum is None else psum + pe
    return [jnp.where(chosen[e], probs[e] / psum, 0.0) for e in range(N_EXPERTS)]


def _moe_kernel(x_ref, g_ref, sc_ref, sh_ref, gt_ref, wr_ref, rb_ref, wg_ref, wu_ref, wd_ref,
                o_ref, h_s, gate_s, acc_s):
    e = pl.program_id(1)
    tm = x_ref.shape[0]
    lane = lax.broadcasted_iota(jnp.int32, (tm, LANES), 1)

    @pl.when(e == 0)
    def _():
        y = _rms(x_ref[...], g_ref[...])
        h = (y * (1.0 + sc_ref[...]) + sh_ref[...]).astype(BF16)
        h_s[...] = h
        logits = _dot(h, wr_ref[...])
        logits = jnp.where(lane < N_EXPERTS, logits, -jnp.inf)
        mx = jnp.max(logits, axis=-1, keepdims=True)
        ex = jnp.exp(logits - mx)
        probs = ex / jnp.sum(ex, axis=-1, keepdims=True)
        selv = probs + rb_ref[...]
        pcols = [probs[:, i:i + 1] for i in range(N_EXPERTS)]
        scols = [selv[:, i:i + 1] for i in range(N_EXPERTS)]
        gcols = _route(pcols, scols)
        gates = jnp.zeros((tm, LANES), F32)
        for i in range(N_EXPERTS):
            gates = jnp.where(lane == i, gcols[i], gates)
        gate_s[...] = gates
        acc_s[...] = jnp.zeros_like(acc_s)

    h = h_s[...]
    gcol = jnp.sum(jnp.where(lane == e, gate_s[...], 0.0), axis=-1, keepdims=True)
    hg = _dot(h, wg_ref[...])
    hu = _dot(h, wu_ref[...])
    act = hg * jax.nn.sigmoid(hg) * hu * gcol
    acc_s[...] += _bdot(act, wd_ref[...])

    @pl.when(e == N_EXPERTS - 1)
    def _():
        o_ref[...] = x_ref[...] + gt_ref[...] * acc_s[...]


def _moe(x, g, grp, wr, rb, wg, wu, wd, tm):
    row = lambda i, e: (i, 0)
    return pl.pallas_call(
        _moe_kernel,
        out_shape=jax.ShapeDtypeStruct((grp.rows, D_MODEL), F32),
        grid=(grp.rows // tm, N_EXPERTS),
        in_specs=[pl.BlockSpec((tm, D_MODEL), row),
                  pl.BlockSpec((1, D_MODEL), lambda i, e: (0, 0)),
                  grp.mod_spec(4, tm),
                  grp.mod_spec(3, tm),
                  grp.mod_spec(5, tm),
                  pl.BlockSpec((D_MODEL, LANES), lambda i, e: (0, 0)),
                  pl.BlockSpec((1, LANES), lambda i, e: (0, 0)),
                  pl.BlockSpec((None, D_MODEL, D_EXPERT), lambda i, e: (e, 0, 0)),
                  pl.BlockSpec((None, D_MODEL, D_EXPERT), lambda i, e: (e, 0, 0)),
                  pl.BlockSpec((None, D_EXPERT, D_MODEL), lambda i, e: (e, 0, 0))],
        out_specs=pl.BlockSpec((tm, D_MODEL), row),
        scratch_shapes=[pltpu.VMEM((tm, D_MODEL), BF16),
                        pltpu.VMEM((tm, LANES), F32),
                        pltpu.VMEM((tm, D_MODEL), F32)],
        compiler_params=_cparams(("arbitrary", "arbitrary")),
    )(x, g.reshape(1, D_MODEL), grp.mod4, grp.mod4, grp.mod4, wr, rb, wg, wu, wd)


def _block_diag(w):
    eye = jnp.eye(C_BLOCKS, dtype=w.dtype)
    return jnp.einsum("hg,hij->higj", eye, w).reshape(D_BRANCH, D_BRANCH)


def _layer_params(l, p):
    row = lambda a: a[l].reshape(1, -1)
    w_in = p["w_in"][l]
    ones = _block_diag(jnp.ones((A_HEADS, A_HEAD, A_HEAD), F32)).astype(BF16)
    eye = jnp.tile(jnp.eye(A_HEAD, dtype=F32), (1, A_HEADS))
    pad_lo = lambda w: jnp.pad(w, ((0, 64), (0, 0)))
    pad_hi = lambda w: jnp.pad(w, ((64, 0), (0, 0)))
    rw = dict(mu=row(p["a_mu"]), w0=row(p["a_w0"]), w2=pad_lo(p["a_w2"][l]).astype(BF16),
              a0=row(p["a_a0"]), a2=pad_hi(p["a_a2"][l]).astype(BF16), g2=p["a_g2"][l].astype(BF16),
              kk=row(p["a_kk"]), ka=row(p["a_ka"]), rk=row(p["a_rk"]), lng=row(p["a_ln_g"]),
              lnb=row(p["a_ln_b"]), ones=ones, eye=eye)
    lru = dict(cw=p["c_conv_w"][l], cb=row(p["c_conv_b"]),
               wri=jnp.concatenate([_block_diag(p["c_wr"][l]), _block_diag(p["c_wi"][l])], axis=1).astype(BF16),
               br=row(p["c_br"]), bi=row(p["c_bi"]), lam=row(p["c_lam"]))
    return dict(
        w_pa=w_in[:, 0:1792].astype(BF16),
        w_rq=w_in[:, 1792:3840].astype(BF16),
        w_cx=w_in[:, 3840:4864].astype(BF16),
        w_gm=w_in[:, 4864:7936].astype(BF16),
        rw=rw, lru=lru,
        w_branch=p["w_branch"][l].astype(BF16),
        w_out=p["w_out"][l].astype(BF16),
        wg=p["moe_wg"][l].astype(BF16), wu=p["moe_wu"][l].astype(BF16), wd=p["moe_wd"][l].astype(BF16),
        norm_mix=p["norm_mix"][l], norm_ffn=p["norm_ffn"][l])


def _wkv_to_kernel(s):
    b = s.shape[0]
    return s.transpose(0, 2, 1, 3).reshape(b, A_HEAD, D_BRANCH)


def _wkv_from_kernel(s):
    b = s.shape[0]
    return s.reshape(b, A_HEAD, A_HEADS, A_HEAD).transpose(0, 2, 1, 3)


def kernel(x_prompt, x_sample, c_prompt, c_sample, state_rwkv_shift, state_rwkv_wkv, state_ret, state_lru_h, state_lru_conv, norm_mix, norm_ffn, norm_final, ada_w, ada_b, w_in, a_mu, a_w0, a_w2, a_a0, a_a2, a_g2, a_kk, a_ka, a_rk, a_ln_g, a_ln_b, c_conv_w, c_conv_b, c_wr, c_br, c_wi, c_bi, c_lam, w_branch, w_out, w_router, router_bias, moe_wg, moe_wu, moe_wd):
    p = dict(norm_mix=norm_mix, norm_ffn=norm_ffn, w_in=w_in, a_mu=a_mu, a_w0=a_w0, a_w2=a_w2,
             a_a0=a_a0, a_a2=a_a2, a_g2=a_g2, a_kk=a_kk, a_ka=a_ka, a_rk=a_rk, a_ln_g=a_ln_g,
             a_ln_b=a_ln_b, c_conv_w=c_conv_w, c_conv_b=c_conv_b, c_wr=c_wr, c_br=c_br, c_wi=c_wi,
             c_bi=c_bi, c_lam=c_lam, w_branch=w_branch, w_out=w_out, moe_wg=moe_wg, moe_wu=moe_wu,
             moe_wd=moe_wd)
    bp, lp_, _ = x_prompt.shape
    bs, ls, _ = x_sample.shape
    layers = [_layer_params(l, p) for l in range(DEPTH)]
    router = (jnp.pad(w_router, ((0, 0), (0, LANES - N_EXPERTS))).astype(BF16),
              jnp.pad(router_bias, (0, LANES - N_EXPERTS)).reshape(1, LANES))

    n_c = bp + bs
    pad_c = (-n_c) % 16
    c_all = jnp.pad(jnp.concatenate([c_prompt, c_sample], axis=0), ((0, pad_c), (0, 0)))
    mods = _ada(c_all, ada_w, ada_b)

    def run(x, batch, seq, mod, states, pos0, cfg):
        xs = x.reshape(batch * seq, D_MODEL)
        grps = []
        for l in range(DEPTH):
            if cfg["per_token"]:
                m = jnp.repeat(mod[l], seq, axis=0).reshape(batch * seq, 6, D_MODEL)
                m4 = m.transpose(1, 0, 2)[None]
            else:
                m4 = mod[l].reshape(batch, 6, 1, D_MODEL)
            grps.append(_Group(batch, seq, m4))
        return _trunk_layers(xs, grps, states, layers, router, norm_final, pos0, cfg)

    zeros = lambda s: jnp.zeros((DEPTH, bp) + s.shape[2:], x_prompt.dtype)
    st_prompt = (zeros(state_rwkv_shift), zeros(state_rwkv_wkv), zeros(state_ret),
                 zeros(state_lru_h), zeros(state_lru_conv))
    st_sample = (state_rwkv_shift, state_rwkv_wkv, state_ret, state_lru_h, state_lru_conv)
    cfg_p = dict(per_token=False, tm=min(512, lp_), tm_mg=min(256, lp_), tm_mm=min(1024, lp_),
                 rwkv_nb=8, rwkv_lb=64, lru_nb=1, lru_lb=min(256, lp_))
    cfg_s = dict(per_token=True, tm=min(512, bs * ls), tm_mg=min(256, bs * ls), tm_mm=min(512, bs * ls),
                 rwkv_nb=16, rwkv_lb=ls, lru_nb=16, lru_lb=ls)
    y_p, new_p = run(x_prompt, bp, lp_, mods[:, :bp], st_prompt, 0, cfg_p)
    y_s, new_s = run(x_sample, bs, ls, mods[:, bp:bp + bs], st_sample, PAST_LEN, cfg_s)
    return (y_p, y_s) + new_p + new_s


def _trunk_layers(x, grps, states, layers, router, norm_final, pos0, cfg):
    batch, seq = grps[0].batch, grps[0].seq
    tabs = _ret_tables(seq, pos0)
    wr, rb = router
    outs = [[] for _ in range(5)]
    for l, lp in enumerate(layers):
        grp = grps[l]
        h = _norm_mod(x, lp["norm_mix"], grp, 1, 0, cfg["tm"])
        pa = _matmul(h, lp["w_pa"], cfg["tm_mm"], 896)
        rq = _matmul(h, lp["w_rq"], cfg["tm_mm"], 1024)
        cx = _matmul(h, lp["w_cx"], cfg["tm_mm"], 1024)
        if seq % RW_CHUNK == 0:
            o_a, wkv = _rwkv_chunked(pa, states[0][l], states[1][l], lp["rw"], batch, seq,
                                     min(4, seq // RW_CHUNK))
        else:
            o_a, wkv = _rwkv(pa, states[0][l], _wkv_to_kernel(states[1][l]), lp["rw"], batch, seq,
                             cfg["rwkv_nb"], cfg["rwkv_lb"])
            wkv = _wkv_from_kernel(wkv)
        o_b, ret = _retention(rq, states[2][l], tabs, batch, seq)
        o_c, lru_h = _rglru(cx, states[4][l], states[3][l], lp["lru"], batch, seq,
                            cfg["lru_nb"], cfg["lru_lb"])
        x = _merge_out(h, o_a, o_b, o_c, x, grp, lp["w_gm"], lp["w_branch"], lp["w_out"], cfg["tm_mg"])
        x = _moe(x, lp["norm_ffn"], grp, wr, rb, lp["wg"], lp["wu"], lp["wd"], cfg["tm"])
        outs[0].append(pa.reshape(batch, seq, A_PROJ)[:, -1])
        outs[1].append(wkv)
        outs[2].append(ret)
        outs[3].append(lru_h)
        outs[4].append(cx.reshape(batch, seq, 2 * D_BRANCH)[:, seq - (CONV_W - 1):, :D_BRANCH])
    y = _final_norm(x, norm_final, cfg["tm"])
    return y.reshape(batch, seq, D_MODEL), tuple(jnp.stack(o) for o in outs)
```

```python
import functools
import math

import numpy as np
import jax
import jax.numpy as jnp
from jax import lax
from jax.experimental import pallas as pl
from jax.experimental.pallas import tpu as pltpu

F32 = jnp.float32
BF16 = jnp.bfloat16

D_MODEL = 1024
DEPTH = 2
PAST_LEN = 16384
D_BRANCH = 512
A_HEAD = 64
A_HEADS = 8
A_PROJ = 1792
A_NORM_EPS = 64e-5
A_KK_EPS = 1e-12
R_HEAD = 128
R_HEADS = 4
R_CHUNK = 128
R_NORM_EPS = 1e-6
ROPE_BASE = 10000.0
C_BLOCK = 64
C_BLOCKS = 8
CONV_W = 4
LRU_C = 8.0
N_EXPERTS = 16
N_GROUPS = 4
EXPERTS_PER_GROUP = 4
D_EXPERT = 512
NORM_EPS = 1e-6

LANES = 128
RET_ROWS = 128
VMEM_LIMIT = 48 * 1024 * 1024


def _cparams(sem):
    return pltpu.CompilerParams(dimension_semantics=sem, vmem_limit_bytes=VMEM_LIMIT)


def _dot(a, b):
    return jnp.dot(a, b, preferred_element_type=F32)


def _bdot(a, b):
    return jnp.dot(a.astype(BF16), b.astype(BF16), preferred_element_type=F32)


def _split3(x):
    hi = x.astype(BF16)
    r1 = x - hi.astype(F32)
    mid = r1.astype(BF16)
    lo = (r1 - mid.astype(F32)).astype(BF16)
    return hi, mid, lo


def _segsum(x, ones_bf16):
    w = ones_bf16.shape[0]
    hi, mid, lo = _split3(x)
    parts = []
    for j in range(x.shape[1] // w):
        c = slice(j * w, (j + 1) * w)
        parts.append((_dot(lo[:, c], ones_bf16) + _dot(mid[:, c], ones_bf16)) + _dot(hi[:, c], ones_bf16))
    return jnp.concatenate(parts, axis=1)


def _ada_kernel(c_ref, w_ref, b_ref, o_ref):
    c = c_ref[...]
    s = c * jax.nn.sigmoid(c)
    o_ref[...] = _bdot(s, w_ref[...]) + b_ref[...]


def _ada(c, ada_w, ada_b):
    rows = c.shape[0]
    tn = 1536
    return pl.pallas_call(
        _ada_kernel,
        out_shape=jax.ShapeDtypeStruct((DEPTH, rows, 6 * D_MODEL), F32),
        grid=(DEPTH, 6 * D_MODEL // tn),
        in_specs=[
            pl.BlockSpec((rows, D_MODEL), lambda l, j: (0, 0)),
            pl.BlockSpec((None, D_MODEL, tn), lambda l, j: (l, 0, j)),
            pl.BlockSpec((None, 1, tn), lambda l, j: (l, 0, j)),
        ],
        out_specs=pl.BlockSpec((None, rows, tn), lambda l, j: (l, 0, j)),
        compiler_params=_cparams(("arbitrary", "arbitrary")),
    )(c, ada_w, ada_b.reshape(DEPTH, 1, 6 * D_MODEL))


class _Group:
    def __init__(self, batch, seq, mod4):
        self.batch = batch
        self.seq = seq
        self.rows = batch * seq
        self.mod4 = mod4
        self.per_token = mod4.shape[0] == 1 and mod4.shape[2] != 1

    def mod_spec(self, k, tm):
        if self.per_token:
            return pl.BlockSpec((None, None, tm, D_MODEL), lambda i, *_: (0, k, i, 0))
        seq = self.seq
        return pl.BlockSpec((None, None, 1, D_MODEL), lambda i, *_: ((i * tm) // seq, k, 0, 0))


def _rms(x, g):
    return x * lax.rsqrt(jnp.mean(x * x, axis=-1, keepdims=True) + NORM_EPS) * g


def _norm_mod_kernel(x_ref, g_ref, sc_ref, sh_ref, o_ref):
    y = _rms(x_ref[...], g_ref[...])
    o_ref[...] = (y * (1.0 + sc_ref[...]) + sh_ref[...]).astype(o_ref.dtype)


def _norm_kernel(x_ref, g_ref, o_ref):
    o_ref[...] = _rms(x_ref[...], g_ref[...]).astype(o_ref.dtype)


def _norm_mod(x, g, grp, k_sc, k_sh, tm):
    return pl.pallas_call(
        _norm_mod_kernel,
        out_shape=jax.ShapeDtypeStruct((grp.rows, D_MODEL), BF16),
        grid=(grp.rows // tm,),
        in_specs=[
            pl.BlockSpec((tm, D_MODEL), lambda i: (i, 0)),
            pl.BlockSpec((1, D_MODEL), lambda i: (0, 0)),
            grp.mod_spec(k_sc, tm),
            grp.mod_spec(k_sh, tm),
        ],
        out_specs=pl.BlockSpec((tm, D_MODEL), lambda i: (i, 0)),
        compiler_params=_cparams(("arbitrary",)),
    )(x, g.reshape(1, D_MODEL), grp.mod4, grp.mod4)


def _final_norm(x, g, tm):
    rows = x.shape[0]
    return pl.pallas_call(
        _norm_kernel,
        out_shape=jax.ShapeDtypeStruct((rows, D_MODEL), F32),
        grid=(rows // tm,),
        in_specs=[
            pl.BlockSpec((tm, D_MODEL), lambda i: (i, 0)),
            pl.BlockSpec((1, D_MODEL), lambda i: (0, 0)),
        ],
        out_specs=pl.BlockSpec((tm, D_MODEL), lambda i: (i, 0)),
        compiler_params=_cparams(("arbitrary",)),
    )(x, g.reshape(1, D_MODEL))


def _mm_kernel(a_ref, w_ref, o_ref):
    o_ref[...] = _dot(a_ref[...], w_ref[...])


def _matmul(a, w, tm, tn):
    rows, k = a.shape
    n = w.shape[1]
    return pl.pallas_call(
        _mm_kernel,
        out_shape=jax.ShapeDtypeStruct((rows, n), F32),
        grid=(n // tn, rows // tm),
        in_specs=[
            pl.BlockSpec((tm, k), lambda j, i: (i, 0)),
            pl.BlockSpec((k, tn), lambda j, i: (0, j)),
        ],
        out_specs=pl.BlockSpec((tm, tn), lambda j, i: (i, j)),
        compiler_params=_cparams(("arbitrary", "arbitrary")),
    )(a, w)


def _rwkv_pre(pa, prev, mu_ref, w0_ref, w2_ref, a0_ref, a2_ref, g2_ref, kk_ref, ka_ref, ones):
    pm = pa + (prev - pa) * mu_ref[...]
    r = pm[:, 0:512]
    k = pm[:, 512:1024]
    v = pm[:, 1024:1536]
    xwa = pm[:, 1536:1664]
    xg = pm[:, 1664:1792]
    w_log = -jax.nn.softplus(-(w0_ref[...] + _bdot(jnp.tanh(xwa), w2_ref[...]))) - 0.5
    logw = -jnp.exp(w_log)
    a = jax.nn.sigmoid(a0_ref[...] + _bdot(xwa, a2_ref[...]))
    g = _bdot(jax.nn.sigmoid(xg), g2_ref[...])
    kk = k * kk_ref[...]
    kk = kk * lax.rsqrt(_segsum(kk * kk, ones) + A_KK_EPS)
    k2 = k * (1.0 + (a - 1.0) * ka_ref[...])
    return r, logw, k2, v, -kk, kk * a, g


def _rwkv_post(o, r, k2, v, g, rk_ref, lng_ref, lnb_ref, ones):
    mean = _segsum(o, ones) * (1.0 / A_HEAD)
    oc = o - mean
    var = _segsum(oc * oc, ones) * (1.0 / A_HEAD)
    o = oc * lax.rsqrt(var + A_NORM_EPS) * lng_ref[...] + lnb_ref[...]
    bonus = _segsum(r * k2 * rk_ref[...], ones) * v
    return (o + bonus) * g


_NN = (((1,), (0,)), ((), ()))
_NT = (((1,), (1,)), ((), ()))
_TN = (((0,), (0,)), ((), ()))
RW_CHUNK = 64
RW_PAIRS = A_HEADS // 2


def _bdg(a, b, dims):
    return lax.dot_general(a.astype(BF16), b.astype(BF16), dims, preferred_element_type=F32)


def _rwkv_chunk_kernel(nck,
                       pa_ref, sh0_ref, g0_ref, mu_ref, w0_ref, w2_ref, a0_ref, a2_ref, g2_ref,
                       kk_ref, ka_ref, rk_ref, lng_ref, lnb_ref, ones_ref, tri_ref,
                       o_ref, gout_ref,
                       carry_ref, prev_ref, g_s, wu_s, m_s, o_s):
    c = pl.program_id(1)
    ck = RW_CHUNK
    ones = ones_ref[...]

    @pl.when(c == 0)
    def _():
        carry_ref[...] = sh0_ref[...]
        g_s[...] = g0_ref[...]

    pa = pa_ref[...]
    rows = pa.shape[0]
    prev_ref[...] = pltpu.roll(pa, 1, axis=0)
    prev_ref[pl.ds(0, 1), :] = carry_ref[...]
    carry_ref[...] = pa[rows - 1:rows, :]
    r, logw, k2, v, an, bn, g = _rwkv_pre(pa, prev_ref[...], mu_ref, w0_ref, w2_ref, a0_ref, a2_ref,
                                          g2_ref, kk_ref, ka_ref, ones)

    tri = tri_ref[...]
    hi, mid, lo = _split3(logw)
    cum =(_dot(tri, lo) + _dot(tri, mid)) + _dot(tri, hi)
    cum_last = jnp.concatenate(
        [jnp.broadcast_to(cum[(i + 1) * ck - 1:(i + 1) * ck, :], (ck, D_BRANCH)) for i in range(nck)], axis=0)
    gam = jnp.exp(cum)
    inv = jnp.exp(-cum)
    to_end = jnp.exp(cum_last - cum)
    a_t = an * jnp.exp(cum - logw)
    b_t = bn * inv
    k_t = k2 * inv
    r_t = r * gam
    b_e = bn * to_end
    k_e = k2 * to_end

    lane = lax.broadcasted_iota(jnp.int32, (ck, LANES), 1)
    rowi = lax.broadcasted_iota(jnp.int32, (ck, LANES), 0)
    m0 = lane < A_HEAD
    coli = lane & (A_HEAD - 1)
    strict = rowi > coli
    incl = rowi >= coli
    eye_p = (rowi == coli).astype(F32)
    r128 = lax.broadcasted_iota(jnp.int32, (LANES, LANES), 0)
    c128 = lax.broadcasted_iota(jnp.int32, (LANES, LANES), 1)
    blockmask = (r128 < A_HEAD) == (c128 < A_HEAD)

    def bd(q):
        q = q.astype(BF16)
        z = jnp.zeros_like(q)
        return jnp.concatenate([jnp.where(m0, q, z), jnp.where(m0, z, q)], axis=0)

    probs = [(i, p) for i in range(nck) for p in range(RW_PAIRS)]
    sl = {(i, p): (slice(i * ck, (i + 1) * ck), slice(p * LANES, (p + 1) * LANES)) for i, p in probs}
    l_pow, l_ak, t_inv = {}, {}, {}
    for q in probs:
        rs, ls = sl[q]
        lhs = jnp.concatenate([a_t[rs, ls], r_t[rs, ls]], axis=0).astype(BF16)
        ab = lax.dot_general(lhs, bd(b_t[rs, ls]), _NT, preferred_element_type=F32)
        ak = lax.dot_general(lhs, bd(k_t[rs, ls]), _NT, preferred_element_type=F32)
        l_pow[q] = jnp.where(strict, ab[:ck], 0.0)
        l_ak[q] = jnp.where(strict, ak[:ck], 0.0)
        m_s[q[0], q[1], :, 0:LANES] = jnp.where(incl, ab[ck:], 0.0)
        m_s[q[0], q[1], :, LANES:2 * LANES] = jnp.where(incl, ak[ck:], 0.0)
        t_inv[q] = eye_p + l_pow[q]
    n = 1
    while 2 * n < ck:
        for q in probs:
            l_pow[q] = _bdg(l_pow[q], bd(l_pow[q]), _NN)
        for q in probs:
            t_inv[q] = t_inv[q] + _bdg(t_inv[q], bd(l_pow[q]), _NN)
        n *= 2
    lak_v = {}
    for q in probs:
        rs, ls = sl[q]
        lak_v[q] = _bdg(l_ak[q], bd(v[rs, ls]), _NN)
    for q in probs:
        rs, ls = sl[q]
        wu_s[q[0], q[1]] = _bdg(t_inv[q], jnp.concatenate([bd(a_t[rs, ls]), bd(lak_v[q])], axis=1), _NN)

    for i in range(nck):
        rs = slice(i * ck, (i + 1) * ck)
        for p in range(RW_PAIRS):
            ls = slice(p * LANES, (p + 1) * LANES)
            g_p = g_s[p]
            g_bf = g_p.astype(BF16)
            v_p = v[rs, ls]
            wu = wu_s[i, p]
            u = lax.dot_general(wu[:, 0:LANES].astype(BF16), g_bf, _NT, preferred_element_type=F32) \
                + wu[:, LANES:2 * LANES]
            o_p = lax.dot_general(r_t[rs, ls].astype(BF16), g_bf, _NT, preferred_element_type=F32) \
                + _bdg(m_s[i, p], jnp.concatenate([bd(u), bd(v_p)], axis=0), _NN)
            upd = _bdg(jnp.concatenate([u, v_p], axis=0),
                       jnp.concatenate([b_e[rs, ls], k_e[rs, ls]], axis=0), _TN)
            g_s[p] = jnp.where(blockmask, g_p * gam[(i + 1) * ck - 1:(i + 1) * ck, ls] + upd, 0.0)
            o_s[rs, ls] = o_p

    o_ref[...] = _rwkv_post(o_s[...], r, k2, v, g, rk_ref, lng_ref, lnb_ref, ones).astype(o_ref.dtype)
    gout_ref[...] = g_s[...]


def _rwkv_chunked(pa, shift0, s0, prm, batch, seq, nck):
    lb = nck * RW_CHUNK
    nstep = seq // lb
    s0p = s0.reshape(batch, RW_PAIRS, 2, A_HEAD, A_HEAD)
    eye2 = jnp.eye(2, dtype=s0.dtype)
    g0 = jnp.einsum("bpjvk,ji->bpjvik", s0p, eye2).reshape(batch, RW_PAIRS, LANES, LANES)
    tri = np.kron(np.eye(nck), np.tril(np.ones((RW_CHUNK, RW_CHUNK)))).astype(np.float32)
    params = [prm[n] for n in ("mu", "w0", "w2", "a0", "a2", "g2", "kk", "ka", "rk", "lng", "lnb", "ones")]
    params.append(jnp.asarray(tri, BF16))

    def full(arr):
        nd = arr.ndim
        return pl.BlockSpec(arr.shape, lambda i, c: (0,) * nd)

    o, g_out = pl.pallas_call(
        functools.partial(_rwkv_chunk_kernel, nck),
        out_shape=(jax.ShapeDtypeStruct((batch * seq, D_BRANCH), BF16),
                   jax.ShapeDtypeStruct((batch, RW_PAIRS, LANES, LANES), F32)),
        grid=(batch, nstep),
        in_specs=[pl.BlockSpec((lb, A_PROJ), lambda i, c: (i * nstep + c, 0)),
                  pl.BlockSpec((None, 1, A_PROJ), lambda i, c: (i, 0, 0)),
                  pl.BlockSpec((None, RW_PAIRS, LANES, LANES), lambda i, c: (i, 0, 0, 0))]
                 + [full(p) for p in params],
        out_specs=(pl.BlockSpec((lb, D_BRANCH), lambda i, c: (i * nstep + c, 0)),
                   pl.BlockSpec((None, RW_PAIRS, LANES, LANES), lambda i, c: (i, 0, 0, 0))),
        scratch_shapes=[pltpu.VMEM((1, A_PROJ), F32),
                        pltpu.VMEM((lb, A_PROJ), F32),
                        pltpu.VMEM((RW_PAIRS, LANES, LANES), F32),
                        pltpu.VMEM((nck, RW_PAIRS, RW_CHUNK, 2 * LANES), F32),
                        pltpu.VMEM((nck, RW_PAIRS, RW_CHUNK, 2 * LANES), F32),
                        pltpu.VMEM((lb, D_BRANCH), F32)],
        compiler_params=_cparams(("arbitrary", "arbitrary")),
    )(pa, shift0.reshape(batch, 1, A_PROJ), g0, *params)
    g5 = g_out.reshape(batch, RW_PAIRS, 2, A_HEAD, 2, A_HEAD)
    s_new = jnp.stack([g5[:, :, 0, :, 0, :], g5[:, :, 1, :, 1, :]], axis=2)
    return o, s_new.reshape(batch, A_HEADS, A_HEAD, A_HEAD)


def _rwkv_step_kernel(nb, lb,
                      pa_ref, sh0_ref, s0_ref, mu_ref, w0_ref, w2_ref, a0_ref, a2_ref, g2_ref,
                      kk_ref, ka_ref, rk_ref, lng_ref, lnb_ref, ones_ref,
                      o_ref, sout_ref):
    ones = ones_ref[...]
    pa = pa_ref[...].reshape(lb * nb, A_PROJ)
    prev = jnp.concatenate([sh0_ref[...], pa[:(lb - 1) * nb, :]], axis=0)
    r, logw, k2, v, an, bn, g = _rwkv_pre(pa, prev, mu_ref, w0_ref, w2_ref, a0_ref, a2_ref,
                                          g2_ref, kk_ref, ka_ref, ones)
    w = jnp.exp(logw)
    srows = nb * A_HEAD
    rowi = lax.broadcasted_iota(jnp.int32, (srows, D_BRANCH), 0)
    lane = lax.broadcasted_iota(jnp.int32, (srows, D_BRANCH), 1)
    eye = (rowi & (A_HEAD - 1)) == (lane & (A_HEAD - 1))

    def per_seq(x, t):
        xt = x[t * nb:(t + 1) * nb, :]
        return jnp.concatenate([jnp.broadcast_to(xt[b:b + 1, :], (A_HEAD, D_BRANCH)) for b in range(nb)],
                               axis=0)

    s = s0_ref[...].reshape(srows, D_BRANCH)
    outs = []
    for t in range(lb):
        sa = _segsum(s * per_seq(an, t), ones)
        vcol = _segsum(jnp.where(eye, per_seq(v, t), 0.0), ones)
        s = s * per_seq(w, t) + sa * per_seq(bn, t) + vcol * per_seq(k2, t)
        out = _segsum(s * per_seq(r, t), ones)
        outs.append(jnp.sum(jnp.where(eye, out, 0.0).reshape(nb, A_HEAD, D_BRANCH), axis=1))
    o = jnp.concatenate(outs, axis=0)
    res = _rwkv_post(o, r, k2, v, g, rk_ref, lng_ref, lnb_ref, ones).astype(o_ref.dtype)
    o_ref[...] = res.reshape(lb, nb, D_BRANCH)
    sout_ref[...] = s.reshape(nb, A_HEAD, D_BRANCH)


def _rwkv_steps(pa, shift0, s0, prm, batch, seq, nb):
    pa_tm = pa.reshape(batch, seq, A_PROJ).transpose(1, 0, 2)
    s0k = s0.transpose(0, 2, 1, 3).reshape(batch, A_HEAD, D_BRANCH)

    def full(arr):
        nd = arr.ndim
        return pl.BlockSpec(arr.shape, lambda i: (0,) * nd)

    params = [prm[n] for n in ("mu", "w0", "w2", "a0", "a2", "g2", "kk", "ka", "rk", "lng", "lnb", "ones")]
    o, s_out = pl.pallas_call(
        functools.partial(_rwkv_step_kernel, nb, seq),
        out_shape=(jax.ShapeDtypeStruct((seq, batch, D_BRANCH), BF16),
                   jax.ShapeDtypeStruct((batch, A_HEAD, D_BRANCH), F32)),
        grid=(batch // nb,),
        in_specs=[pl.BlockSpec((seq, nb, A_PROJ), lambda i: (0, i, 0)),
                  pl.BlockSpec((nb, A_PROJ), lambda i: (i, 0)),
                  pl.BlockSpec((nb, A_HEAD, D_BRANCH), lambda i: (i, 0, 0))]
                 + [full(p) for p in params],
        out_specs=(pl.BlockSpec((seq, nb, D_BRANCH), lambda i: (0, i, 0)),
                   pl.BlockSpec((nb, A_HEAD, D_BRANCH), lambda i: (i, 0, 0))),
        compiler_params=_cparams(("arbitrary",)),
    )(pa_tm, shift0, s0k, *params)
    o = o.transpose(1, 0, 2).reshape(batch * seq, D_BRANCH)
    s_new = s_out.reshape(batch, A_HEAD, A_HEADS, A_HEAD).transpose(0, 2, 1, 3)
    return o, s_new


def _ret_kernel(nb, lb,
                rq_ref, cos_ref, sin_ref, dm_ref, kd_ref, qd_ref, cd_ref, s0_ref,
                o_ref, sout_ref, s_ref):
    c = pl.program_id(1)

    @pl.when(c == 0)
    def _():
        s_ref[...] = s0_ref[...]

    cos = cos_ref[...]
    sin = sin_ref[...]
    scale = R_HEAD ** -0.5
    half = R_HEAD // 2
    row8 = lax.broadcasted_iota(jnp.int32, (8, R_HEAD), 0)
    for h in range(R_HEADS):
        lo, hi = h * R_HEAD, (h + 1) * R_HEAD
        q = rq_ref[:, lo:hi]
        k = rq_ref[:, 512 + lo:512 + hi]
        v = rq_ref[:, 1024 + lo:1024 + hi]
        gate = rq_ref[:, 1536 + lo:1536 + hi]
        qh = q * cos + pltpu.roll(q, half, axis=1) * sin
        kh = (k * cos + pltpu.roll(k, half, axis=1) * sin) * scale
        scores = lax.dot_general(qh.astype(BF16), kh.astype(BF16), (((1,), (1,)), ((), ())),
                                 preferred_element_type=F32) * dm_ref[h]
        o = _bdot(scores, v)
        qd = qh * qd_ref[:, lo:hi]
        ku = kh * kd_ref[:, lo:hi]
        cd = cd_ref[h]
        if nb == 1:
            s = s_ref[0, h]
            o = o + _bdot(qd, s)
            upd = lax.dot_general(ku.astype(BF16), v.astype(BF16), (((0,), (0,)), ((), ())),
                                  preferred_element_type=F32)
            s_ref[0, h] = s * cd + upd
        else:
            per_tile = 8 // lb
            inter = []
            for i in range(RET_ROWS // 8):
                qd_t = qd[i * 8:(i + 1) * 8, :]
                ku_t = ku[i * 8:(i + 1) * 8, :]
                v_t = v[i * 8:(i + 1) * 8, :].astype(BF16)
                acc = None
                for j in range(per_tile):
                    b = i * per_tile + j
                    m = (row8 >= j * lb) & (row8 < (j + 1) * lb)
                    s = s_ref[b, h]
                    part = _bdot(jnp.where(m, qd_t, 0.0), s)
                    acc = part if acc is None else acc + part
                    upd = lax.dot_general(jnp.where(m, ku_t, 0.0).astype(BF16), v_t,
                                          (((0,), (0,)), ((), ())), preferred_element_type=F32)
                    s_ref[b, h] = s * cd + upd
                inter.append(acc)
            o = o + jnp.concatenate(inter, axis=0)
        oc = o - jnp.mean(o, axis=-1, keepdims=True)
        on = oc * lax.rsqrt(jnp.mean(oc * oc, axis=-1, keepdims=True) + R_NORM_EPS)
        o_ref[:, lo:hi] = (gate * jax.nn.sigmoid(gate) * on).astype(o_ref.dtype)
    sout_ref[...] = s_ref[...]


def _retention(rq, s0, tabs, batch, seq):
    lb = min(R_CHUNK, seq)
    nb = RET_ROWS // lb
    nchunk = seq // lb
    cos, sin, dm, kd, qd, cd = tabs
    ntab = cos.shape[0] // RET_ROWS

    def const(arr):
        nd = arr.ndim
        return pl.BlockSpec(arr.shape, lambda i, c: (0,) * nd)

    tab_idx = (lambda i, c: (c, 0)) if ntab > 1 else (lambda i, c: (0, 0))
    return pl.pallas_call(
        functools.partial(_ret_kernel, nb, lb),
        out_shape=(jax.ShapeDtypeStruct((batch * seq, D_BRANCH), BF16),
                   jax.ShapeDtypeStruct((batch, R_HEADS, R_HEAD, R_HEAD), F32)),
        grid=(batch // nb, nchunk),
        in_specs=[pl.BlockSpec((RET_ROWS, 4 * D_BRANCH), lambda i, c: (i * nchunk + c, 0)),
                  pl.BlockSpec((RET_ROWS, R_HEAD), tab_idx),
                  pl.BlockSpec((RET_ROWS, R_HEAD), tab_idx),
                  const(dm), const(kd), const(qd), const(cd),
                  pl.BlockSpec((nb, R_HEADS, R_HEAD, R_HEAD), lambda i, c: (i, 0, 0, 0))],
        out_specs=(pl.BlockSpec((RET_ROWS, D_BRANCH), lambda i, c: (i * nchunk + c, 0)),
                   pl.BlockSpec((nb, R_HEADS, R_HEAD, R_HEAD), lambda i, c: (i, 0, 0, 0))),
        scratch_shapes=[pltpu.VMEM((nb, R_HEADS, R_HEAD, R_HEAD), F32)],
        compiler_params=_cparams(("arbitrary", "arbitrary")),
    )(rq, cos, sin, dm, kd, qd, cd, s0)


def _ret_tables(seq, pos0):
    lb = min(R_CHUNK, seq)
    nb = RET_ROWS // lb
    half = R_HEAD // 2
    pos = pos0 + jnp.arange(seq, dtype=jnp.int32)
    inv = ROPE_BASE ** (-jnp.arange(half, dtype=F32) / half)
    ang = pos.astype(F32)[:, None] * inv[None, :]
    cos, sin = jnp.cos(ang), jnp.sin(ang)
    cos2 = jnp.concatenate([cos, cos], axis=-1)
    sin2 = jnp.concatenate([-sin, sin], axis=-1)
    if nb > 1:
        cos2 = jnp.tile(cos2, (nb, 1))
        sin2 = jnp.tile(sin2, (nb, 1))
    log_g = jnp.log1p(-jnp.exp2(-5.0 - jnp.arange(R_HEADS, dtype=F32)))
    idx = jnp.arange(lb, dtype=F32)
    diff = idx[:, None] - idx[None, :]
    dmask = jnp.where(diff >= 0, jnp.exp(log_g[:, None, None] * jnp.maximum(diff, 0.0)), 0.0)
    if nb > 1:
        dmask = jnp.einsum("ab,hij->haibj", jnp.eye(nb, dtype=F32), dmask).reshape(
            R_HEADS, RET_ROWS, RET_ROWS)
    k_decay = jnp.exp(log_g[None, :] * (lb - 1.0 - idx)[:, None])
    q_decay = jnp.exp(log_g[None, :] * (idx + 1.0)[:, None])
    kd = jnp.tile(jnp.repeat(k_decay, R_HEAD, axis=1), (nb, 1))
    qd = jnp.tile(jnp.repeat(q_decay, R_HEAD, axis=1), (nb, 1))
    cd = jnp.broadcast_to(jnp.exp(log_g * lb)[:, None, None], (R_HEADS, 1, R_HEAD))
    return cos2, sin2, dmask, kd, qd, cd


def _lru_kernel(nb, lb, three_d,
                cx_ref, conv0_ref, h0_ref, cw_ref, cb_ref, wri_ref, br_ref, bi_ref, lam_ref,
                o_ref, hout_ref,
                cc_ref, hc_ref, x1_ref, x2_ref, x3_ref, a_ref, b_ref):
    c = pl.program_id(1)
    rows = nb * lb

    @pl.when(c == 0)
    def _():
        cc_ref[...] = conv0_ref[...]
        hc_ref[...] = h0_ref[...]

    cx = cx_ref[...]
    if three_d:
        cx = cx.reshape(rows, 2 * D_BRANCH)
    xb = cx[:, :D_BRANCH]
    gb = cx[:, D_BRANCH:]
    x1_ref[...] = pltpu.roll(xb, 1, axis=0)
    x2_ref[...] = pltpu.roll(xb, 2, axis=0)
    x3_ref[...] = pltpu.roll(xb, 3, axis=0)
    for b in range(nb):
        r0 = b * lb
        c0 = cc_ref[b, 0:1, :]
        c1 = cc_ref[b, 1:2, :]
        c2 = cc_ref[b, 2:3, :]
        x1_ref[pl.ds(r0, 1), :] = c2
        x2_ref[pl.ds(r0, 1), :] = c1
        x2_ref[pl.ds(r0 + 1, 1), :] = c2
        x3_ref[pl.ds(r0, 1), :] = c0
        x3_ref[pl.ds(r0 + 1, 1), :] = c1
        x3_ref[pl.ds(r0 + 2, 1), :] = c2
        cc_ref[b] = xb[r0 + lb - 3:r0 + lb, :]
    cw = cw_ref[...]
    xc = cb_ref[...] + (((x3_ref[...] * cw[0:1, :] + x2_ref[...] * cw[1:2, :]) + x1_ref[...] * cw[2:3, :])
                        + xb * cw[3:4, :])
    ri = _bdot(xc, wri_ref[...])
    r = jax.nn.sigmoid(ri[:, :D_BRANCH] + br_ref[...])
    i = jax.nn.sigmoid(ri[:, D_BRANCH:] + bi_ref[...])
    log_a = LRU_C * r * jax.nn.log_sigmoid(lam_ref[...])
    a = jnp.exp(log_a)
    bb = jnp.sqrt(-jnp.tanh(log_a) * (a * a + 1.0)) * (i * xc)
    a_ref[...] = a
    b_ref[...] = bb
    for b in range(nb):
        r0 = b * lb
        b_ref[pl.ds(r0, 1), :] = bb[r0:r0 + 1, :] + a[r0:r0 + 1, :] * hc_ref[b]
    a = a_ref[...]
    bb = b_ref[...]
    t_idx = lax.broadcasted_iota(jnp.int32, (rows, D_BRANCH), 0) % lb
    s = 1
    while s < lb:
        keep = t_idx >= s
        a_sh = jnp.where(keep, pltpu.roll(a, s, axis=0), 1.0)
        b_sh = jnp.where(keep, pltpu.roll(bb, s, axis=0), 0.0)
        bb = a * b_sh + bb
        a = a * a_sh
        s *= 2
    h = bb
    for b in range(nb):
        r0 = b * lb
        hc_ref[b] = h[r0 + lb - 1:r0 + lb, :]
    res = (h * jax.nn.gelu(gb)).astype(o_ref.dtype)
    if three_d:
        res = res.reshape(nb, lb, D_BRANCH)
    o_ref[...] = res
    hout_ref[...] = hc_ref[...]


def _rglru(cx, conv0, h0, prm, batch, seq, nb, lb):
    nchunk = seq // lb
    rows = nb * lb
    three_d = nb > 1 and nchunk > 1
    if three_d:
        cx_in = cx.reshape(batch, seq, 2 * D_BRANCH)
        cx_spec = pl.BlockSpec((nb, lb, 2 * D_BRANCH), lambda i, c: (i, c, 0))
        o_shape = jax.ShapeDtypeStruct((batch, seq, D_BRANCH), BF16)
        o_spec = pl.BlockSpec((nb, lb, D_BRANCH), lambda i, c: (i, c, 0))
    else:
        cx_in = cx
        cx_spec = pl.BlockSpec((rows, 2 * D_BRANCH), lambda i, c: (i * nchunk + c, 0))
        o_shape = jax.ShapeDtypeStruct((batch * seq, D_BRANCH), BF16)
        o_spec = pl.BlockSpec((rows, D_BRANCH), lambda i, c: (i * nchunk + c, 0))

    def full(arr):
        nd = arr.ndim
        return pl.BlockSpec(arr.shape, lambda i, c: (0,) * nd)

    params = [prm[n] for n in ("cw", "cb", "wri", "br", "bi", "lam")]
    o, h_out = pl.pallas_call(
        functools.partial(_lru_kernel, nb, lb, three_d),
        out_shape=(o_shape, jax.ShapeDtypeStruct((batch, 1, D_BRANCH), F32)),
        grid=(batch // nb, nchunk),
        in_specs=[cx_spec,
                  pl.BlockSpec((nb, CONV_W - 1, D_BRANCH), lambda i, c: (i, 0, 0)),
                  pl.BlockSpec((nb, 1, D_BRANCH), lambda i, c: (i, 0, 0))]
                 + [full(p) for p in params],
        out_specs=(o_spec, pl.BlockSpec((nb, 1, D_BRANCH), lambda i, c: (i, 0, 0))),
        scratch_shapes=[pltpu.VMEM((nb, CONV_W - 1, D_BRANCH), F32),
                        pltpu.VMEM((nb, 1, D_BRANCH), F32)]
                       + [pltpu.VMEM((rows, D_BRANCH), F32) for _ in range(5)],
        compiler_params=_cparams(("arbitrary", "arbitrary")),
    )(cx_in, conv0, h0.reshape(batch, 1, D_BRANCH), *params)
    return o.reshape(batch * seq, D_BRANCH), h_out.reshape(batch, D_BRANCH)


def _merge_kernel(h_ref, oa_ref, ob_ref, oc_ref, x_ref, gt_ref, wgm_ref, wb_ref, wout_ref, o_ref):
    gm = _dot(h_ref[...], wgm_ref[...])
    merged = None
    for n, br_ref in enumerate((oa_ref, ob_ref, oc_ref)):
        br = _dot(br_ref[...], wb_ref[n])
        term = jax.nn.sigmoid(gm[:, n * D_MODEL:(n + 1) * D_MODEL]) * br
        merged = term if merged is None else merged + term
    y = _bdot(merged, wout_ref[...])
    o_ref[...] = x_ref[...] + gt_ref[...] * y


def _merge_out(h, oa, ob, oc, x, grp, wgm, wb, wout, tm):
    row = lambda i: (i, 0)
    return pl.pallas_call(
        _merge_kernel,
        out_shape=jax.ShapeDtypeStruct((grp.rows, D_MODEL), F32),
        grid=(grp.rows // tm,),
        in_specs=[pl.BlockSpec((tm, D_MODEL), row),
                  pl.BlockSpec((tm, D_BRANCH), row),
                  pl.BlockSpec((tm, D_BRANCH), row),
                  pl.BlockSpec((tm, D_BRANCH), row),
                  pl.BlockSpec((tm, D_MODEL), row),
                  grp.mod_spec(2, tm),
                  pl.BlockSpec(wgm.shape, lambda i: (0, 0)),
                  pl.BlockSpec(wb.shape, lambda i: (0, 0, 0)),
                  pl.BlockSpec(wout.shape, lambda i: (0, 0))],
        out_specs=pl.BlockSpec((tm, D_MODEL), row),
        compiler_params=_cparams(("arbitrary",)),
    )(h, oa, ob, oc, x, grp.mod4, wgm, wb, wout)


def _route(probs, sel):
    def beats(vj, vi, j, i):
        return (vj > vi) | ((vj == vi) & (j < i)) if j < i else (vj > vi)

    grp_score = []
    for g in range(N_GROUPS):
        vals = sel[g * 4:(g + 1) * 4]
        best = None
        for i in range(4):
            for j in range(i + 1, 4):
                sm = vals[i] + vals[j]
                best = sm if best is None else jnp.maximum(best, sm)
        grp_score.append(best)
    top = grp_score[0]
    gidx = jnp.zeros_like(top, dtype=jnp.int32)
    for g in range(1, N_GROUPS):
        better = grp_score[g] > top
        top = jnp.where(better, grp_score[g], top)
        gidx = jnp.where(better, g, gidx)
    chosen = []
    for e in range(N_EXPERTS):
        g, i = divmod(e, 4)
        vals = sel[g * 4:(g + 1) * 4]
        n_beat = None
        for j in range(4):
            if j == i:
                continue
            bj = beats(vals[j], vals[i], j, i).astype(jnp.int32)
            n_beat = bj if n_beat is None else n_beat + bj
        chosen.append((gidx == g) & (n_beat < 2))
    psum = None
    for e in range(N_EXPERTS):
        pe = jnp.where(chosen[e], probs[e], 0.0)
        psum = pe if psum is None else psum + pe
    return [jnp.where(chosen[e], probs[e] / psum, 0.0) for e in range(N_EXPERTS)]


def _moe_kernel(x_ref, g_ref, sc_ref, sh_ref, gt_ref, wr_ref, rb_ref, wg_ref, wu_ref, wd_ref,
                o_ref, h_s, gate_s):
    e = pl.program_id(1)
    tm = x_ref.shape[0]
    lane = lax.broadcasted_iota(jnp.int32, (tm, LANES), 1)

    @pl.when(e == 0)
    def _():
        y = _rms(x_ref[...], g_ref[...])
        h = (y * (1.0 + sc_ref[...]) + sh_ref[...]).astype(BF16)
        h_s[...] = h
        logits = _dot(h, wr_ref[...])
        logits = jnp.where(lane < N_EXPERTS, logits, -jnp.inf)
        mx = jnp.max(logits, axis=-1, keepdims=True)
        ex = jnp.exp(logits - mx)
        probs = ex / jnp.sum(ex, axis=-1, keepdims=True)
        selv = probs + rb_ref[...]
        pcols = [probs[:, i:i + 1] for i in range(N_EXPERTS)]
        scols = [selv[:, i:i + 1] for i in range(N_EXPERTS)]
        gcols = _route(pcols, scols)
        gates = jnp.zeros((tm, LANES), F32)
        for i in range(N_EXPERTS):
            gates = jnp.where(lane == i, gcols[i], gates)
        gate_s[...] = gates
        o_ref[...] = jnp.zeros_like(o_ref)

    h = h_s[...]
    gcol = jnp.sum(jnp.where(lane == e, gate_s[...], 0.0), axis=-1, keepdims=True)
    hg = _dot(h, wg_ref[...])
    hu = _dot(h, wu_ref[...])
    act = hg * jax.nn.sigmoid(hg) * hu * gcol
    o_ref[...] += _bdot(act, wd_ref[...])

    @pl.when(e == N_EXPERTS - 1)
    def _():
        o_ref[...] = x_ref[...] + gt_ref[...] * o_ref[...]


def _moe(x, g, grp, wr, rb, wg, wu, wd, tm):
    row = lambda i, e: (i, 0)
    return pl.pallas_call(
        _moe_kernel,
        out_shape=jax.ShapeDtypeStruct((grp.rows, D_MODEL), F32),
        grid=(grp.rows // tm, N_EXPERTS),
        in_specs=[pl.BlockSpec((tm, D_MODEL), row),
                  pl.BlockSpec((1, D_MODEL), lambda i, e: (0, 0)),
                  grp.mod_spec(4, tm),
                  grp.mod_spec(3, tm),
                  grp.mod_spec(5, tm),
                  pl.BlockSpec((D_MODEL, LANES), lambda i, e: (0, 0)),
                  pl.BlockSpec((1, LANES), lambda i, e: (0, 0)),
                  pl.BlockSpec((None, D_MODEL, D_EXPERT), lambda i, e: (e, 0, 0)),
                  pl.BlockSpec((None, D_MODEL, D_EXPERT), lambda i, e: (e, 0, 0)),
                  pl.BlockSpec((None, D_EXPERT, D_MODEL), lambda i, e: (e, 0, 0))],
        out_specs=pl.BlockSpec((tm, D_MODEL), row),
        scratch_shapes=[pltpu.VMEM((tm, D_MODEL), BF16),
                        pltpu.VMEM((tm, LANES), F32)],
        compiler_params=_cparams(("arbitrary", "arbitrary")),
    )(x, g.reshape(1, D_MODEL), grp.mod4, grp.mod4, grp.mod4, wr, rb, wg, wu, wd)


def _block_diag(w):
    eye = jnp.eye(C_BLOCKS, dtype=w.dtype)
    return jnp.einsum("hg,hij->higj", eye, w).reshape(D_BRANCH, D_BRANCH)


def _layer_params(l, p):
    row = lambda a: a[l].reshape(1, -1)
    w_in = p["w_in"][l]
    ones = jnp.asarray(np.kron(np.eye(4), np.ones((A_HEAD, A_HEAD))), BF16)
    pad_lo = lambda w: jnp.pad(w, ((0, 64), (0, 0)))
    pad_hi = lambda w: jnp.pad(w, ((64, 0), (0, 0)))
    rw = dict(mu=row(p["a_mu"]), w0=row(p["a_w0"]), w2=pad_lo(p["a_w2"][l]).astype(BF16),
              a0=row(p["a_a0"]), a2=pad_hi(p["a_a2"][l]).astype(BF16), g2=p["a_g2"][l].astype(BF16),
              kk=row(p["a_kk"]), ka=row(p["a_ka"]), rk=row(p["a_rk"]), lng=row(p["a_ln_g"]),
              lnb=row(p["a_ln_b"]), ones=ones)
    lru = dict(cw=p["c_conv_w"][l], cb=row(p["c_conv_b"]),
               wri=jnp.concatenate([_block_diag(p["c_wr"][l]), _block_diag(p["c_wi"][l])], axis=1).astype(BF16),
               br=row(p["c_br"]), bi=row(p["c_bi"]), lam=row(p["c_lam"]))
    return dict(
        w_pa=w_in[:, 0:1792].astype(BF16),
        w_rq=w_in[:, 1792:3840].astype(BF16),
        w_cx=w_in[:, 3840:4864].astype(BF16),
        w_gm=w_in[:, 4864:7936].astype(BF16),
        rw=rw, lru=lru,
        w_branch=p["w_branch"][l].astype(BF16),
        w_out=p["w_out"][l].astype(BF16),
        wg=p["moe_wg"][l].astype(BF16), wu=p["moe_wu"][l].astype(BF16), wd=p["moe_wd"][l].astype(BF16),
        norm_mix=p["norm_mix"][l], norm_ffn=p["norm_ffn"][l])


def kernel(x_prompt, x_sample, c_prompt, c_sample, state_rwkv_shift, state_rwkv_wkv, state_ret, state_lru_h, state_lru_conv, norm_mix, norm_ffn, norm_final, ada_w, ada_b, w_in, a_mu, a_w0, a_w2, a_a0, a_a2, a_g2, a_kk, a_ka, a_rk, a_ln_g, a_ln_b, c_conv_w, c_conv_b, c_wr, c_br, c_wi, c_bi, c_lam, w_branch, w_out, w_router, router_bias, moe_wg, moe_wu, moe_wd):
    p = dict(norm_mix=norm_mix, norm_ffn=norm_ffn, w_in=w_in, a_mu=a_mu, a_w0=a_w0, a_w2=a_w2,
             a_a0=a_a0, a_a2=a_a2, a_g2=a_g2, a_kk=a_kk, a_ka=a_ka, a_rk=a_rk, a_ln_g=a_ln_g,
             a_ln_b=a_ln_b, c_conv_w=c_conv_w, c_conv_b=c_conv_b, c_wr=c_wr, c_br=c_br, c_wi=c_wi,
             c_bi=c_bi, c_lam=c_lam, w_branch=w_branch, w_out=w_out, moe_wg=moe_wg, moe_wu=moe_wu,
             moe_wd=moe_wd)
    bp, lp_, _ = x_prompt.shape
    bs, ls, _ = x_sample.shape
    layers = [_layer_params(l, p) for l in range(DEPTH)]
    router = (jnp.pad(w_router, ((0, 0), (0, LANES - N_EXPERTS))).astype(BF16),
              jnp.pad(router_bias, (0, LANES - N_EXPERTS)).reshape(1, LANES))

    n_c = bp + bs
    pad_c = (-n_c) % 16
    c_all = jnp.pad(jnp.concatenate([c_prompt, c_sample], axis=0), ((0, pad_c), (0, 0)))
    mods = _ada(c_all, ada_w, ada_b)

    def run(x, batch, seq, mod, states, pos0, cfg):
        xs = x.reshape(batch * seq, D_MODEL)
        grps = []
        for l in range(DEPTH):
            if cfg["per_token"]:
                m = jnp.repeat(mod[l], seq, axis=0).reshape(batch * seq, 6, D_MODEL)
                m4 = m.transpose(1, 0, 2)[None]
            else:
                m4 = mod[l].reshape(batch, 6, 1, D_MODEL)
            grps.append(_Group(batch, seq, m4))
        return _trunk_layers(xs, grps, states, layers, router, norm_final, pos0, cfg)

    zeros = lambda s: jnp.zeros((DEPTH, bp) + s.shape[2:], x_prompt.dtype)
    st_prompt = (zeros(state_rwkv_shift), zeros(state_rwkv_wkv), zeros(state_ret),
                 zeros(state_lru_h), zeros(state_lru_conv))
    st_sample = (state_rwkv_shift, state_rwkv_wkv, state_ret, state_lru_h, state_lru_conv)
    cfg_p = dict(per_token=False, tm=min(512, lp_), tm_mg=min(256, lp_), tm_mm=min(1024, lp_),
                 tm_moe=min(1024, lp_), rwkv_nb=8, lru_nb=1, lru_lb=min(256, lp_))
    cfg_s = dict(per_token=True, tm=min(512, bs * ls), tm_mg=min(256, bs * ls), tm_mm=min(512, bs * ls),
                 tm_moe=min(512, bs * ls), rwkv_nb=16, lru_nb=16, lru_lb=ls)
    y_p, new_p = run(x_prompt, bp, lp_, mods[:, :bp], st_prompt, 0, cfg_p)
    y_s, new_s = run(x_sample, bs, ls, mods[:, bp:bp + bs], st_sample, PAST_LEN, cfg_s)
    return (y_p, y_s) + new_p + new_s


def _trunk_layers(x, grps, states, layers, router, norm_final, pos0, cfg):
    batch, seq = grps[0].batch, grps[0].seq
    tabs = _ret_tables(seq, pos0)
    wr, rb = router
    outs = [[] for _ in range(5)]
    for l, lp in enumerate(layers):
        grp = grps[l]
        h = _norm_mod(x, lp["norm_mix"], grp, 1, 0, cfg["tm"])
        pa = _matmul(h, lp["w_pa"], cfg["tm_mm"], 896)
        rq = _matmul(h, lp["w_rq"], cfg["tm_mm"], 1024)
        cx = _matmul(h, lp["w_cx"], cfg["tm_mm"], 1024)
        if seq % RW_CHUNK == 0:
            o_a, wkv = _rwkv_chunked(pa, states[0][l], states[1][l], lp["rw"], batch, seq,
                                     min(4, seq // RW_CHUNK))
        else:
            o_a, wkv = _rwkv_steps(pa, states[0][l], states[1][l], lp["rw"], batch, seq, cfg["rwkv_nb"])
        o_b, ret = _retention(rq, states[2][l], tabs, batch, seq)
        o_c, lru_h = _rglru(cx, states[4][l], states[3][l], lp["lru"], batch, seq,
                            cfg["lru_nb"], cfg["lru_lb"])
        x = _merge_out(h, o_a, o_b, o_c, x, grp, lp["w_gm"], lp["w_branch"], lp["w_out"], cfg["tm_mg"])
        x = _moe(x, lp["norm_ffn"], grp, wr, rb, lp["wg"], lp["wu"], lp["wd"], cfg["tm_moe"])
        outs[0].append(pa.reshape(batch, seq, A_PROJ)[:, -1])
        outs[1].append(wkv)
        outs[2].append(ret)
        outs[3].append(lru_h)
        outs[4].append(cx.reshape(batch, seq, 2 * D_BRANCH)[:, seq - (CONV_W - 1):, :D_BRANCH])
    y = _final_norm(x, norm_final, cfg["tm"])
    return y.reshape(batch, seq, D_MODEL), tuple(jnp.stack(o) for o in outs)
```

```python
import functools
import math

import numpy as np
import jax
import jax.numpy as jnp
from jax import lax
from jax.experimental import pallas as pl
from jax.experimental.pallas import tpu as pltpu

F32 = jnp.float32
BF16 = jnp.bfloat16

D_MODEL = 1024
DEPTH = 2
PAST_LEN = 16384
D_BRANCH = 512
A_HEAD = 64
A_HEADS = 8
A_PROJ = 1792
A_NORM_EPS = 64e-5
A_KK_EPS = 1e-12
R_HEAD = 128
R_HEADS = 4
R_CHUNK = 128
R_NORM_EPS = 1e-6
ROPE_BASE = 10000.0
C_BLOCK = 64
C_BLOCKS = 8
CONV_W = 4
LRU_C = 8.0
N_EXPERTS = 16
N_GROUPS = 4
EXPERTS_PER_GROUP = 4
TOP_K = 2
D_EXPERT = 512
NORM_EPS = 1e-6

LANES = 128
RET_ROWS = 128
VMEM_LIMIT = 48 * 1024 * 1024
MOE_VMEM_LIMIT = 56 * 1024 * 1024


def _cparams(sem, vmem=VMEM_LIMIT):
    return pltpu.CompilerParams(dimension_semantics=sem, vmem_limit_bytes=vmem)


def _dot(a, b):
    return jnp.dot(a, b, preferred_element_type=F32)


def _bdot(a, b):
    return jnp.dot(a.astype(BF16), b.astype(BF16), preferred_element_type=F32)


def _split3(x):
    hi = x.astype(BF16)
    r1 = x - hi.astype(F32)
    mid = r1.astype(BF16)
    lo = (r1 - mid.astype(F32)).astype(BF16)
    return hi, mid, lo


def _segsum(x, ones_bf16):
    w = ones_bf16.shape[0]
    hi, mid, lo = _split3(x)
    parts = []
    for j in range(x.shape[1] // w):
        c = slice(j * w, (j + 1) * w)
        parts.append((_dot(lo[:, c], ones_bf16) + _dot(mid[:, c], ones_bf16)) + _dot(hi[:, c], ones_bf16))
    return jnp.concatenate(parts, axis=1)


def _ada_kernel(c_ref, w_ref, b_ref, o_ref):
    c = c_ref[...]
    s = c * jax.nn.sigmoid(c)
    o_ref[...] = _bdot(s, w_ref[...]) + b_ref[...]


def _ada(c, ada_w, ada_b):
    rows = c.shape[0]
    tn = 1536
    return pl.pallas_call(
        _ada_kernel,
        out_shape=jax.ShapeDtypeStruct((DEPTH, rows, 6 * D_MODEL), F32),
        grid=(DEPTH, 6 * D_MODEL // tn),
        in_specs=[
            pl.BlockSpec((rows, D_MODEL), lambda l, j: (0, 0)),
            pl.BlockSpec((None, D_MODEL, tn), lambda l, j: (l, 0, j)),
            pl.BlockSpec((None, 1, tn), lambda l, j: (l, 0, j)),
        ],
        out_specs=pl.BlockSpec((None, rows, tn), lambda l, j: (l, 0, j)),
        compiler_params=_cparams(("arbitrary", "arbitrary")),
    )(c, ada_w, ada_b.reshape(DEPTH, 1, 6 * D_MODEL))


class _Group:
    def __init__(self, batch, seq, mod4):
        self.batch = batch
        self.seq = seq
        self.rows = batch * seq
        self.mod4 = mod4
        self.per_token = mod4.shape[0] == 1 and mod4.shape[2] != 1

    def mod_spec(self, k, tm):
        if self.per_token:
            return pl.BlockSpec((None, None, tm, D_MODEL), lambda i, *_: (0, k, i, 0))
        seq = self.seq
        return pl.BlockSpec((None, None, 1, D_MODEL), lambda i, *_: ((i * tm) // seq, k, 0, 0))


def _rms(x, g):
    return x * lax.rsqrt(jnp.mean(x * x, axis=-1, keepdims=True) + NORM_EPS) * g


def _norm_mod_kernel(x_ref, g_ref, sc_ref, sh_ref, o_ref):
    y = _rms(x_ref[...], g_ref[...])
    o_ref[...] = (y * (1.0 + sc_ref[...]) + sh_ref[...]).astype(o_ref.dtype)


def _norm_kernel(x_ref, g_ref, o_ref):
    o_ref[...] = _rms(x_ref[...], g_ref[...]).astype(o_ref.dtype)


def _norm_mod(x, g, grp, k_sc, k_sh, tm):
    return pl.pallas_call(
        _norm_mod_kernel,
        out_shape=jax.ShapeDtypeStruct((grp.rows, D_MODEL), BF16),
        grid=(grp.rows // tm,),
        in_specs=[
            pl.BlockSpec((tm, D_MODEL), lambda i: (i, 0)),
            pl.BlockSpec((1, D_MODEL), lambda i: (0, 0)),
            grp.mod_spec(k_sc, tm),
            grp.mod_spec(k_sh, tm),
        ],
        out_specs=pl.BlockSpec((tm, D_MODEL), lambda i: (i, 0)),
        compiler_params=_cparams(("arbitrary",)),
    )(x, g.reshape(1, D_MODEL), grp.mod4, grp.mod4)


def _final_norm(x, g, tm):
    rows = x.shape[0]
    return pl.pallas_call(
        _norm_kernel,
        out_shape=jax.ShapeDtypeStruct((rows, D_MODEL), F32),
        grid=(rows // tm,),
        in_specs=[
            pl.BlockSpec((tm, D_MODEL), lambda i: (i, 0)),
            pl.BlockSpec((1, D_MODEL), lambda i: (0, 0)),
        ],
        out_specs=pl.BlockSpec((tm, D_MODEL), lambda i: (i, 0)),
        compiler_params=_cparams(("arbitrary",)),
    )(x, g.reshape(1, D_MODEL))


def _mm_kernel(a_ref, w_ref, o_ref):
    o_ref[...] = _dot(a_ref[...], w_ref[...])


def _matmul(a, w3, l, col0, n, tm, tn):
    rows, k = a.shape
    c0 = col0 // tn
    return pl.pallas_call(
        _mm_kernel,
        out_shape=jax.ShapeDtypeStruct((rows, n), F32),
        grid=(n // tn, rows // tm),
        in_specs=[
            pl.BlockSpec((tm, k), lambda j, i: (i, 0)),
            pl.BlockSpec((None, k, tn), lambda j, i: (l, 0, c0 + j)),
        ],
        out_specs=pl.BlockSpec((tm, tn), lambda j, i: (i, j)),
        compiler_params=_cparams(("arbitrary", "arbitrary")),
    )(a, w3)


W_RQ, W_CX, W_GM, W_PA, W_PA_PAD = 0, 2048, 3072, 6144, 2048


def _arrange_w_in(w_in):
    pad = jnp.zeros(w_in.shape[:2] + (W_PA_PAD - A_PROJ,), w_in.dtype)
    return jnp.concatenate([w_in[..., 1792:3840], w_in[..., 3840:4864], w_in[..., 4864:7936],
                            w_in[..., 0:1792], pad], axis=-1).astype(BF16)


def _rwkv_pre(pa, prev, mu_ref, w0_ref, w2_ref, a0_ref, a2_ref, g2_ref, kk_ref, ka_ref, ones):
    pm = pa + (prev - pa) * mu_ref[...]
    r = pm[:, 0:512]
    k = pm[:, 512:1024]
    v = pm[:, 1024:1536]
    xwa = pm[:, 1536:1664]
    xg = pm[:, 1664:1792]
    w_log = -jax.nn.softplus(-(w0_ref[...] + _bdot(jnp.tanh(xwa), w2_ref[...]))) - 0.5
    logw = -jnp.exp(w_log)
    a = jax.nn.sigmoid(a0_ref[...] + _bdot(xwa, a2_ref[...]))
    g = _bdot(jax.nn.sigmoid(xg), g2_ref[...])
    kk = k * kk_ref[...]
    kk = kk * lax.rsqrt(_segsum(kk * kk, ones) + A_KK_EPS)
    k2 = k * (1.0 + (a - 1.0) * ka_ref[...])
    return r, logw, k2, v, -kk, kk * a, g


def _rwkv_post(o, r, k2, v, g, rk_ref, lng_ref, lnb_ref, ones):
    mean = _segsum(o, ones) * (1.0 / A_HEAD)
    oc = o - mean
    var = _segsum(oc * oc, ones) * (1.0 / A_HEAD)
    o = oc * lax.rsqrt(var + A_NORM_EPS) * lng_ref[...] + lnb_ref[...]
    bonus = _segsum(r * k2 * rk_ref[...], ones) * v
    return (o + bonus) * g


_NN = (((1,), (0,)), ((), ()))
_NT = (((1,), (1,)), ((), ()))
_TN = (((0,), (0,)), ((), ()))
RW_CHUNK = 64
RW_PAIRS = A_HEADS // 2


def _bdg(a, b, dims):
    return lax.dot_general(a.astype(BF16), b.astype(BF16), dims, preferred_element_type=F32)


def _rwkv_chunk_kernel(nck,
                       pa_ref, sh0_ref, g0_ref, mu_ref, w0_ref, w2_ref, a0_ref, a2_ref, g2_ref,
                       kk_ref, ka_ref, rk_ref, lng_ref, lnb_ref, ones_ref, tri_ref,
                       o_ref, gout_ref,
                       carry_ref, prev_ref, g_s, wu_s, m_s, zy_s, o_s):
    c = pl.program_id(1)
    ck = RW_CHUNK
    ones = ones_ref[...]

    @pl.when(c == 0)
    def _():
        carry_ref[...] = sh0_ref[...]
        g_s[...] = g0_ref[...]

    pa = pa_ref[...]
    rows = pa.shape[0]
    prev_ref[...] = pltpu.roll(pa, 1, axis=0)
    prev_ref[pl.ds(0, 1), :] = carry_ref[...]
    carry_ref[...] = pa[rows - 1:rows, :]
    r, logw, k2, v, an, bn, g = _rwkv_pre(pa, prev_ref[...], mu_ref, w0_ref, w2_ref, a0_ref, a2_ref,
                                          g2_ref, kk_ref, ka_ref, ones)

    tri = tri_ref[...]
    hi, mid, lo = _split3(logw)
    cum =(_dot(tri, lo) + _dot(tri, mid)) + _dot(tri, hi)
    cum_last = jnp.concatenate(
        [jnp.broadcast_to(cum[(i + 1) * ck - 1:(i + 1) * ck, :], (ck, D_BRANCH)) for i in range(nck)], axis=0)
    gam = jnp.exp(cum)
    inv = jnp.exp(-cum)
    to_end = jnp.exp(cum_last - cum)
    a_t = an * jnp.exp(cum - logw)
    b_t = bn * inv
    k_t = k2 * inv
    r_t = r * gam
    b_e = bn * to_end
    k_e = k2 * to_end

    lane = lax.broadcasted_iota(jnp.int32, (ck, LANES), 1)
    rowi = lax.broadcasted_iota(jnp.int32, (ck, LANES), 0)
    m0 = lane < A_HEAD
    coli = lane & (A_HEAD - 1)
    strict = rowi > coli
    incl = rowi >= coli
    eye_p = (rowi == coli).astype(F32)
    r128 = lax.broadcasted_iota(jnp.int32, (LANES, LANES), 0)
    c128 = lax.broadcasted_iota(jnp.int32, (LANES, LANES), 1)
    blockmask = (r128 < A_HEAD) == (c128 < A_HEAD)
    eye128 = r128 == c128
    bk_t = jnp.concatenate([b_e, k_e], axis=1).T

    def bd(q):
        q = q.astype(BF16)
        z = jnp.zeros_like(q)
        return jnp.concatenate([jnp.where(m0, q, z), jnp.where(m0, z, q)], axis=0)

    probs = [(i, p) for i in range(nck) for p in range(RW_PAIRS)]
    sl = {(i, p): (slice(i * ck, (i + 1) * ck), slice(p * LANES, (p + 1) * LANES)) for i, p in probs}
    l_pow, l_ak, t_inv = {}, {}, {}
    for q in probs:
        rs, ls = sl[q]
        lhs = jnp.concatenate([a_t[rs, ls], r_t[rs, ls]], axis=0).astype(BF16)
        ab = lax.dot_general(lhs, bd(b_t[rs, ls]), _NT, preferred_element_type=F32)
        ak = lax.dot_general(lhs, bd(k_t[rs, ls]), _NT, preferred_element_type=F32)
        l_pow[q] = jnp.where(strict, ab[:ck], 0.0)
        l_ak[q] = jnp.where(strict, ak[:ck], 0.0)
        m_s[q[0], q[1], :, 0:LANES] = jnp.where(incl, ab[ck:], 0.0)
        m_s[q[0], q[1], :, LANES:2 * LANES] = jnp.where(incl, ak[ck:], 0.0)
        t_inv[q] = eye_p + l_pow[q]
    n = 1
    while 2 * n < ck:
        for q in probs:
            l_pow[q] = _bdg(l_pow[q], bd(l_pow[q]), _NN)
        for q in probs:
            t_inv[q] = t_inv[q] + _bdg(t_inv[q], bd(l_pow[q]), _NN)
        n *= 2
    lak_v = {}
    for q in probs:
        rs, ls = sl[q]
        lak_v[q] = _bdg(l_ak[q], bd(v[rs, ls]), _NN)
    for q in probs:
        rs, ls = sl[q]
        wu = _bdg(t_inv[q], jnp.concatenate([bd(a_t[rs, ls]), bd(lak_v[q])], axis=1), _NN)
        wu_s[q[0], q[1]] = wu
        i, p = q
        half = jnp.zeros((ck, LANES), F32)
        place = (lambda x: jnp.concatenate([x, half], axis=0)) if i % 2 == 0 else \
                (lambda x: jnp.concatenate([half, x], axis=0))
        tcols = slice((i // 2) * LANES, (i // 2 + 1) * LANES)
        b_tr = bk_t[p * LANES:(p + 1) * LANES, tcols]
        k_tr = bk_t[D_BRANCH + p * LANES:D_BRANCH + (p + 1) * LANES, tcols]
        z_t = _bdg(b_tr, place(wu[:, 0:LANES]), _NN)
        y_t = _bdg(jnp.concatenate([b_tr, k_tr], axis=1),
                   jnp.concatenate([place(wu[:, LANES:2 * LANES]), place(v[rs, ls])], axis=0), _NN)
        zy_s[i, p, :, 0:LANES] = jnp.where(blockmask, z_t, 0.0)
        zy_s[i, p, :, LANES:2 * LANES] = jnp.where(blockmask, y_t, 0.0)

    for i in range(nck):
        rs = slice(i * ck, (i + 1) * ck)
        for p in range(RW_PAIRS):
            ls = slice(p * LANES, (p + 1) * LANES)
            h_p = g_s[p]
            h_bf = h_p.astype(BF16)
            gcol = jnp.sum(jnp.where(eye128, gam[(i + 1) * ck - 1:(i + 1) * ck, ls], 0.0), axis=1, keepdims=True)
            g_s[p] = (h_p * gcol + _dot(zy_s[i, p, :, 0:LANES].astype(BF16), h_bf)) + zy_s[i, p, :, LANES:2 * LANES]
            wu = wu_s[i, p]
            u = _dot(wu[:, 0:LANES].astype(BF16), h_bf) + wu[:, LANES:2 * LANES]
            o_s[rs, ls] = _dot(r_t[rs, ls].astype(BF16), h_bf) \
                + _bdg(m_s[i, p], jnp.concatenate([bd(u), bd(v[rs, ls])], axis=0), _NN)

    o_ref[...] = _rwkv_post(o_s[...], r, k2, v, g, rk_ref, lng_ref, lnb_ref, ones).astype(o_ref.dtype)
    gout_ref[...] = g_s[...]


def _rwkv_chunked(pa, shift0, s0, prm, batch, seq, nck):
    lb = nck * RW_CHUNK
    nstep = seq // lb
    s0p = s0.reshape(batch, RW_PAIRS, 2, A_HEAD, A_HEAD)
    eye2 = jnp.eye(2, dtype=s0.dtype)
    g0 = jnp.einsum("bpjvk,ji->bpjkiv", s0p, eye2).reshape(batch, RW_PAIRS, LANES, LANES)
    tri = np.kron(np.eye(nck), np.tril(np.ones((RW_CHUNK, RW_CHUNK)))).astype(np.float32)
    params = [prm[n] for n in ("mu", "w0", "w2", "a0", "a2", "g2", "kk", "ka", "rk", "lng", "lnb", "ones")]
    params.append(jnp.asarray(tri, BF16))

    def full(arr):
        nd = arr.ndim
        return pl.BlockSpec(arr.shape, lambda i, c: (0,) * nd)

    o, g_out = pl.pallas_call(
        functools.partial(_rwkv_chunk_kernel, nck),
        out_shape=(jax.ShapeDtypeStruct((batch * seq, D_BRANCH), BF16),
                   jax.ShapeDtypeStruct((batch, RW_PAIRS, LANES, LANES), F32)),
        grid=(batch, nstep),
        in_specs=[pl.BlockSpec((lb, A_PROJ), lambda i, c: (i * nstep + c, 0)),
                  pl.BlockSpec((None, 1, A_PROJ), lambda i, c: (i, 0, 0)),
                  pl.BlockSpec((None, RW_PAIRS, LANES, LANES), lambda i, c: (i, 0, 0, 0))]
                 + [full(p) for p in params],
        out_specs=(pl.BlockSpec((lb, D_BRANCH), lambda i, c: (i * nstep + c, 0)),
                   pl.BlockSpec((None, RW_PAIRS, LANES, LANES), lambda i, c: (i, 0, 0, 0))),
        scratch_shapes=[pltpu.VMEM((1, A_PROJ), F32),
                        pltpu.VMEM((lb, A_PROJ), F32),
                        pltpu.VMEM((RW_PAIRS, LANES, LANES), F32),
                        pltpu.VMEM((nck, RW_PAIRS, RW_CHUNK, 2 * LANES), F32),
                        pltpu.VMEM((nck, RW_PAIRS, RW_CHUNK, 2 * LANES), F32),
                        pltpu.VMEM((nck, RW_PAIRS, LANES, 2 * LANES), F32),
                        pltpu.VMEM((lb, D_BRANCH), F32)],
        compiler_params=_cparams(("arbitrary", "arbitrary")),
    )(pa, shift0.reshape(batch, 1, A_PROJ), g0, *params)
    g5 = g_out.reshape(batch, RW_PAIRS, 2, A_HEAD, 2, A_HEAD)
    s_new = jnp.stack([g5[:, :, 0, :, 0, :], g5[:, :, 1, :, 1, :]], axis=2).swapaxes(-1, -2)
    return o, s_new.reshape(batch, A_HEADS, A_HEAD, A_HEAD)


def _rwkv_step_kernel(nb, lb,
                      pa_ref, sh0_ref, s0_ref, mu_ref, w0_ref, w2_ref, a0_ref, a2_ref, g2_ref,
                      kk_ref, ka_ref, rk_ref, lng_ref, lnb_ref, ones_ref,
                      o_ref, sout_ref):
    ones = ones_ref[...]
    pa = pa_ref[...].reshape(lb * nb, A_PROJ)
    prev = jnp.concatenate([sh0_ref[...], pa[:(lb - 1) * nb, :]], axis=0)
    r, logw, k2, v, an, bn, g = _rwkv_pre(pa, prev, mu_ref, w0_ref, w2_ref, a0_ref, a2_ref,
                                          g2_ref, kk_ref, ka_ref, ones)
    w = jnp.exp(logw)
    srows = nb * A_HEAD
    rowi = lax.broadcasted_iota(jnp.int32, (srows, D_BRANCH), 0)
    lane = lax.broadcasted_iota(jnp.int32, (srows, D_BRANCH), 1)
    eye = (rowi & (A_HEAD - 1)) == (lane & (A_HEAD - 1))

    def per_seq(x, t):
        xt = x[t * nb:(t + 1) * nb, :]
        return jnp.concatenate([jnp.broadcast_to(xt[b:b + 1, :], (A_HEAD, D_BRANCH)) for b in range(nb)],
                               axis=0)

    s = s0_ref[...].reshape(srows, D_BRANCH)
    outs = []
    for t in range(lb):
        sa = _segsum(s * per_seq(an, t), ones)
        vcol = _segsum(jnp.where(eye, per_seq(v, t), 0.0), ones)
        s = s * per_seq(w, t) + sa * per_seq(bn, t) + vcol * per_seq(k2, t)
        out = _segsum(s * per_seq(r, t), ones)
        outs.append(jnp.sum(jnp.where(eye, out, 0.0).reshape(nb, A_HEAD, D_BRANCH), axis=1))
    o = jnp.concatenate(outs, axis=0)
    res = _rwkv_post(o, r, k2, v, g, rk_ref, lng_ref, lnb_ref, ones).astype(o_ref.dtype)
    o_ref[...] = res.reshape(lb, nb, D_BRANCH)
    sout_ref[...] = s.reshape(nb, A_HEAD, D_BRANCH)


def _rwkv_steps(pa, shift0, s0, prm, batch, seq, nb):
    pa_tm = pa.reshape(batch, seq, pa.shape[1]).transpose(1, 0, 2)
    s0k = s0.transpose(0, 2, 1, 3).reshape(batch, A_HEAD, D_BRANCH)

    def full(arr):
        nd = arr.ndim
        return pl.BlockSpec(arr.shape, lambda i: (0,) * nd)

    params = [prm[n] for n in ("mu", "w0", "w2", "a0", "a2", "g2", "kk", "ka", "rk", "lng", "lnb", "ones")]
    o, s_out = pl.pallas_call(
        functools.partial(_rwkv_step_kernel, nb, seq),
        out_shape=(jax.ShapeDtypeStruct((seq, batch, D_BRANCH), BF16),
                   jax.ShapeDtypeStruct((batch, A_HEAD, D_BRANCH), F32)),
        grid=(batch // nb,),
        in_specs=[pl.BlockSpec((seq, nb, A_PROJ), lambda i: (0, i, 0)),
                  pl.BlockSpec((nb, A_PROJ), lambda i: (i, 0)),
                  pl.BlockSpec((nb, A_HEAD, D_BRANCH), lambda i: (i, 0, 0))]
                 + [full(p) for p in params],
        out_specs=(pl.BlockSpec((seq, nb, D_BRANCH), lambda i: (0, i, 0)),
                   pl.BlockSpec((nb, A_HEAD, D_BRANCH), lambda i: (i, 0, 0))),
        compiler_params=_cparams(("arbitrary",)),
    )(pa_tm, shift0, s0k, *params)
    o = o.transpose(1, 0, 2).reshape(batch * seq, D_BRANCH)
    s_new = s_out.reshape(batch, A_HEAD, A_HEADS, A_HEAD).transpose(0, 2, 1, 3)
    return o, s_new


def _ret_kernel(nb, lb,
                rq_ref, cos_ref, sin_ref, dm_ref, kd_ref, qd_ref, cd_ref, s0_ref,
                o_ref, sout_ref, s_ref):
    c = pl.program_id(1)

    @pl.when(c == 0)
    def _():
        s_ref[...] = s0_ref[...]

    cos = cos_ref[...]
    sin = sin_ref[...]
    scale = R_HEAD ** -0.5
    half = R_HEAD // 2
    row8 = lax.broadcasted_iota(jnp.int32, (8, R_HEAD), 0)
    for h in range(R_HEADS):
        lo, hi = h * R_HEAD, (h + 1) * R_HEAD
        q = rq_ref[:, lo:hi]
        k = rq_ref[:, 512 + lo:512 + hi]
        v = rq_ref[:, 1024 + lo:1024 + hi]
        gate = rq_ref[:, 1536 + lo:1536 + hi]
        qh = q * cos + pltpu.roll(q, half, axis=1) * sin
        kh = (k * cos + pltpu.roll(k, half, axis=1) * sin) * scale
        scores = lax.dot_general(qh.astype(BF16), kh.astype(BF16), (((1,), (1,)), ((), ())),
                                 preferred_element_type=F32) * dm_ref[h]
        o = _bdot(scores, v)
        qd = qh * qd_ref[:, lo:hi]
        ku = kh * kd_ref[:, lo:hi]
        cd = cd_ref[h]
        if nb == 1:
            s = s_ref[0, h]
            o = o + _bdot(qd, s)
            upd = lax.dot_general(ku.astype(BF16), v.astype(BF16), (((0,), (0,)), ((), ())),
                                  preferred_element_type=F32)
            s_ref[0, h] = s * cd + upd
        else:
            per_tile = 8 // lb
            inter = []
            for i in range(RET_ROWS // 8):
                qd_t = qd[i * 8:(i + 1) * 8, :]
                ku_t = ku[i * 8:(i + 1) * 8, :]
                v_t = v[i * 8:(i + 1) * 8, :].astype(BF16)
                acc = None
                for j in range(per_tile):
                    b = i * per_tile + j
                    m = (row8 >= j * lb) & (row8 < (j + 1) * lb)
                    s = s_ref[b, h]
                    part = _bdot(jnp.where(m, qd_t, 0.0), s)
                    acc = part if acc is None else acc + part
                    upd = lax.dot_general(jnp.where(m, ku_t, 0.0).astype(BF16), v_t,
                                          (((0,), (0,)), ((), ())), preferred_element_type=F32)
                    s_ref[b, h] = s * cd + upd
                inter.append(acc)
            o = o + jnp.concatenate(inter, axis=0)
        oc = o - jnp.mean(o, axis=-1, keepdims=True)
        on = oc * lax.rsqrt(jnp.mean(oc * oc, axis=-1, keepdims=True) + R_NORM_EPS)
        o_ref[:, lo:hi] = (gate * jax.nn.sigmoid(gate) * on).astype(o_ref.dtype)
    sout_ref[...] = s_ref[...]


def _retention(rq, s0, tabs, batch, seq):
    lb = min(R_CHUNK, seq)
    nb = RET_ROWS // lb
    nchunk = seq // lb
    cos, sin, dm, kd, qd, cd = tabs
    ntab = cos.shape[0] // RET_ROWS

    def const(arr):
        nd = arr.ndim
        return pl.BlockSpec(arr.shape, lambda i, c: (0,) * nd)

    tab_idx = (lambda i, c: (c, 0)) if ntab > 1 else (lambda i, c: (0, 0))
    return pl.pallas_call(
        functools.partial(_ret_kernel, nb, lb),
        out_shape=(jax.ShapeDtypeStruct((batch * seq, D_BRANCH), BF16),
                   jax.ShapeDtypeStruct((batch, R_HEADS, R_HEAD, R_HEAD), F32)),
        grid=(batch // nb, nchunk),
        in_specs=[pl.BlockSpec((RET_ROWS, 4 * D_BRANCH), lambda i, c: (i * nchunk + c, 0)),
                  pl.BlockSpec((RET_ROWS, R_HEAD), tab_idx),
                  pl.BlockSpec((RET_ROWS, R_HEAD), tab_idx),
                  const(dm), const(kd), const(qd), const(cd),
                  pl.BlockSpec((nb, R_HEADS, R_HEAD, R_HEAD), lambda i, c: (i, 0, 0, 0))],
        out_specs=(pl.BlockSpec((RET_ROWS, D_BRANCH), lambda i, c: (i * nchunk + c, 0)),
                   pl.BlockSpec((nb, R_HEADS, R_HEAD, R_HEAD), lambda i, c: (i, 0, 0, 0))),
        scratch_shapes=[pltpu.VMEM((nb, R_HEADS, R_HEAD, R_HEAD), F32)],
        compiler_params=_cparams(("arbitrary", "arbitrary")),
    )(rq, cos, sin, dm, kd, qd, cd, s0)


def _ret_tables(seq, pos0):
    lb = min(R_CHUNK, seq)
    nb = RET_ROWS // lb
    half = R_HEAD // 2
    pos = pos0 + jnp.arange(seq, dtype=jnp.int32)
    inv = ROPE_BASE ** (-jnp.arange(half, dtype=F32) / half)
    ang = pos.astype(F32)[:, None] * inv[None, :]
    cos, sin = jnp.cos(ang), jnp.sin(ang)
    cos2 = jnp.concatenate([cos, cos], axis=-1)
    sin2 = jnp.concatenate([-sin, sin], axis=-1)
    if nb > 1:
        cos2 = jnp.tile(cos2, (nb, 1))
        sin2 = jnp.tile(sin2, (nb, 1))
    log_g = jnp.log1p(-jnp.exp2(-5.0 - jnp.arange(R_HEADS, dtype=F32)))
    idx = jnp.arange(lb, dtype=F32)
    diff = idx[:, None] - idx[None, :]
    dmask = jnp.where(diff >= 0, jnp.exp(log_g[:, None, None] * jnp.maximum(diff, 0.0)), 0.0)
    if nb > 1:
        dmask = jnp.einsum("ab,hij->haibj", jnp.eye(nb, dtype=F32), dmask).reshape(
            R_HEADS, RET_ROWS, RET_ROWS)
    k_decay = jnp.exp(log_g[None, :] * (lb - 1.0 - idx)[:, None])
    q_decay = jnp.exp(log_g[None, :] * (idx + 1.0)[:, None])
    kd = jnp.tile(jnp.repeat(k_decay, R_HEAD, axis=1), (nb, 1))
    qd = jnp.tile(jnp.repeat(q_decay, R_HEAD, axis=1), (nb, 1))
    cd = jnp.broadcast_to(jnp.exp(log_g * lb)[:, None, None], (R_HEADS, 1, R_HEAD))
    return cos2, sin2, dmask, kd, qd, cd


def _lru_kernel(nb, lb, three_d,
                cx_ref, conv0_ref, h0_ref, cw_ref, cb_ref, wri_ref, br_ref, bi_ref, lam_ref,
                o_ref, hout_ref,
                cc_ref, hc_ref, x1_ref, x2_ref, x3_ref, a_ref, b_ref):
    c = pl.program_id(1)
    rows = nb * lb

    @pl.when(c == 0)
    def _():
        cc_ref[...] = conv0_ref[...]
        hc_ref[...] = h0_ref[...]

    cx = cx_ref[...]
    if three_d:
        cx = cx.reshape(rows, 2 * D_BRANCH)
    xb = cx[:, :D_BRANCH]
    gb = cx[:, D_BRANCH:]
    x1_ref[...] = pltpu.roll(xb, 1, axis=0)
    x2_ref[...] = pltpu.roll(xb, 2, axis=0)
    x3_ref[...] = pltpu.roll(xb, 3, axis=0)
    for b in range(nb):
        r0 = b * lb
        c0 = cc_ref[b, 0:1, :]
        c1 = cc_ref[b, 1:2, :]
        c2 = cc_ref[b, 2:3, :]
        x1_ref[pl.ds(r0, 1), :] = c2
        x2_ref[pl.ds(r0, 1), :] = c1
        x2_ref[pl.ds(r0 + 1, 1), :] = c2
        x3_ref[pl.ds(r0, 1), :] = c0
        x3_ref[pl.ds(r0 + 1, 1), :] = c1
        x3_ref[pl.ds(r0 + 2, 1), :] = c2
        cc_ref[b] = xb[r0 + lb - 3:r0 + lb, :]
    cw = cw_ref[...]
    xc = cb_ref[...] + (((x3_ref[...] * cw[0:1, :] + x2_ref[...] * cw[1:2, :]) + x1_ref[...] * cw[2:3, :])
                        + xb * cw[3:4, :])
    ri = _bdot(xc, wri_ref[...])
    r = jax.nn.sigmoid(ri[:, :D_BRANCH] + br_ref[...])
    i = jax.nn.sigmoid(ri[:, D_BRANCH:] + bi_ref[...])
    log_a = LRU_C * r * jax.nn.log_sigmoid(lam_ref[...])
    a = jnp.exp(log_a)
    bb = jnp.sqrt(-jnp.tanh(log_a) * (a * a + 1.0)) * (i * xc)
    a_ref[...] = a
    b_ref[...] = bb
    for b in range(nb):
        r0 = b * lb
        b_ref[pl.ds(r0, 1), :] = bb[r0:r0 + 1, :] + a[r0:r0 + 1, :] * hc_ref[b]
    a = a_ref[...]
    bb = b_ref[...]
    t_idx = lax.broadcasted_iota(jnp.int32, (rows, D_BRANCH), 0) % lb
    s = 1
    while s < lb:
        keep = t_idx >= s
        a_sh = jnp.where(keep, pltpu.roll(a, s, axis=0), 1.0)
        b_sh = jnp.where(keep, pltpu.roll(bb, s, axis=0), 0.0)
        bb = a * b_sh + bb
        a = a * a_sh
        s *= 2
    h = bb
    for b in range(nb):
        r0 = b * lb
        hc_ref[b] = h[r0 + lb - 1:r0 + lb, :]
    res = (h * jax.nn.gelu(gb)).astype(o_ref.dtype)
    if three_d:
        res = res.reshape(nb, lb, D_BRANCH)
    o_ref[...] = res
    hout_ref[...] = hc_ref[...]


def _rglru(cx, conv0, h0, prm, batch, seq, nb, lb):
    nchunk = seq // lb
    rows = nb * lb
    three_d = nb > 1 and nchunk > 1
    if three_d:
        cx_in = cx.reshape(batch, seq, 2 * D_BRANCH)
        cx_spec = pl.BlockSpec((nb, lb, 2 * D_BRANCH), lambda i, c: (i, c, 0))
        o_shape = jax.ShapeDtypeStruct((batch, seq, D_BRANCH), BF16)
        o_spec = pl.BlockSpec((nb, lb, D_BRANCH), lambda i, c: (i, c, 0))
    else:
        cx_in = cx
        cx_spec = pl.BlockSpec((rows, 2 * D_BRANCH), lambda i, c: (i * nchunk + c, 0))
        o_shape = jax.ShapeDtypeStruct((batch * seq, D_BRANCH), BF16)
        o_spec = pl.BlockSpec((rows, D_BRANCH), lambda i, c: (i * nchunk + c, 0))

    def full(arr):
        nd = arr.ndim
        return pl.BlockSpec(arr.shape, lambda i, c: (0,) * nd)

    params = [prm[n] for n in ("cw", "cb", "wri", "br", "bi", "lam")]
    o, h_out = pl.pallas_call(
        functools.partial(_lru_kernel, nb, lb, three_d),
        out_shape=(o_shape, jax.ShapeDtypeStruct((batch, 1, D_BRANCH), F32)),
        grid=(batch // nb, nchunk),
        in_specs=[cx_spec,
                  pl.BlockSpec((nb, CONV_W - 1, D_BRANCH), lambda i, c: (i, 0, 0)),
                  pl.BlockSpec((nb, 1, D_BRANCH), lambda i, c: (i, 0, 0))]
                 + [full(p) for p in params],
        out_specs=(o_spec, pl.BlockSpec((nb, 1, D_BRANCH), lambda i, c: (i, 0, 0))),
        scratch_shapes=[pltpu.VMEM((nb, CONV_W - 1, D_BRANCH), F32),
                        pltpu.VMEM((nb, 1, D_BRANCH), F32)]
                       + [pltpu.VMEM((rows, D_BRANCH), F32) for _ in range(5)],
        compiler_params=_cparams(("arbitrary", "arbitrary")),
    )(cx_in, conv0, h0.reshape(batch, 1, D_BRANCH), *params)
    return o.reshape(batch * seq, D_BRANCH), h_out.reshape(batch, D_BRANCH)


def _merge_kernel(h_ref, oa_ref, ob_ref, oc_ref, x_ref, gt_ref, wgm_ref, wb_ref, wout_ref, o_ref):
    gm = _dot(h_ref[...], wgm_ref[...])
    merged = None
    for n, br_ref in enumerate((oa_ref, ob_ref, oc_ref)):
        br = _dot(br_ref[...], wb_ref[n])
        term = jax.nn.sigmoid(gm[:, n * D_MODEL:(n + 1) * D_MODEL]) * br
        merged = term if merged is None else merged + term
    y = _bdot(merged, wout_ref[...])
    o_ref[...] = x_ref[...] + gt_ref[...] * y


def _merge_out(h, oa, ob, oc, x, grp, w3, l, wb, wout, tm):
    row = lambda i: (i, 0)
    return pl.pallas_call(
        _merge_kernel,
        out_shape=jax.ShapeDtypeStruct((grp.rows, D_MODEL), F32),
        grid=(grp.rows // tm,),
        in_specs=[pl.BlockSpec((tm, D_MODEL), row),
                  pl.BlockSpec((tm, D_BRANCH), row),
                  pl.BlockSpec((tm, D_BRANCH), row),
                  pl.BlockSpec((tm, D_BRANCH), row),
                  pl.BlockSpec((tm, D_MODEL), row),
                  grp.mod_spec(2, tm),
                  pl.BlockSpec((None, D_MODEL, 3 * D_MODEL), lambda i: (l, 0, W_GM // (3 * D_MODEL))),
                  pl.BlockSpec(wb.shape, lambda i: (0, 0, 0)),
                  pl.BlockSpec(wout.shape, lambda i: (0, 0))],
        out_specs=pl.BlockSpec((tm, D_MODEL), row),
        compiler_params=_cparams(("arbitrary",)),
    )(h, oa, ob, oc, x, grp.mod4, w3, wb, wout)


def _route(probs, sel, lane):
    ge = EXPERTS_PER_GROUP
    pos = lane & (ge - 1)

    def member(x, k):
        return jnp.where(pos + k < ge, pltpu.roll(x, LANES - k, axis=1), pltpu.roll(x, ge - k, axis=1))

    n_ahead = jnp.zeros(sel.shape, jnp.int32)
    for k in range(1, ge):
        other = member(sel, k)
        lower_index = pos + k >= ge
        n_ahead = n_ahead + ((other > sel) | ((other == sel) & lower_index)).astype(jnp.int32)
    in_top2 = n_ahead < TOP_K
    kept = jnp.where(in_top2, sel, 0.0)
    score = kept
    for k in range(1, ge):
        score = score + member(kept, k)
    best = lane < N_EXPERTS
    for d in range(1, N_GROUPS):
        later = pltpu.roll(score, LANES - ge * d, axis=1)
        earlier = pltpu.roll(score, ge * d, axis=1)
        best = best & (later <= score) & (earlier < score)
    pk = jnp.where(best & in_top2, probs, 0.0)
    return pk / jnp.sum(pk, axis=-1, keepdims=True)


def _moe_kernel(x_ref, g_ref, sc_ref, sh_ref, gt_ref, wr_ref, rb_ref, wg_ref, wu_ref, wd_ref,
                o_ref, h_s, gate_s):
    e = pl.program_id(1)
    tm = x_ref.shape[0]
    lane = lax.broadcasted_iota(jnp.int32, (tm, LANES), 1)

    @pl.when(e == 0)
    def _():
        y = _rms(x_ref[...], g_ref[...])
        h = (y * (1.0 + sc_ref[...]) + sh_ref[...]).astype(BF16)
        h_s[...] = h
        logits = _dot(h, wr_ref[...])
        logits = jnp.where(lane < N_EXPERTS, logits, -jnp.inf)
        mx = jnp.max(logits, axis=-1, keepdims=True)
        ex = jnp.exp(logits - mx)
        probs = ex / jnp.sum(ex, axis=-1, keepdims=True)
        selv = jnp.where(lane < N_EXPERTS, probs + rb_ref[...], -jnp.inf)
        gate_s[...] = _route(probs, selv, lane)
        o_ref[...] = jnp.zeros_like(o_ref)

    h = h_s[...]
    gcol = jnp.sum(jnp.where(lane == e, gate_s[...], 0.0), axis=-1, keepdims=True)
    hg = _bdot(h, wg_ref[...])
    hu = _bdot(h, wu_ref[...])
    act = hg * jax.nn.sigmoid(hg) * hu * gcol
    o_ref[...] += _bdot(act, wd_ref[...])

    @pl.when(e == N_EXPERTS - 1)
    def _():
        o_ref[...] = x_ref[...] + gt_ref[...] * o_ref[...]


def _moe(x, g, grp, wr, rb, wg, wu, wd, l, tm):
    row = lambda i, e: (i, 0)
    return pl.pallas_call(
        _moe_kernel,
        out_shape=jax.ShapeDtypeStruct((grp.rows, D_MODEL), F32),
        grid=(grp.rows // tm, N_EXPERTS),
        in_specs=[pl.BlockSpec((tm, D_MODEL), row),
                  pl.BlockSpec((1, D_MODEL), lambda i, e: (0, 0)),
                  grp.mod_spec(4, tm),
                  grp.mod_spec(3, tm),
                  grp.mod_spec(5, tm),
                  pl.BlockSpec((D_MODEL, LANES), lambda i, e: (0, 0)),
                  pl.BlockSpec((1, LANES), lambda i, e: (0, 0)),
                  pl.BlockSpec((None, None, D_MODEL, D_EXPERT), lambda i, e: (l, e, 0, 0)),
                  pl.BlockSpec((None, None, D_MODEL, D_EXPERT), lambda i, e: (l, e, 0, 0)),
                  pl.BlockSpec((None, None, D_EXPERT, D_MODEL), lambda i, e: (l, e, 0, 0))],
        out_specs=pl.BlockSpec((tm, D_MODEL), row),
        scratch_shapes=[pltpu.VMEM((tm, D_MODEL), BF16),
                        pltpu.VMEM((tm, LANES), F32)],
        compiler_params=_cparams(("arbitrary", "arbitrary"), MOE_VMEM_LIMIT),
    )(x, g.reshape(1, D_MODEL), grp.mod4, grp.mod4, grp.mod4, wr, rb, wg, wu, wd)


def _block_diag(w):
    eye = jnp.eye(C_BLOCKS, dtype=w.dtype)
    return jnp.einsum("hg,hij->higj", eye, w).reshape(D_BRANCH, D_BRANCH)


def _layer_params(l, p):
    row = lambda a: a[l].reshape(1, -1)
    ones = jnp.asarray(np.kron(np.eye(4), np.ones((A_HEAD, A_HEAD))), BF16)
    pad_lo = lambda w: jnp.pad(w, ((0, 64), (0, 0)))
    pad_hi = lambda w: jnp.pad(w, ((64, 0), (0, 0)))
    rw = dict(mu=row(p["a_mu"]), w0=row(p["a_w0"]), w2=pad_lo(p["a_w2"][l]).astype(BF16),
              a0=row(p["a_a0"]), a2=pad_hi(p["a_a2"][l]).astype(BF16), g2=p["a_g2"][l].astype(BF16),
              kk=row(p["a_kk"]), ka=row(p["a_ka"]), rk=row(p["a_rk"]), lng=row(p["a_ln_g"]),
              lnb=row(p["a_ln_b"]), ones=ones)
    lru = dict(cw=p["c_conv_w"][l], cb=row(p["c_conv_b"]),
               wri=jnp.concatenate([_block_diag(p["c_wr"][l]), _block_diag(p["c_wi"][l])], axis=1).astype(BF16),
               br=row(p["c_br"]), bi=row(p["c_bi"]), lam=row(p["c_lam"]))
    return dict(
        rw=rw, lru=lru,
        w_branch=p["w_branch"][l].astype(BF16),
        w_out=p["w_out"][l].astype(BF16),
        wg=p["moe_wg"], wu=p["moe_wu"], wd=p["moe_wd"],
        norm_mix=p["norm_mix"][l], norm_ffn=p["norm_ffn"][l])


def kernel(x_prompt, x_sample, c_prompt, c_sample, state_rwkv_shift, state_rwkv_wkv, state_ret, state_lru_h, state_lru_conv, norm_mix, norm_ffn, norm_final, ada_w, ada_b, w_in, a_mu, a_w0, a_w2, a_a0, a_a2, a_g2, a_kk, a_ka, a_rk, a_ln_g, a_ln_b, c_conv_w, c_conv_b, c_wr, c_br, c_wi, c_bi, c_lam, w_branch, w_out, w_router, router_bias, moe_wg, moe_wu, moe_wd):
    p = dict(norm_mix=norm_mix, norm_ffn=norm_ffn, w_in=w_in, a_mu=a_mu, a_w0=a_w0, a_w2=a_w2,
             a_a0=a_a0, a_a2=a_a2, a_g2=a_g2, a_kk=a_kk, a_ka=a_ka, a_rk=a_rk, a_ln_g=a_ln_g,
             a_ln_b=a_ln_b, c_conv_w=c_conv_w, c_conv_b=c_conv_b, c_wr=c_wr, c_br=c_br, c_wi=c_wi,
             c_bi=c_bi, c_lam=c_lam, w_branch=w_branch, w_out=w_out, moe_wg=moe_wg, moe_wu=moe_wu,
             moe_wd=moe_wd)
    bp, lp_, _ = x_prompt.shape
    bs, ls, _ = x_sample.shape
    layers = [_layer_params(l, p) for l in range(DEPTH)]
    w3 = _arrange_w_in(w_in)
    router = (jnp.pad(w_router, ((0, 0), (0, LANES - N_EXPERTS))).astype(BF16),
              jnp.pad(router_bias, (0, LANES - N_EXPERTS)).reshape(1, LANES))

    n_c = bp + bs
    pad_c = (-n_c) % 16
    c_all = jnp.pad(jnp.concatenate([c_prompt, c_sample], axis=0), ((0, pad_c), (0, 0)))
    mods = _ada(c_all, ada_w, ada_b)

    def run(x, batch, seq, mod, states, pos0, cfg):
        xs = x.reshape(batch * seq, D_MODEL)
        grps = []
        for l in range(DEPTH):
            if cfg["per_token"]:
                m = jnp.repeat(mod[l], seq, axis=0).reshape(batch * seq, 6, D_MODEL)
                m4 = m.transpose(1, 0, 2)[None]
            else:
                m4 = mod[l].reshape(batch, 6, 1, D_MODEL)
            grps.append(_Group(batch, seq, m4))
        return _trunk_layers(xs, grps, states, layers, w3, router, norm_final, pos0, cfg)

    zeros = lambda s: jnp.zeros((DEPTH, bp) + s.shape[2:], x_prompt.dtype)
    st_prompt = (zeros(state_rwkv_shift), zeros(state_rwkv_wkv), zeros(state_ret),
                 zeros(state_lru_h), zeros(state_lru_conv))
    st_sample = (state_rwkv_shift, state_rwkv_wkv, state_ret, state_lru_h, state_lru_conv)
    cfg_p = dict(per_token=False, tm=min(512, lp_), tm_mg=min(256, lp_), tm_mm=min(1024, lp_),
                 tm_moe=min(1024, lp_), rwkv_nb=8, lru_nb=1, lru_lb=min(256, lp_))
    cfg_s = dict(per_token=True, tm=min(512, bs * ls), tm_mg=min(256, bs * ls), tm_mm=min(512, bs * ls),
                 tm_moe=min(512, bs * ls), rwkv_nb=16, lru_nb=16, lru_lb=ls)
    y_p, new_p = run(x_prompt, bp, lp_, mods[:, :bp], st_prompt, 0, cfg_p)
    y_s, new_s = run(x_sample, bs, ls, mods[:, bp:bp + bs], st_sample, PAST_LEN, cfg_s)
    return (y_p, y_s) + new_p + new_s


def _trunk_layers(x, grps, states, layers, w3, router, norm_final, pos0, cfg):
    batch, seq = grps[0].batch, grps[0].seq
    tabs = _ret_tables(seq, pos0)
    wr, rb = router
    outs = [[] for _ in range(5)]
    for l, lp in enumerate(layers):
        grp = grps[l]
        h = _norm_mod(x, lp["norm_mix"], grp, 1, 0, cfg["tm"])
        pa = _matmul(h, w3, l, W_PA, W_PA_PAD, cfg["tm_mm"], 1024)
        rq = _matmul(h, w3, l, W_RQ, 4 * D_BRANCH, cfg["tm_mm"], 1024)
        cx = _matmul(h, w3, l, W_CX, 2 * D_BRANCH, cfg["tm_mm"], 1024)
        if seq % RW_CHUNK == 0:
            o_a, wkv = _rwkv_chunked(pa, states[0][l], states[1][l], lp["rw"], batch, seq,
                                     min(4, seq // RW_CHUNK))
        else:
            o_a, wkv = _rwkv_steps(pa, states[0][l], states[1][l], lp["rw"], batch, seq, cfg["rwkv_nb"])
        o_b, ret = _retention(rq, states[2][l], tabs, batch, seq)
        o_c, lru_h = _rglru(cx, states[4][l], states[3][l], lp["lru"], batch, seq,
                            cfg["lru_nb"], cfg["lru_lb"])
        x = _merge_out(h, o_a, o_b, o_c, x, grp, w3, l, lp["w_branch"], lp["w_out"], cfg["tm_mg"])
        x = _moe(x, lp["norm_ffn"], grp, wr, rb, lp["wg"], lp["wu"], lp["wd"], l, cfg["tm_moe"])
        outs[0].append(pa.reshape(batch, seq, W_PA_PAD)[:, -1, :A_PROJ])
        outs[1].append(wkv)
        outs[2].append(ret)
        outs[3].append(lru_h)
        outs[4].append(cx.reshape(batch, seq, 2 * D_BRANCH)[:, seq - (CONV_W - 1):, :D_BRANCH])
    y = _final_norm(x, norm_final, cfg["tm"])
    return y.reshape(batch, seq, D_MODEL), tuple(jnp.stack(o) for o in outs)
```

```python
import functools
import math

import numpy as np
import jax
import jax.numpy as jnp
from jax import lax
from jax.experimental import pallas as pl
from jax.experimental.pallas import tpu as pltpu

F32 = jnp.float32
BF16 = jnp.bfloat16

D_MODEL = 1024
DEPTH = 2
PAST_LEN = 16384
D_BRANCH = 512
A_HEAD = 64
A_HEADS = 8
A_PROJ = 1792
A_NORM_EPS = 64e-5
A_KK_EPS = 1e-12
R_HEAD = 128
R_HEADS = 4
R_CHUNK = 128
R_NORM_EPS = 1e-6
ROPE_BASE = 10000.0
C_BLOCK = 64
C_BLOCKS = 8
CONV_W = 4
LRU_C = 8.0
N_EXPERTS = 16
N_GROUPS = 4
EXPERTS_PER_GROUP = 4
TOP_K = 2
D_EXPERT = 512
NORM_EPS = 1e-6

LANES = 128
RET_ROWS = 128
VMEM_LIMIT = 48 * 1024 * 1024
MOE_VMEM_LIMIT = 56 * 1024 * 1024


def _cparams(sem, vmem=VMEM_LIMIT):
    return pltpu.CompilerParams(dimension_semantics=sem, vmem_limit_bytes=vmem)


def _dot(a, b):
    return jnp.dot(a, b, preferred_element_type=F32)


def _bdot(a, b):
    return jnp.dot(a.astype(BF16), b.astype(BF16), preferred_element_type=F32)


def _split3(x):
    hi = x.astype(BF16)
    r1 = x - hi.astype(F32)
    mid = r1.astype(BF16)
    lo = (r1 - mid.astype(F32)).astype(BF16)
    return hi, mid, lo


def _segsum(x, ones_bf16):
    w = ones_bf16.shape[0]
    hi, mid, lo = _split3(x)
    parts = []
    for j in range(x.shape[1] // w):
        c = slice(j * w, (j + 1) * w)
        parts.append((_dot(lo[:, c], ones_bf16) + _dot(mid[:, c], ones_bf16)) + _dot(hi[:, c], ones_bf16))
    return jnp.concatenate(parts, axis=1)


def _ada_kernel(c_ref, w_ref, b_ref, o_ref):
    c = c_ref[...]
    s = c * jax.nn.sigmoid(c)
    o_ref[...] = _bdot(s, w_ref[...]) + b_ref[...]


def _ada(c, ada_w, ada_b):
    rows = c.shape[0]
    tn = 1536
    return pl.pallas_call(
        _ada_kernel,
        out_shape=jax.ShapeDtypeStruct((DEPTH, rows, 6 * D_MODEL), F32),
        grid=(DEPTH, 6 * D_MODEL // tn),
        in_specs=[
            pl.BlockSpec((rows, D_MODEL), lambda l, j: (0, 0)),
            pl.BlockSpec((None, D_MODEL, tn), lambda l, j: (l, 0, j)),
            pl.BlockSpec((None, 1, tn), lambda l, j: (l, 0, j)),
        ],
        out_specs=pl.BlockSpec((None, rows, tn), lambda l, j: (l, 0, j)),
        compiler_params=_cparams(("arbitrary", "arbitrary")),
    )(c, ada_w, ada_b.reshape(DEPTH, 1, 6 * D_MODEL))


class _Group:
    def __init__(self, batch, seq, mod4):
        self.batch = batch
        self.seq = seq
        self.rows = batch * seq
        self.mod4 = mod4
        self.per_token = mod4.shape[0] == 1 and mod4.shape[2] != 1

    def mod_spec(self, k, tm):
        if self.per_token:
            return pl.BlockSpec((None, None, tm, D_MODEL), lambda i, *_: (0, k, i, 0))
        seq = self.seq
        return pl.BlockSpec((None, None, 1, D_MODEL), lambda i, *_: ((i * tm) // seq, k, 0, 0))


def _rms(x, g):
    return x * lax.rsqrt(jnp.mean(x * x, axis=-1, keepdims=True) + NORM_EPS) * g


def _norm_mod_kernel(x_ref, g_ref, sc_ref, sh_ref, o_ref):
    y = _rms(x_ref[...], g_ref[...])
    o_ref[...] = (y * (1.0 + sc_ref[...]) + sh_ref[...]).astype(o_ref.dtype)


def _norm_kernel(x_ref, g_ref, o_ref):
    o_ref[...] = _rms(x_ref[...], g_ref[...]).astype(o_ref.dtype)


def _norm_mod(x, g, grp, k_sc, k_sh, tm):
    return pl.pallas_call(
        _norm_mod_kernel,
        out_shape=jax.ShapeDtypeStruct((grp.rows, D_MODEL), BF16),
        grid=(grp.rows // tm,),
        in_specs=[
            pl.BlockSpec((tm, D_MODEL), lambda i: (i, 0)),
            pl.BlockSpec((1, D_MODEL), lambda i: (0, 0)),
            grp.mod_spec(k_sc, tm),
            grp.mod_spec(k_sh, tm),
        ],
        out_specs=pl.BlockSpec((tm, D_MODEL), lambda i: (i, 0)),
        compiler_params=_cparams(("arbitrary",)),
    )(x, g.reshape(1, D_MODEL), grp.mod4, grp.mod4)


def _final_norm(x, g, tm):
    rows = x.shape[0]
    return pl.pallas_call(
        _norm_kernel,
        out_shape=jax.ShapeDtypeStruct((rows, D_MODEL), F32),
        grid=(rows // tm,),
        in_specs=[
            pl.BlockSpec((tm, D_MODEL), lambda i: (i, 0)),
            pl.BlockSpec((1, D_MODEL), lambda i: (0, 0)),
        ],
        out_specs=pl.BlockSpec((tm, D_MODEL), lambda i: (i, 0)),
        compiler_params=_cparams(("arbitrary",)),
    )(x, g.reshape(1, D_MODEL))


def _mm_kernel(a_ref, w_ref, o_ref):
    o_ref[...] = _dot(a_ref[...], w_ref[...])


def _matmul(a, w3, l, col0, n, tm, tn):
    rows, k = a.shape
    c0 = col0 // tn
    return pl.pallas_call(
        _mm_kernel,
        out_shape=jax.ShapeDtypeStruct((rows, n), F32),
        grid=(n // tn, rows // tm),
        in_specs=[
            pl.BlockSpec((tm, k), lambda j, i: (i, 0)),
            pl.BlockSpec((None, k, tn), lambda j, i: (l, 0, c0 + j)),
        ],
        out_specs=pl.BlockSpec((tm, tn), lambda j, i: (i, j)),
        compiler_params=_cparams(("arbitrary", "arbitrary")),
    )(a, w3)


W_RQ, W_CX, W_GM, W_PA, W_PA_PAD = 0, 2048, 3072, 6144, 2048


W_BLK = 256


def _arrange_kernel(src_ref, w_ref, o_ref):
    del src_ref
    j = pl.program_id(1)
    real = (W_PA + A_PROJ) // W_BLK

    @pl.when(j < real)
    def _():
        o_ref[...] = w_ref[...].astype(BF16)

    @pl.when(j >= real)
    def _():
        o_ref[...] = jnp.zeros_like(o_ref)


def _arrange_w_in(w_in):
    depth, k, n = w_in.shape
    nblk = n // W_BLK
    first = A_PROJ // W_BLK
    src = list(range(first, nblk)) + list(range(first))
    nout = (W_PA + W_PA_PAD) // W_BLK
    src = jnp.asarray(src + [0] * (nout - len(src)), jnp.int32)
    return pl.pallas_call(
        _arrange_kernel,
        out_shape=jax.ShapeDtypeStruct((depth, k, nout * W_BLK), BF16),
        grid_spec=pltpu.PrefetchScalarGridSpec(
            num_scalar_prefetch=1,
            grid=(depth, nout),
            in_specs=[pl.BlockSpec((None, k, W_BLK), lambda l, j, s: (l, 0, s[j]))],
            out_specs=pl.BlockSpec((None, k, W_BLK), lambda l, j, s: (l, 0, j))),
        compiler_params=_cparams(("arbitrary", "arbitrary")),
    )(src, w_in)


def _rwkv_pre(pa, prev, mu_ref, w0_ref, w2_ref, a0_ref, a2_ref, g2_ref, kk_ref, ka_ref, ones):
    pm = pa + (prev - pa) * mu_ref[...]
    r = pm[:, 0:512]
    k = pm[:, 512:1024]
    v = pm[:, 1024:1536]
    xwa = pm[:, 1536:1664]
    xg = pm[:, 1664:1792]
    w_log = -jax.nn.softplus(-(w0_ref[...] + _bdot(jnp.tanh(xwa), w2_ref[...]))) - 0.5
    logw = -jnp.exp(w_log)
    a = jax.nn.sigmoid(a0_ref[...] + _bdot(xwa, a2_ref[...]))
    g = _bdot(jax.nn.sigmoid(xg), g2_ref[...])
    kk = k * kk_ref[...]
    kk = kk * lax.rsqrt(_segsum(kk * kk, ones) + A_KK_EPS)
    k2 = k * (1.0 + (a - 1.0) * ka_ref[...])
    return r, logw, k2, v, -kk, kk * a, g


def _rwkv_post(o, r, k2, v, g, rk_ref, lng_ref, lnb_ref, ones):
    mean = _segsum(o, ones) * (1.0 / A_HEAD)
    oc = o - mean
    var = _segsum(oc * oc, ones) * (1.0 / A_HEAD)
    o = oc * lax.rsqrt(var + A_NORM_EPS) * lng_ref[...] + lnb_ref[...]
    bonus = _segsum(r * k2 * rk_ref[...], ones) * v
    return (o + bonus) * g


_NN = (((1,), (0,)), ((), ()))
_NT = (((1,), (1,)), ((), ()))
_TN = (((0,), (0,)), ((), ()))
RW_CHUNK = 64
RW_PAIRS = A_HEADS // 2


def _bdg(a, b, dims):
    return lax.dot_general(a.astype(BF16), b.astype(BF16), dims, preferred_element_type=F32)


def _rwkv_chunk_kernel(nck,
                       pa_ref, sh0_ref, g0_ref, mu_ref, w0_ref, w2_ref, a0_ref, a2_ref, g2_ref,
                       kk_ref, ka_ref, rk_ref, lng_ref, lnb_ref, ones_ref, tri_ref,
                       o_ref, gout_ref,
                       carry_ref, prev_ref, g_s, wu_s, m_s, zy_s, o_s):
    c = pl.program_id(1)
    ck = RW_CHUNK
    ones = ones_ref[...]

    @pl.when(c == 0)
    def _():
        carry_ref[...] = sh0_ref[...]
        g_s[...] = g0_ref[...]

    pa = pa_ref[...]
    rows = pa.shape[0]
    prev_ref[...] = pltpu.roll(pa, 1, axis=0)
    prev_ref[pl.ds(0, 1), :] = carry_ref[...]
    carry_ref[...] = pa[rows - 1:rows, :]
    r, logw, k2, v, an, bn, g = _rwkv_pre(pa, prev_ref[...], mu_ref, w0_ref, w2_ref, a0_ref, a2_ref,
                                          g2_ref, kk_ref, ka_ref, ones)

    tri = tri_ref[...]
    hi, mid, lo = _split3(logw)
    cum =(_dot(tri, lo) + _dot(tri, mid)) + _dot(tri, hi)
    cum_last = jnp.concatenate(
        [jnp.broadcast_to(cum[(i + 1) * ck - 1:(i + 1) * ck, :], (ck, D_BRANCH)) for i in range(nck)], axis=0)
    gam = jnp.exp(cum)
    inv = jnp.exp(-cum)
    to_end = jnp.exp(cum_last - cum)
    a_t = an * jnp.exp(cum - logw)
    b_t = bn * inv
    k_t = k2 * inv
    r_t = r * gam
    b_e = bn * to_end
    k_e = k2 * to_end

    lane = lax.broadcasted_iota(jnp.int32, (ck, LANES), 1)
    rowi = lax.broadcasted_iota(jnp.int32, (ck, LANES), 0)
    m0 = lane < A_HEAD
    coli = lane & (A_HEAD - 1)
    strict = rowi > coli
    incl = rowi >= coli
    eye_p = (rowi == coli).astype(F32)
    r128 = lax.broadcasted_iota(jnp.int32, (LANES, LANES), 0)
    c128 = lax.broadcasted_iota(jnp.int32, (LANES, LANES), 1)
    blockmask = (r128 < A_HEAD) == (c128 < A_HEAD)
    eye128 = r128 == c128
    bk_t = jnp.concatenate([b_e, k_e], axis=1).T

    def bd(q):
        q = q.astype(BF16)
        z = jnp.zeros_like(q)
        return jnp.concatenate([jnp.where(m0, q, z), jnp.where(m0, z, q)], axis=0)

    probs = [(i, p) for i in range(nck) for p in range(RW_PAIRS)]
    sl = {(i, p): (slice(i * ck, (i + 1) * ck), slice(p * LANES, (p + 1) * LANES)) for i, p in probs}
    l_pow, l_ak, t_inv = {}, {}, {}
    for q in probs:
        rs, ls = sl[q]
        lhs = jnp.concatenate([a_t[rs, ls], r_t[rs, ls]], axis=0).astype(BF16)
        ab = lax.dot_general(lhs, bd(b_t[rs, ls]), _NT, preferred_element_type=F32)
        ak = lax.dot_general(lhs, bd(k_t[rs, ls]), _NT, preferred_element_type=F32)
        l_pow[q] = jnp.where(strict, ab[:ck], 0.0)
        l_ak[q] = jnp.where(strict, ak[:ck], 0.0)
        m_s[q[0], q[1], :, 0:LANES] = jnp.where(incl, ab[ck:], 0.0)
        m_s[q[0], q[1], :, LANES:2 * LANES] = jnp.where(incl, ak[ck:], 0.0)
        t_inv[q] = eye_p + l_pow[q]
    n = 1
    while 2 * n < ck:
        for q in probs:
            l_pow[q] = _bdg(l_pow[q], bd(l_pow[q]), _NN)
        for q in probs:
            t_inv[q] = t_inv[q] + _bdg(t_inv[q], bd(l_pow[q]), _NN)
        n *= 2
    lak_v = {}
    for q in probs:
        rs, ls = sl[q]
        lak_v[q] = _bdg(l_ak[q], bd(v[rs, ls]), _NN)
    for q in probs:
        rs, ls = sl[q]
        wu = _bdg(t_inv[q], jnp.concatenate([bd(a_t[rs, ls]), bd(lak_v[q])], axis=1), _NN)
        wu_s[q[0], q[1]] = wu
        i, p = q
        half = jnp.zeros((ck, LANES), F32)
        place = (lambda x: jnp.concatenate([x, half], axis=0)) if i % 2 == 0 else \
                (lambda x: jnp.concatenate([half, x], axis=0))
        tcols = slice((i // 2) * LANES, (i // 2 + 1) * LANES)
        b_tr = bk_t[p * LANES:(p + 1) * LANES, tcols]
        k_tr = bk_t[D_BRANCH + p * LANES:D_BRANCH + (p + 1) * LANES, tcols]
        z_t = _bdg(b_tr, place(wu[:, 0:LANES]), _NN)
        y_t = _bdg(jnp.concatenate([b_tr, k_tr], axis=1),
                   jnp.concatenate([place(wu[:, LANES:2 * LANES]), place(v[rs, ls])], axis=0), _NN)
        zy_s[i, p, :, 0:LANES] = jnp.where(blockmask, z_t, 0.0)
        zy_s[i, p, :, LANES:2 * LANES] = jnp.where(blockmask, y_t, 0.0)

    for i in range(nck):
        rs = slice(i * ck, (i + 1) * ck)
        for p in range(RW_PAIRS):
            ls = slice(p * LANES, (p + 1) * LANES)
            h_p = g_s[p]
            h_bf = h_p.astype(BF16)
            gcol = jnp.sum(jnp.where(eye128, gam[(i + 1) * ck - 1:(i + 1) * ck, ls], 0.0), axis=1, keepdims=True)
            g_s[p] = (h_p * gcol + _dot(zy_s[i, p, :, 0:LANES].astype(BF16), h_bf)) + zy_s[i, p, :, LANES:2 * LANES]
            wu = wu_s[i, p]
            u = _dot(wu[:, 0:LANES].astype(BF16), h_bf) + wu[:, LANES:2 * LANES]
            o_s[rs, ls] = _dot(r_t[rs, ls].astype(BF16), h_bf) \
                + _bdg(m_s[i, p], jnp.concatenate([bd(u), bd(v[rs, ls])], axis=0), _NN)

    o_ref[...] = _rwkv_post(o_s[...], r, k2, v, g, rk_ref, lng_ref, lnb_ref, ones).astype(o_ref.dtype)
    gout_ref[...] = g_s[...]


def _rwkv_chunked(pa, shift0, s0, prm, batch, seq, nck):
    lb = nck * RW_CHUNK
    nstep = seq // lb
    s0p = s0.reshape(batch, RW_PAIRS, 2, A_HEAD, A_HEAD)
    eye2 = jnp.eye(2, dtype=s0.dtype)
    g0 = jnp.einsum("bpjvk,ji->bpjkiv", s0p, eye2).reshape(batch, RW_PAIRS, LANES, LANES)
    tri = np.kron(np.eye(nck), np.tril(np.ones((RW_CHUNK, RW_CHUNK)))).astype(np.float32)
    params = [prm[n] for n in ("mu", "w0", "w2", "a0", "a2", "g2", "kk", "ka", "rk", "lng", "lnb", "ones")]
    params.append(jnp.asarray(tri, BF16))

    def full(arr):
        nd = arr.ndim
        return pl.BlockSpec(arr.shape, lambda i, c: (0,) * nd)

    o, g_out = pl.pallas_call(
        functools.partial(_rwkv_chunk_kernel, nck),
        out_shape=(jax.ShapeDtypeStruct((batch * seq, D_BRANCH), BF16),
                   jax.ShapeDtypeStruct((batch, RW_PAIRS, LANES, LANES), F32)),
        grid=(batch, nstep),
        in_specs=[pl.BlockSpec((lb, A_PROJ), lambda i, c: (i * nstep + c, 0)),
                  pl.BlockSpec((None, 1, A_PROJ), lambda i, c: (i, 0, 0)),
                  pl.BlockSpec((None, RW_PAIRS, LANES, LANES), lambda i, c: (i, 0, 0, 0))]
                 + [full(p) for p in params],
        out_specs=(pl.BlockSpec((lb, D_BRANCH), lambda i, c: (i * nstep + c, 0)),
                   pl.BlockSpec((None, RW_PAIRS, LANES, LANES), lambda i, c: (i, 0, 0, 0))),
        scratch_shapes=[pltpu.VMEM((1, A_PROJ), F32),
                        pltpu.VMEM((lb, A_PROJ), F32),
                        pltpu.VMEM((RW_PAIRS, LANES, LANES), F32),
                        pltpu.VMEM((nck, RW_PAIRS, RW_CHUNK, 2 * LANES), F32),
                        pltpu.VMEM((nck, RW_PAIRS, RW_CHUNK, 2 * LANES), F32),
                        pltpu.VMEM((nck, RW_PAIRS, LANES, 2 * LANES), F32),
                        pltpu.VMEM((lb, D_BRANCH), F32)],
        compiler_params=_cparams(("arbitrary", "arbitrary")),
    )(pa, shift0.reshape(batch, 1, A_PROJ), g0, *params)
    g5 = g_out.reshape(batch, RW_PAIRS, 2, A_HEAD, 2, A_HEAD)
    s_new = jnp.stack([g5[:, :, 0, :, 0, :], g5[:, :, 1, :, 1, :]], axis=2).swapaxes(-1, -2)
    return o, s_new.reshape(batch, A_HEADS, A_HEAD, A_HEAD)


def _rwkv_step_kernel(nb, lb,
                      pa_ref, sh0_ref, s0_ref, mu_ref, w0_ref, w2_ref, a0_ref, a2_ref, g2_ref,
                      kk_ref, ka_ref, rk_ref, lng_ref, lnb_ref, ones_ref,
                      o_ref, sout_ref):
    ones = ones_ref[...]
    pa = pa_ref[...].reshape(lb * nb, A_PROJ)
    prev = jnp.concatenate([sh0_ref[...], pa[:(lb - 1) * nb, :]], axis=0)
    r, logw, k2, v, an, bn, g = _rwkv_pre(pa, prev, mu_ref, w0_ref, w2_ref, a0_ref, a2_ref,
                                          g2_ref, kk_ref, ka_ref, ones)
    w = jnp.exp(logw)
    srows = nb * A_HEAD
    rowi = lax.broadcasted_iota(jnp.int32, (srows, D_BRANCH), 0)
    lane = lax.broadcasted_iota(jnp.int32, (srows, D_BRANCH), 1)
    eye = (rowi & (A_HEAD - 1)) == (lane & (A_HEAD - 1))

    def per_seq(x, t):
        xt = x[t * nb:(t + 1) * nb, :]
        return jnp.concatenate([jnp.broadcast_to(xt[b:b + 1, :], (A_HEAD, D_BRANCH)) for b in range(nb)],
                               axis=0)

    s = s0_ref[...].reshape(srows, D_BRANCH)
    outs = []
    for t in range(lb):
        sa = _segsum(s * per_seq(an, t), ones)
        vcol = _segsum(jnp.where(eye, per_seq(v, t), 0.0), ones)
        s = s * per_seq(w, t) + sa * per_seq(bn, t) + vcol * per_seq(k2, t)
        out = _segsum(s * per_seq(r, t), ones)
        outs.append(jnp.sum(jnp.where(eye, out, 0.0).reshape(nb, A_HEAD, D_BRANCH), axis=1))
    o = jnp.concatenate(outs, axis=0)
    res = _rwkv_post(o, r, k2, v, g, rk_ref, lng_ref, lnb_ref, ones).astype(o_ref.dtype)
    o_ref[...] = res.reshape(lb, nb, D_BRANCH)
    sout_ref[...] = s.reshape(nb, A_HEAD, D_BRANCH)


def _rwkv_steps(pa, shift0, s0, prm, batch, seq, nb):
    pa_tm = pa.reshape(batch, seq, pa.shape[1]).transpose(1, 0, 2)
    s0k = s0.transpose(0, 2, 1, 3).reshape(batch, A_HEAD, D_BRANCH)

    def full(arr):
        nd = arr.ndim
        return pl.BlockSpec(arr.shape, lambda i: (0,) * nd)

    params = [prm[n] for n in ("mu", "w0", "w2", "a0", "a2", "g2", "kk", "ka", "rk", "lng", "lnb", "ones")]
    o, s_out = pl.pallas_call(
        functools.partial(_rwkv_step_kernel, nb, seq),
        out_shape=(jax.ShapeDtypeStruct((seq, batch, D_BRANCH), BF16),
                   jax.ShapeDtypeStruct((batch, A_HEAD, D_BRANCH), F32)),
        grid=(batch // nb,),
        in_specs=[pl.BlockSpec((seq, nb, A_PROJ), lambda i: (0, i, 0)),
                  pl.BlockSpec((nb, A_PROJ), lambda i: (i, 0)),
                  pl.BlockSpec((nb, A_HEAD, D_BRANCH), lambda i: (i, 0, 0))]
                 + [full(p) for p in params],
        out_specs=(pl.BlockSpec((seq, nb, D_BRANCH), lambda i: (0, i, 0)),
                   pl.BlockSpec((nb, A_HEAD, D_BRANCH), lambda i: (i, 0, 0))),
        compiler_params=_cparams(("arbitrary",)),
    )(pa_tm, shift0, s0k, *params)
    o = o.transpose(1, 0, 2).reshape(batch * seq, D_BRANCH)
    s_new = s_out.reshape(batch, A_HEAD, A_HEADS, A_HEAD).transpose(0, 2, 1, 3)
    return o, s_new


def _ret_kernel(nb, lb,
                rq_ref, cos_ref, sin_ref, dm_ref, kd_ref, qd_ref, cd_ref, s0_ref,
                o_ref, sout_ref, s_ref):
    c = pl.program_id(1)

    @pl.when(c == 0)
    def _():
        s_ref[...] = s0_ref[...]

    cos = cos_ref[...]
    sin = sin_ref[...]
    scale = R_HEAD ** -0.5
    half = R_HEAD // 2
    row8 = lax.broadcasted_iota(jnp.int32, (8, R_HEAD), 0)
    for h in range(R_HEADS):
        lo, hi = h * R_HEAD, (h + 1) * R_HEAD
        q = rq_ref[:, lo:hi]
        k = rq_ref[:, 512 + lo:512 + hi]
        v = rq_ref[:, 1024 + lo:1024 + hi]
        gate = rq_ref[:, 1536 + lo:1536 + hi]
        qh = q * cos + pltpu.roll(q, half, axis=1) * sin
        kh = (k * cos + pltpu.roll(k, half, axis=1) * sin) * scale
        scores = lax.dot_general(qh.astype(BF16), kh.astype(BF16), (((1,), (1,)), ((), ())),
                                 preferred_element_type=F32) * dm_ref[h]
        o = _bdot(scores, v)
        qd = qh * qd_ref[:, lo:hi]
        ku = kh * kd_ref[:, lo:hi]
        cd = cd_ref[h]
        if nb == 1:
            s = s_ref[0, h]
            o = o + _bdot(qd, s)
            upd = lax.dot_general(ku.astype(BF16), v.astype(BF16), (((0,), (0,)), ((), ())),
                                  preferred_element_type=F32)
            s_ref[0, h] = s * cd + upd
        else:
            per_tile = 8 // lb
            inter = []
            for i in range(RET_ROWS // 8):
                qd_t = qd[i * 8:(i + 1) * 8, :]
                ku_t = ku[i * 8:(i + 1) * 8, :]
                v_t = v[i * 8:(i + 1) * 8, :].astype(BF16)
                acc = None
                for j in range(per_tile):
                    b = i * per_tile + j
                    m = (row8 >= j * lb) & (row8 < (j + 1) * lb)
                    s = s_ref[b, h]
                    part = _bdot(jnp.where(m, qd_t, 0.0), s)
                    acc = part if acc is None else acc + part
                    upd = lax.dot_general(jnp.where(m, ku_t, 0.0).astype(BF16), v_t,
                                          (((0,), (0,)), ((), ())), preferred_element_type=F32)
                    s_ref[b, h] = s * cd + upd
                inter.append(acc)
            o = o + jnp.concatenate(inter, axis=0)
        oc = o - jnp.mean(o, axis=-1, keepdims=True)
        on = oc * lax.rsqrt(jnp.mean(oc * oc, axis=-1, keepdims=True) + R_NORM_EPS)
        o_ref[:, lo:hi] = (gate * jax.nn.sigmoid(gate) * on).astype(o_ref.dtype)
    sout_ref[...] = s_ref[...]


def _retention(rq, s0, tabs, batch, seq):
    lb = min(R_CHUNK, seq)
    nb = RET_ROWS // lb
    nchunk = seq // lb
    cos, sin, dm, kd, qd, cd = tabs
    ntab = cos.shape[0] // RET_ROWS

    def const(arr):
        nd = arr.ndim
        return pl.BlockSpec(arr.shape, lambda i, c: (0,) * nd)

    tab_idx = (lambda i, c: (c, 0)) if ntab > 1 else (lambda i, c: (0, 0))
    return pl.pallas_call(
        functools.partial(_ret_kernel, nb, lb),
        out_shape=(jax.ShapeDtypeStruct((batch * seq, D_BRANCH), BF16),
                   jax.ShapeDtypeStruct((batch, R_HEADS, R_HEAD, R_HEAD), F32)),
        grid=(batch // nb, nchunk),
        in_specs=[pl.BlockSpec((RET_ROWS, 4 * D_BRANCH), lambda i, c: (i * nchunk + c, 0)),
                  pl.BlockSpec((RET_ROWS, R_HEAD), tab_idx),
                  pl.BlockSpec((RET_ROWS, R_HEAD), tab_idx),
                  const(dm), const(kd), const(qd), const(cd),
                  pl.BlockSpec((nb, R_HEADS, R_HEAD, R_HEAD), lambda i, c: (i, 0, 0, 0))],
        out_specs=(pl.BlockSpec((RET_ROWS, D_BRANCH), lambda i, c: (i * nchunk + c, 0)),
                   pl.BlockSpec((nb, R_HEADS, R_HEAD, R_HEAD), lambda i, c: (i, 0, 0, 0))),
        scratch_shapes=[pltpu.VMEM((nb, R_HEADS, R_HEAD, R_HEAD), F32)],
        compiler_params=_cparams(("arbitrary", "arbitrary")),
    )(rq, cos, sin, dm, kd, qd, cd, s0)


def _ret_tables(seq, pos0):
    lb = min(R_CHUNK, seq)
    nb = RET_ROWS // lb
    half = R_HEAD // 2
    pos = pos0 + jnp.arange(seq, dtype=jnp.int32)
    inv = ROPE_BASE ** (-jnp.arange(half, dtype=F32) / half)
    ang = pos.astype(F32)[:, None] * inv[None, :]
    cos, sin = jnp.cos(ang), jnp.sin(ang)
    cos2 = jnp.concatenate([cos, cos], axis=-1)
    sin2 = jnp.concatenate([-sin, sin], axis=-1)
    if nb > 1:
        cos2 = jnp.tile(cos2, (nb, 1))
        sin2 = jnp.tile(sin2, (nb, 1))
    log_g = jnp.log1p(-jnp.exp2(-5.0 - jnp.arange(R_HEADS, dtype=F32)))
    idx = jnp.arange(lb, dtype=F32)
    diff = idx[:, None] - idx[None, :]
    dmask = jnp.where(diff >= 0, jnp.exp(log_g[:, None, None] * jnp.maximum(diff, 0.0)), 0.0)
    if nb > 1:
        dmask = jnp.einsum("ab,hij->haibj", jnp.eye(nb, dtype=F32), dmask).reshape(
            R_HEADS, RET_ROWS, RET_ROWS)
    k_decay = jnp.exp(log_g[None, :] * (lb - 1.0 - idx)[:, None])
    q_decay = jnp.exp(log_g[None, :] * (idx + 1.0)[:, None])
    kd = jnp.tile(jnp.repeat(k_decay, R_HEAD, axis=1), (nb, 1))
    qd = jnp.tile(jnp.repeat(q_decay, R_HEAD, axis=1), (nb, 1))
    cd = jnp.broadcast_to(jnp.exp(log_g * lb)[:, None, None], (R_HEADS, 1, R_HEAD))
    return cos2, sin2, dmask, kd, qd, cd


def _lru_kernel(nb, lb, three_d,
                cx_ref, conv0_ref, h0_ref, cw_ref, cb_ref, wri_ref, br_ref, bi_ref, lam_ref,
                o_ref, hout_ref,
                cc_ref, hc_ref, x1_ref, x2_ref, x3_ref, a_ref, b_ref):
    c = pl.program_id(1)
    rows = nb * lb

    @pl.when(c == 0)
    def _():
        cc_ref[...] = conv0_ref[...]
        hc_ref[...] = h0_ref[...]

    cx = cx_ref[...]
    if three_d:
        cx = cx.reshape(rows, 2 * D_BRANCH)
    xb = cx[:, :D_BRANCH]
    gb = cx[:, D_BRANCH:]
    x1_ref[...] = pltpu.roll(xb, 1, axis=0)
    x2_ref[...] = pltpu.roll(xb, 2, axis=0)
    x3_ref[...] = pltpu.roll(xb, 3, axis=0)
    for b in range(nb):
        r0 = b * lb
        c0 = cc_ref[b, 0:1, :]
        c1 = cc_ref[b, 1:2, :]
        c2 = cc_ref[b, 2:3, :]
        x1_ref[pl.ds(r0, 1), :] = c2
        x2_ref[pl.ds(r0, 1), :] = c1
        x2_ref[pl.ds(r0 + 1, 1), :] = c2
        x3_ref[pl.ds(r0, 1), :] = c0
        x3_ref[pl.ds(r0 + 1, 1), :] = c1
        x3_ref[pl.ds(r0 + 2, 1), :] = c2
        cc_ref[b] = xb[r0 + lb - 3:r0 + lb, :]
    cw = cw_ref[...]
    xc = cb_ref[...] + (((x3_ref[...] * cw[0:1, :] + x2_ref[...] * cw[1:2, :]) + x1_ref[...] * cw[2:3, :])
                        + xb * cw[3:4, :])
    ri = _bdot(xc, wri_ref[...])
    r = jax.nn.sigmoid(ri[:, :D_BRANCH] + br_ref[...])
    i = jax.nn.sigmoid(ri[:, D_BRANCH:] + bi_ref[...])
    log_a = LRU_C * r * jax.nn.log_sigmoid(lam_ref[...])
    a = jnp.exp(log_a)
    bb = jnp.sqrt(-jnp.tanh(log_a) * (a * a + 1.0)) * (i * xc)
    a_ref[...] = a
    b_ref[...] = bb
    for b in range(nb):
        r0 = b * lb
        b_ref[pl.ds(r0, 1), :] = bb[r0:r0 + 1, :] + a[r0:r0 + 1, :] * hc_ref[b]
    a = a_ref[...]
    bb = b_ref[...]
    t_idx = lax.broadcasted_iota(jnp.int32, (rows, D_BRANCH), 0) % lb
    s = 1
    while s < lb:
        keep = t_idx >= s
        a_sh = jnp.where(keep, pltpu.roll(a, s, axis=0), 1.0)
        b_sh = jnp.where(keep, pltpu.roll(bb, s, axis=0), 0.0)
        bb = a * b_sh + bb
        a = a * a_sh
        s *= 2
    h = bb
    for b in range(nb):
        r0 = b * lb
        hc_ref[b] = h[r0 + lb - 1:r0 + lb, :]
    res = (h * jax.nn.gelu(gb)).astype(o_ref.dtype)
    if three_d:
        res = res.reshape(nb, lb, D_BRANCH)
    o_ref[...] = res
    hout_ref[...] = hc_ref[...]


def _rglru(cx, conv0, h0, prm, batch, seq, nb, lb):
    nchunk = seq // lb
    rows = nb * lb
    three_d = nb > 1 and nchunk > 1
    if three_d:
        cx_in = cx.reshape(batch, seq, 2 * D_BRANCH)
        cx_spec = pl.BlockSpec((nb, lb, 2 * D_BRANCH), lambda i, c: (i, c, 0))
        o_shape = jax.ShapeDtypeStruct((batch, seq, D_BRANCH), BF16)
        o_spec = pl.BlockSpec((nb, lb, D_BRANCH), lambda i, c: (i, c, 0))
    else:
        cx_in = cx
        cx_spec = pl.BlockSpec((rows, 2 * D_BRANCH), lambda i, c: (i * nchunk + c, 0))
        o_shape = jax.ShapeDtypeStruct((batch * seq, D_BRANCH), BF16)
        o_spec = pl.BlockSpec((rows, D_BRANCH), lambda i, c: (i * nchunk + c, 0))

    def full(arr):
        nd = arr.ndim
        return pl.BlockSpec(arr.shape, lambda i, c: (0,) * nd)

    params = [prm[n] for n in ("cw", "cb", "wri", "br", "bi", "lam")]
    o, h_out = pl.pallas_call(
        functools.partial(_lru_kernel, nb, lb, three_d),
        out_shape=(o_shape, jax.ShapeDtypeStruct((batch, 1, D_BRANCH), F32)),
        grid=(batch // nb, nchunk),
        in_specs=[cx_spec,
                  pl.BlockSpec((nb, CONV_W - 1, D_BRANCH), lambda i, c: (i, 0, 0)),
                  pl.BlockSpec((nb, 1, D_BRANCH), lambda i, c: (i, 0, 0))]
                 + [full(p) for p in params],
        out_specs=(o_spec, pl.BlockSpec((nb, 1, D_BRANCH), lambda i, c: (i, 0, 0))),
        scratch_shapes=[pltpu.VMEM((nb, CONV_W - 1, D_BRANCH), F32),
                        pltpu.VMEM((nb, 1, D_BRANCH), F32)]
                       + [pltpu.VMEM((rows, D_BRANCH), F32) for _ in range(5)],
        compiler_params=_cparams(("arbitrary", "arbitrary")),
    )(cx_in, conv0, h0.reshape(batch, 1, D_BRANCH), *params)
    return o.reshape(batch * seq, D_BRANCH), h_out.reshape(batch, D_BRANCH)


def _merge_kernel(h_ref, oa_ref, ob_ref, oc_ref, x_ref, gt_ref, wgm_ref, wb_ref, wout_ref, o_ref):
    gm = _dot(h_ref[...], wgm_ref[...])
    merged = None
    for n, br_ref in enumerate((oa_ref, ob_ref, oc_ref)):
        br = _dot(br_ref[...], wb_ref[n])
        term = jax.nn.sigmoid(gm[:, n * D_MODEL:(n + 1) * D_MODEL]) * br
        merged = term if merged is None else merged + term
    y = _bdot(merged, wout_ref[...])
    o_ref[...] = x_ref[...] + gt_ref[...] * y


def _merge_out(h, oa, ob, oc, x, grp, w3, l, wb, wout, tm):
    row = lambda i: (i, 0)
    return pl.pallas_call(
        _merge_kernel,
        out_shape=jax.ShapeDtypeStruct((grp.rows, D_MODEL), F32),
        grid=(grp.rows // tm,),
        in_specs=[pl.BlockSpec((tm, D_MODEL), row),
                  pl.BlockSpec((tm, D_BRANCH), row),
                  pl.BlockSpec((tm, D_BRANCH), row),
                  pl.BlockSpec((tm, D_BRANCH), row),
                  pl.BlockSpec((tm, D_MODEL), row),
                  grp.mod_spec(2, tm),
                  pl.BlockSpec((None, D_MODEL, 3 * D_MODEL), lambda i: (l, 0, W_GM // (3 * D_MODEL))),
                  pl.BlockSpec(wb.shape, lambda i: (0, 0, 0)),
                  pl.BlockSpec(wout.shape, lambda i: (0, 0))],
        out_specs=pl.BlockSpec((tm, D_MODEL), row),
        compiler_params=_cparams(("arbitrary",)),
    )(h, oa, ob, oc, x, grp.mod4, w3, wb, wout)


def _group_member(x, k, lane):
    ge = EXPERTS_PER_GROUP
    pos = lane & (ge - 1)
    return jnp.where(pos + k < ge, pltpu.roll(x, LANES - k, axis=1), pltpu.roll(x, ge - k, axis=1))


def _in_group_top2(sel, lane):
    ge = EXPERTS_PER_GROUP
    pos = lane & (ge - 1)
    n_ahead = jnp.zeros(sel.shape, jnp.int32)
    for k in range(1, ge):
        other = _group_member(sel, k, lane)
        lower_index = pos + k >= ge
        n_ahead = n_ahead + ((other > sel) | ((other == sel) & lower_index)).astype(jnp.int32)
    return n_ahead < TOP_K


def _best_group(sel, in_top2, lane):
    ge = EXPERTS_PER_GROUP
    kept = jnp.where(in_top2, sel, 0.0)
    score = kept
    for k in range(1, ge):
        score = score + _group_member(kept, k, lane)
    best = lane < N_EXPERTS
    for d in range(1, N_GROUPS):
        later = pltpu.roll(score, LANES - ge * d, axis=1)
        earlier = pltpu.roll(score, ge * d, axis=1)
        best = best & (later <= score) & (earlier < score)
    return best


def _router_probs(h_bf16, wr_ref, rb_ref, lane):
    logits = _dot(h_bf16, wr_ref[...])
    logits = jnp.where(lane < N_EXPERTS, logits, -jnp.inf)
    mx = jnp.max(logits, axis=-1, keepdims=True)
    ex = jnp.exp(logits - mx)
    probs = ex / jnp.sum(ex, axis=-1, keepdims=True)
    sel = jnp.where(lane < N_EXPERTS, probs + rb_ref[...], -jnp.inf)
    return probs, sel


def _route(probs, sel, lane):
    in_top2 = _in_group_top2(sel, lane)
    pk = jnp.where(_best_group(sel, in_top2, lane) & in_top2, probs, 0.0)
    return pk / jnp.sum(pk, axis=-1, keepdims=True)


def _moe_kernel(x_ref, g_ref, sc_ref, sh_ref, gt_ref, wr_ref, rb_ref, wg_ref, wu_ref, wd_ref,
                o_ref, h_s, gate_s):
    e = pl.program_id(1)
    tm = x_ref.shape[0]
    lane = lax.broadcasted_iota(jnp.int32, (tm, LANES), 1)

    @pl.when(e == 0)
    def _():
        y = _rms(x_ref[...], g_ref[...])
        h = (y * (1.0 + sc_ref[...]) + sh_ref[...]).astype(BF16)
        h_s[...] = h
        probs, selv = _router_probs(h, wr_ref, rb_ref, lane)
        gate_s[...] = _route(probs, selv, lane)
        o_ref[...] = jnp.zeros_like(o_ref)

    h = h_s[...]
    gcol = jnp.sum(jnp.where(lane == e, gate_s[...], 0.0), axis=-1, keepdims=True)
    hg = _bdot(h, wg_ref[...])
    hu = _bdot(h, wu_ref[...])
    act = hg * jax.nn.sigmoid(hg) * hu * gcol
    o_ref[...] += _bdot(act, wd_ref[...])

    @pl.when(e == N_EXPERTS - 1)
    def _():
        o_ref[...] = x_ref[...] + gt_ref[...] * o_ref[...]


def _moe(x, g, grp, wr, rb, wg, wu, wd, l, tm):
    row = lambda i, e: (i, 0)
    return pl.pallas_call(
        _moe_kernel,
        out_shape=jax.ShapeDtypeStruct((grp.rows, D_MODEL), F32),
        grid=(grp.rows // tm, N_EXPERTS),
        in_specs=[pl.BlockSpec((tm, D_MODEL), row),
                  pl.BlockSpec((1, D_MODEL), lambda i, e: (0, 0)),
                  grp.mod_spec(4, tm),
                  grp.mod_spec(3, tm),
                  grp.mod_spec(5, tm),
                  pl.BlockSpec((D_MODEL, LANES), lambda i, e: (0, 0)),
                  pl.BlockSpec((1, LANES), lambda i, e: (0, 0)),
                  pl.BlockSpec((None, None, D_MODEL, D_EXPERT), lambda i, e: (l, e, 0, 0)),
                  pl.BlockSpec((None, None, D_MODEL, D_EXPERT), lambda i, e: (l, e, 0, 0)),
                  pl.BlockSpec((None, None, D_EXPERT, D_MODEL), lambda i, e: (l, e, 0, 0))],
        out_specs=pl.BlockSpec((tm, D_MODEL), row),
        scratch_shapes=[pltpu.VMEM((tm, D_MODEL), BF16),
                        pltpu.VMEM((tm, LANES), F32)],
        compiler_params=_cparams(("arbitrary", "arbitrary"), MOE_VMEM_LIMIT),
    )(x, g.reshape(1, D_MODEL), grp.mod4, grp.mod4, grp.mod4, wr, rb, wg, wu, wd)


MOE_SORT_TILE = 1024
MOE_DMA_ROWS = 2048


def _moe_router_kernel(x_ref, g_ref, sc_ref, sh_ref, wr_ref, rb_ref, tri_ref,
                       h_ref, gr_ref, cnt_ref, base_s):
    i = pl.program_id(0)
    tm = x_ref.shape[0]
    lane = lax.broadcasted_iota(jnp.int32, (tm, LANES), 1)

    @pl.when(i == 0)
    def _():
        base_s[...] = jnp.zeros_like(base_s)

    y = _rms(x_ref[...], g_ref[...])
    h = y * (1.0 + sc_ref[...]) + sh_ref[...]
    h_ref[...] = h
    probs, sel = _router_probs(h.astype(BF16), wr_ref, rb_ref, lane)
    best = _best_group(sel, _in_group_top2(sel, lane), lane)
    first = best & ((lane & (EXPERTS_PER_GROUP - 1)) == 0)
    gid = jnp.sum(jnp.where(first, lane >> 2, 0).astype(F32), axis=-1, keepdims=True)
    onehot = (lane.astype(F32) == gid).astype(BF16)
    incl = _dot(tri_ref[...], onehot) + base_s[...]
    rank = jnp.sum(jnp.where(lane.astype(F32) == gid, incl - 1.0, 0.0), axis=-1, keepdims=True)
    gr_ref[...] = jnp.where(lane == 0, gid, jnp.where(lane == 1, rank, 0.0)).astype(jnp.int32)
    base_s[...] = incl[tm - 1:tm, :]
    cnt_ref[...] = incl[tm - 1:tm, :].astype(jnp.int32)


def _scatter_rows_kernel(dest_ref, src_hbm, zeros_hbm, dst_hbm, sem):
    del zeros_hbm
    base = pl.program_id(0) * MOE_DMA_ROWS

    def issue(t, carry):
        pltpu.make_async_copy(src_hbm.at[pl.ds(base + t, 1)], dst_hbm.at[pl.ds(dest_ref[base + t], 1)],
                              sem).start()
        return carry

    def drain(t, carry):
        pltpu.make_async_copy(src_hbm.at[pl.ds(0, 1)], dst_hbm.at[pl.ds(0, 1)], sem).wait()
        return carry

    lax.fori_loop(0, MOE_DMA_ROWS, issue, 0)
    lax.fori_loop(0, MOE_DMA_ROWS, drain, 0)


def _moe_group_kernel(tg_ref, nv_ref, xs_ref, wr_ref, rb_ref, wg_ref, wu_ref, wd_ref, ys_ref, h_s, gate_s):
    i = pl.program_id(0)
    j = pl.program_id(1)
    tm = xs_ref.shape[0]
    lane = lax.broadcasted_iota(jnp.int32, (tm, LANES), 1)
    grp = tg_ref[i]

    @pl.when((i >= nv_ref[0]) & (j == 0))
    def _():
        ys_ref[...] = jnp.zeros_like(ys_ref)

    @pl.when(i < nv_ref[0])
    def _():
        @pl.when(j == 0)
        def _():
            h = xs_ref[...].astype(BF16)
            h_s[...] = h
            probs, sel = _router_probs(h, wr_ref, rb_ref, lane)
            pk = jnp.where(_in_group_top2(sel, lane) & ((lane >> 2) == grp), probs, 0.0)
            psum = jnp.sum(pk, axis=-1, keepdims=True)
            gate_s[...] = pk / jnp.where(psum > 0.0, psum, 1.0)
            ys_ref[...] = jnp.zeros_like(ys_ref)

        h = h_s[...]
        gcol = jnp.sum(jnp.where(lane == grp * EXPERTS_PER_GROUP + j, gate_s[...], 0.0), axis=-1, keepdims=True)
        hg = _bdot(h, wg_ref[...])
        hu = _bdot(h, wu_ref[...])
        act = hg * jax.nn.sigmoid(hg) * hu * gcol
        ys_ref[...] += _bdot(act, wd_ref[...])


def _gather_residual_kernel(dest_ref, x_ref, gt_ref, ys_hbm, o_ref, buf, sem):
    tm = x_ref.shape[0]
    base = pl.program_id(0) * tm

    def issue(t, carry):
        pltpu.make_async_copy(ys_hbm.at[pl.ds(dest_ref[base + t], 1)], buf.at[pl.ds(t, 1)], sem).start()
        return carry

    def drain(t, carry):
        pltpu.make_async_copy(ys_hbm.at[pl.ds(0, 1)], buf.at[pl.ds(0, 1)], sem).wait()
        return carry

    lax.fori_loop(0, tm, issue, 0)
    lax.fori_loop(0, tm, drain, 0)
    o_ref[...] = x_ref[...] + gt_ref[...] * buf[...]


def _moe_sorted(x, g, grp, wr, rb, wg, wu, wd, l, tm):
    rows = grp.rows
    te = MOE_SORT_TILE
    ntile = rows // te + N_GROUPS
    cap = ntile * te
    tri = jnp.asarray(np.tril(np.ones((tm, tm), np.float32)), BF16)
    row = lambda i: (i, 0)
    h, gr, cnt = pl.pallas_call(
        _moe_router_kernel,
        out_shape=(jax.ShapeDtypeStruct((rows, D_MODEL), F32),
                   jax.ShapeDtypeStruct((rows, LANES), jnp.int32),
                   jax.ShapeDtypeStruct((1, LANES), jnp.int32)),
        grid=(rows // tm,),
        in_specs=[pl.BlockSpec((tm, D_MODEL), row),
                  pl.BlockSpec((1, D_MODEL), lambda i: (0, 0)),
                  grp.mod_spec(4, tm),
                  grp.mod_spec(3, tm),
                  pl.BlockSpec((D_MODEL, LANES), lambda i: (0, 0)),
                  pl.BlockSpec((1, LANES), lambda i: (0, 0)),
                  pl.BlockSpec((tm, tm), lambda i: (0, 0))],
        out_specs=(pl.BlockSpec((tm, D_MODEL), row),
                   pl.BlockSpec((tm, LANES), row),
                   pl.BlockSpec((1, LANES), lambda i: (0, 0))),
        scratch_shapes=[pltpu.VMEM((1, LANES), F32)],
        compiler_params=_cparams(("arbitrary",)),
    )(x, g.reshape(1, D_MODEL), grp.mod4, grp.mod4, wr, rb, tri)

    counts = cnt[0, :N_GROUPS]
    padded = ((counts + te - 1) // te) * te
    ends = jnp.cumsum(padded)
    dest = (ends - padded)[gr[:, 0]] + gr[:, 1]
    tile_group = jnp.minimum(jnp.searchsorted(ends, jnp.arange(ntile, dtype=jnp.int32) * te, side="right"),
                             N_GROUPS - 1).astype(jnp.int32)
    n_valid = (ends[N_GROUPS - 1] // te).astype(jnp.int32).reshape(1)

    xs = pl.pallas_call(
        _scatter_rows_kernel,
        out_shape=jax.ShapeDtypeStruct((cap, D_MODEL), F32),
        grid_spec=pltpu.PrefetchScalarGridSpec(
            num_scalar_prefetch=1,
            grid=(rows // MOE_DMA_ROWS,),
            in_specs=[pl.BlockSpec(memory_space=pl.ANY), pl.BlockSpec(memory_space=pl.ANY)],
            out_specs=pl.BlockSpec(memory_space=pl.ANY),
            scratch_shapes=[pltpu.SemaphoreType.DMA(())]),
        input_output_aliases={2: 0},
        compiler_params=_cparams(("arbitrary",)),
    )(dest, h, jnp.zeros((cap, D_MODEL), F32))

    ys = pl.pallas_call(
        _moe_group_kernel,
        out_shape=jax.ShapeDtypeStruct((cap, D_MODEL), F32),
        grid_spec=pltpu.PrefetchScalarGridSpec(
            num_scalar_prefetch=2,
            grid=(ntile, EXPERTS_PER_GROUP),
            in_specs=[pl.BlockSpec((te, D_MODEL), lambda i, j, tg, nv: (i, 0)),
                      pl.BlockSpec((D_MODEL, LANES), lambda i, j, tg, nv: (0, 0)),
                      pl.BlockSpec((1, LANES), lambda i, j, tg, nv: (0, 0)),
                      pl.BlockSpec((None, None, D_MODEL, D_EXPERT),
                                   lambda i, j, tg, nv: (l, tg[i] * EXPERTS_PER_GROUP + j, 0, 0)),
                      pl.BlockSpec((None, None, D_MODEL, D_EXPERT),
                                   lambda i, j, tg, nv: (l, tg[i] * EXPERTS_PER_GROUP + j, 0, 0)),
                      pl.BlockSpec((None, None, D_EXPERT, D_MODEL),
                                   lambda i, j, tg, nv: (l, tg[i] * EXPERTS_PER_GROUP + j, 0, 0))],
            out_specs=pl.BlockSpec((te, D_MODEL), lambda i, j, tg, nv: (i, 0)),
            scratch_shapes=[pltpu.VMEM((te, D_MODEL), BF16), pltpu.VMEM((te, LANES), F32)]),
        compiler_params=_cparams(("arbitrary", "arbitrary"), MOE_VMEM_LIMIT),
    )(tile_group, n_valid, xs, wr, rb, wg, wu, wd)

    gt_spec = grp.mod_spec(5, tm)
    return pl.pallas_call(
        _gather_residual_kernel,
        out_shape=jax.ShapeDtypeStruct((rows, D_MODEL), F32),
        grid_spec=pltpu.PrefetchScalarGridSpec(
            num_scalar_prefetch=1,
            grid=(rows // tm,),
            in_specs=[pl.BlockSpec((tm, D_MODEL), lambda i, d: (i, 0)),
                      pl.BlockSpec(gt_spec.block_shape, lambda i, d: gt_spec.index_map(i)),
                      pl.BlockSpec(memory_space=pl.ANY)],
            out_specs=pl.BlockSpec((tm, D_MODEL), lambda i, d: (i, 0)),
            scratch_shapes=[pltpu.VMEM((tm, D_MODEL), F32), pltpu.SemaphoreType.DMA(())]),
        compiler_params=_cparams(("arbitrary",)),
    )(dest, x, grp.mod4, ys)


def _block_diag(w):
    eye = jnp.eye(C_BLOCKS, dtype=w.dtype)
    return jnp.einsum("hg,hij->higj", eye, w).reshape(D_BRANCH, D_BRANCH)


def _layer_params(l, p):
    row = lambda a: a[l].reshape(1, -1)
    ones = jnp.asarray(np.kron(np.eye(4), np.ones((A_HEAD, A_HEAD))), BF16)
    pad_lo = lambda w: jnp.pad(w, ((0, 64), (0, 0)))
    pad_hi = lambda w: jnp.pad(w, ((64, 0), (0, 0)))
    rw = dict(mu=row(p["a_mu"]), w0=row(p["a_w0"]), w2=pad_lo(p["a_w2"][l]).astype(BF16),
              a0=row(p["a_a0"]), a2=pad_hi(p["a_a2"][l]).astype(BF16), g2=p["a_g2"][l].astype(BF16),
              kk=row(p["a_kk"]), ka=row(p["a_ka"]), rk=row(p["a_rk"]), lng=row(p["a_ln_g"]),
              lnb=row(p["a_ln_b"]), ones=ones)
    lru = dict(cw=p["c_conv_w"][l], cb=row(p["c_conv_b"]),
               wri=jnp.concatenate([_block_diag(p["c_wr"][l]), _block_diag(p["c_wi"][l])], axis=1).astype(BF16),
               br=row(p["c_br"]), bi=row(p["c_bi"]), lam=row(p["c_lam"]))
    return dict(
        rw=rw, lru=lru,
        w_branch=p["w_branch"][l].astype(BF16),
        w_out=p["w_out"][l].astype(BF16),
        wg=p["moe_wg"], wu=p["moe_wu"], wd=p["moe_wd"],
        norm_mix=p["norm_mix"][l], norm_ffn=p["norm_ffn"][l])


def kernel(x_prompt, x_sample, c_prompt, c_sample, state_rwkv_shift, state_rwkv_wkv, state_ret, state_lru_h, state_lru_conv, norm_mix, norm_ffn, norm_final, ada_w, ada_b, w_in, a_mu, a_w0, a_w2, a_a0, a_a2, a_g2, a_kk, a_ka, a_rk, a_ln_g, a_ln_b, c_conv_w, c_conv_b, c_wr, c_br, c_wi, c_bi, c_lam, w_branch, w_out, w_router, router_bias, moe_wg, moe_wu, moe_wd):
    p = dict(norm_mix=norm_mix, norm_ffn=norm_ffn, w_in=w_in, a_mu=a_mu, a_w0=a_w0, a_w2=a_w2,
             a_a0=a_a0, a_a2=a_a2, a_g2=a_g2, a_kk=a_kk, a_ka=a_ka, a_rk=a_rk, a_ln_g=a_ln_g,
             a_ln_b=a_ln_b, c_conv_w=c_conv_w, c_conv_b=c_conv_b, c_wr=c_wr, c_br=c_br, c_wi=c_wi,
             c_bi=c_bi, c_lam=c_lam, w_branch=w_branch, w_out=w_out, moe_wg=moe_wg, moe_wu=moe_wu,
             moe_wd=moe_wd)
    bp, lp_, _ = x_prompt.shape
    bs, ls, _ = x_sample.shape
    layers = [_layer_params(l, p) for l in range(DEPTH)]
    w3 = _arrange_w_in(w_in)
    router = (jnp.pad(w_router, ((0, 0), (0, LANES - N_EXPERTS))).astype(BF16),
              jnp.pad(router_bias, (0, LANES - N_EXPERTS)).reshape(1, LANES))

    n_c = bp + bs
    pad_c = (-n_c) % 16
    c_all = jnp.pad(jnp.concatenate([c_prompt, c_sample], axis=0), ((0, pad_c), (0, 0)))
    mods = _ada(c_all, ada_w, ada_b)

    def run(x, batch, seq, mod, states, pos0, cfg):
        xs = x.reshape(batch * seq, D_MODEL)
        grps = []
        for l in range(DEPTH):
            if cfg["per_token"]:
                m = jnp.repeat(mod[l], seq, axis=0).reshape(batch * seq, 6, D_MODEL)
                m4 = m.transpose(1, 0, 2)[None]
            else:
                m4 = mod[l].reshape(batch, 6, 1, D_MODEL)
            grps.append(_Group(batch, seq, m4))
        return _trunk_layers(xs, grps, states, layers, w3, router, norm_final, pos0, cfg)

    zeros = lambda s: jnp.zeros((DEPTH, bp) + s.shape[2:], x_prompt.dtype)
    st_prompt = (zeros(state_rwkv_shift), zeros(state_rwkv_wkv), zeros(state_ret),
                 zeros(state_lru_h), zeros(state_lru_conv))
    st_sample = (state_rwkv_shift, state_rwkv_wkv, state_ret, state_lru_h, state_lru_conv)
    cfg_p = dict(per_token=False, tm=min(512, lp_), tm_mg=min(256, lp_), tm_mm=min(1024, lp_),
                 tm_moe=min(1024, lp_), moe_sorted=(bp * lp_) % MOE_DMA_ROWS == 0, rwkv_nb=8, lru_nb=1, lru_lb=min(256, lp_))
    cfg_s = dict(per_token=True, tm=min(512, bs * ls), tm_mg=min(256, bs * ls), tm_mm=min(512, bs * ls),
                 tm_moe=min(512, bs * ls), moe_sorted=False, rwkv_nb=16, lru_nb=16, lru_lb=ls)
    y_p, new_p = run(x_prompt, bp, lp_, mods[:, :bp], st_prompt, 0, cfg_p)
    y_s, new_s = run(x_sample, bs, ls, mods[:, bp:bp + bs], st_sample, PAST_LEN, cfg_s)
    return (y_p, y_s) + new_p + new_s


def _trunk_layers(x, grps, states, layers, w3, router, norm_final, pos0, cfg):
    batch, seq = grps[0].batch, grps[0].seq
    tabs = _ret_tables(seq, pos0)
    wr, rb = router
    outs = [[] for _ in range(5)]
    for l, lp in enumerate(layers):
        grp = grps[l]
        h = _norm_mod(x, lp["norm_mix"], grp, 1, 0, cfg["tm"])
        pa = _matmul(h, w3, l, W_PA, W_PA_PAD, cfg["tm_mm"], 1024)
        rq = _matmul(h, w3, l, W_RQ, 4 * D_BRANCH, cfg["tm_mm"], 1024)
        cx = _matmul(h, w3, l, W_CX, 2 * D_BRANCH, cfg["tm_mm"], 1024)
        if seq % RW_CHUNK == 0:
            o_a, wkv = _rwkv_chunked(pa, states[0][l], states[1][l], lp["rw"], batch, seq,
                                     min(4, seq // RW_CHUNK))
        else:
            o_a, wkv = _rwkv_steps(pa, states[0][l], states[1][l], lp["rw"], batch, seq, cfg["rwkv_nb"])
        o_b, ret = _retention(rq, states[2][l], tabs, batch, seq)
        o_c, lru_h = _rglru(cx, states[4][l], states[3][l], lp["lru"], batch, seq,
                            cfg["lru_nb"], cfg["lru_lb"])
        x = _merge_out(h, o_a, o_b, o_c, x, grp, w3, l, lp["w_branch"], lp["w_out"], cfg["tm_mg"])
        moe = _moe_sorted if cfg["moe_sorted"] else _moe
        x = moe(x, lp["norm_ffn"], grp, wr, rb, lp["wg"], lp["wu"], lp["wd"], l, cfg["tm_moe"])
        outs[0].append(pa.reshape(batch, seq, W_PA_PAD)[:, -1, :A_PROJ])
        outs[1].append(wkv)
        outs[2].append(ret)
        outs[3].append(lru_h)
        outs[4].append(cx.reshape(batch, seq, 2 * D_BRANCH)[:, seq - (CONV_W - 1):, :D_BRANCH])
    y = _final_norm(x, norm_final, cfg["tm"])
    return y.reshape(batch, seq, D_MODEL), tuple(jnp.stack(o) for o in outs)
```

```python
import functools
import math

import numpy as np
import jax
import jax.numpy as jnp
from jax import lax
from jax.experimental import pallas as pl
from jax.experimental.pallas import tpu as pltpu

F32 = jnp.float32
BF16 = jnp.bfloat16

D_MODEL = 1024
DEPTH = 2
PAST_LEN = 16384
D_BRANCH = 512
A_HEAD = 64
A_HEADS = 8
A_PROJ = 1792
A_NORM_EPS = 64e-5
A_KK_EPS = 1e-12
R_HEAD = 128
R_HEADS = 4
R_CHUNK = 128
R_NORM_EPS = 1e-6
ROPE_BASE = 10000.0
C_BLOCK = 64
C_BLOCKS = 8
CONV_W = 4
LRU_C = 8.0
N_EXPERTS = 16
N_GROUPS = 4
EXPERTS_PER_GROUP = 4
TOP_K = 2
D_EXPERT = 512
NORM_EPS = 1e-6

LANES = 128
RET_ROWS = 128
VMEM_LIMIT = 48 * 1024 * 1024
MOE_VMEM_LIMIT = 56 * 1024 * 1024


def _cparams(sem, vmem=VMEM_LIMIT):
    return pltpu.CompilerParams(dimension_semantics=sem, vmem_limit_bytes=vmem)


def _dot(a, b):
    return jnp.dot(a, b, preferred_element_type=F32)


def _bdot(a, b):
    return jnp.dot(a.astype(BF16), b.astype(BF16), preferred_element_type=F32)


def _split3(x):
    hi = x.astype(BF16)
    r1 = x - hi.astype(F32)
    mid = r1.astype(BF16)
    lo = (r1 - mid.astype(F32)).astype(BF16)
    return hi, mid, lo


def _segsum(x, ones_bf16):
    w = ones_bf16.shape[0]
    hi, mid, lo = _split3(x)
    parts = []
    for j in range(x.shape[1] // w):
        c = slice(j * w, (j + 1) * w)
        parts.append((_dot(lo[:, c], ones_bf16) + _dot(mid[:, c], ones_bf16)) + _dot(hi[:, c], ones_bf16))
    return jnp.concatenate(parts, axis=1)


def _ada_kernel(c_ref, w_ref, b_ref, o_ref):
    c = c_ref[...]
    s = c * jax.nn.sigmoid(c)
    o_ref[...] = _bdot(s, w_ref[...]) + b_ref[...]


def _ada(c, ada_w, ada_b):
    rows = c.shape[0]
    tn = 1536
    return pl.pallas_call(
        _ada_kernel,
        out_shape=jax.ShapeDtypeStruct((DEPTH, rows, 6 * D_MODEL), F32),
        grid=(DEPTH, 6 * D_MODEL // tn),
        in_specs=[
            pl.BlockSpec((rows, D_MODEL), lambda l, j: (0, 0)),
            pl.BlockSpec((None, D_MODEL, tn), lambda l, j: (l, 0, j)),
            pl.BlockSpec((None, 1, tn), lambda l, j: (l, 0, j)),
        ],
        out_specs=pl.BlockSpec((None, rows, tn), lambda l, j: (l, 0, j)),
        compiler_params=_cparams(("arbitrary", "arbitrary")),
    )(c, ada_w, ada_b.reshape(DEPTH, 1, 6 * D_MODEL))


class _Group:
    def __init__(self, batch, seq, mod4):
        self.batch = batch
        self.seq = seq
        self.rows = batch * seq
        self.mod4 = mod4
        self.per_token = mod4.shape[0] == 1 and mod4.shape[2] != 1

    def mod_spec(self, k, tm):
        if self.per_token:
            return pl.BlockSpec((None, None, tm, D_MODEL), lambda i, *_: (0, k, i, 0))
        seq = self.seq
        return pl.BlockSpec((None, None, 1, D_MODEL), lambda i, *_: ((i * tm) // seq, k, 0, 0))


def _rms(x, g):
    return x * lax.rsqrt(jnp.mean(x * x, axis=-1, keepdims=True) + NORM_EPS) * g


def _norm_mod_kernel(x_ref, g_ref, sc_ref, sh_ref, o_ref):
    y = _rms(x_ref[...], g_ref[...])
    o_ref[...] = (y * (1.0 + sc_ref[...]) + sh_ref[...]).astype(o_ref.dtype)


def _norm_kernel(x_ref, g_ref, o_ref):
    o_ref[...] = _rms(x_ref[...], g_ref[...]).astype(o_ref.dtype)


def _norm_mod(x, g, grp, k_sc, k_sh, tm):
    return pl.pallas_call(
        _norm_mod_kernel,
        out_shape=jax.ShapeDtypeStruct((grp.rows, D_MODEL), BF16),
        grid=(grp.rows // tm,),
        in_specs=[
            pl.BlockSpec((tm, D_MODEL), lambda i: (i, 0)),
            pl.BlockSpec((1, D_MODEL), lambda i: (0, 0)),
            grp.mod_spec(k_sc, tm),
            grp.mod_spec(k_sh, tm),
        ],
        out_specs=pl.BlockSpec((tm, D_MODEL), lambda i: (i, 0)),
        compiler_params=_cparams(("arbitrary",)),
    )(x, g.reshape(1, D_MODEL), grp.mod4, grp.mod4)


def _final_norm(x, g, tm):
    rows = x.shape[0]
    return pl.pallas_call(
        _norm_kernel,
        out_shape=jax.ShapeDtypeStruct((rows, D_MODEL), F32),
        grid=(rows // tm,),
        in_specs=[
            pl.BlockSpec((tm, D_MODEL), lambda i: (i, 0)),
            pl.BlockSpec((1, D_MODEL), lambda i: (0, 0)),
        ],
        out_specs=pl.BlockSpec((tm, D_MODEL), lambda i: (i, 0)),
        compiler_params=_cparams(("arbitrary",)),
    )(x, g.reshape(1, D_MODEL))


def _mm_kernel(a_ref, w_ref, o_ref):
    o_ref[...] = _dot(a_ref[...], w_ref[...])


def _matmul(a, w3, l, col0, n, tm, tn):
    rows, k = a.shape
    c0 = col0 // tn
    return pl.pallas_call(
        _mm_kernel,
        out_shape=jax.ShapeDtypeStruct((rows, n), F32),
        grid=(n // tn, rows // tm),
        in_specs=[
            pl.BlockSpec((tm, k), lambda j, i: (i, 0)),
            pl.BlockSpec((None, k, tn), lambda j, i: (l, 0, c0 + j)),
        ],
        out_specs=pl.BlockSpec((tm, tn), lambda j, i: (i, j)),
        compiler_params=_cparams(("arbitrary", "arbitrary")),
    )(a, w3)


W_RQ, W_CX, W_GM, W_PA, W_PA_PAD = 0, 2048, 3072, 6144, 2048


W_BLK = 256


def _arrange_kernel(src_ref, w_ref, o_ref):
    del src_ref
    j = pl.program_id(1)
    real = (W_PA + A_PROJ) // W_BLK

    @pl.when(j < real)
    def _():
        o_ref[...] = w_ref[...].astype(BF16)

    @pl.when(j >= real)
    def _():
        o_ref[...] = jnp.zeros_like(o_ref)


def _arrange_w_in(w_in):
    depth, k, n = w_in.shape
    nblk = n // W_BLK
    first = A_PROJ // W_BLK
    src = list(range(first, nblk)) + list(range(first))
    nout = (W_PA + W_PA_PAD) // W_BLK
    src = jnp.asarray(src + [0] * (nout - len(src)), jnp.int32)
    return pl.pallas_call(
        _arrange_kernel,
        out_shape=jax.ShapeDtypeStruct((depth, k, nout * W_BLK), BF16),
        grid_spec=pltpu.PrefetchScalarGridSpec(
            num_scalar_prefetch=1,
            grid=(depth, nout),
            in_specs=[pl.BlockSpec((None, k, W_BLK), lambda l, j, s: (l, 0, s[j]))],
            out_specs=pl.BlockSpec((None, k, W_BLK), lambda l, j, s: (l, 0, j))),
        compiler_params=_cparams(("arbitrary", "arbitrary")),
    )(src, w_in)


def _rwkv_pre(pa, prev, mu_ref, w0_ref, w2_ref, a0_ref, a2_ref, g2_ref, kk_ref, ka_ref, ones):
    pm = pa + (prev - pa) * mu_ref[...]
    r = pm[:, 0:512]
    k = pm[:, 512:1024]
    v = pm[:, 1024:1536]
    xwa = pm[:, 1536:1664]
    xg = pm[:, 1664:1792]
    w_log = -jax.nn.softplus(-(w0_ref[...] + _bdot(jnp.tanh(xwa), w2_ref[...]))) - 0.5
    logw = -jnp.exp(w_log)
    a = jax.nn.sigmoid(a0_ref[...] + _bdot(xwa, a2_ref[...]))
    g = _bdot(jax.nn.sigmoid(xg), g2_ref[...])
    kk = k * kk_ref[...]
    kk = kk * lax.rsqrt(_segsum(kk * kk, ones) + A_KK_EPS)
    k2 = k * (1.0 + (a - 1.0) * ka_ref[...])
    return r, logw, k2, v, -kk, kk * a, g


def _rwkv_post(o, r, k2, v, g, rk_ref, lng_ref, lnb_ref, ones):
    mean = _segsum(o, ones) * (1.0 / A_HEAD)
    oc = o - mean
    var = _segsum(oc * oc, ones) * (1.0 / A_HEAD)
    o = oc * lax.rsqrt(var + A_NORM_EPS) * lng_ref[...] + lnb_ref[...]
    bonus = _segsum(r * k2 * rk_ref[...], ones) * v
    return (o + bonus) * g


_NN = (((1,), (0,)), ((), ()))
_NT = (((1,), (1,)), ((), ()))
_TN = (((0,), (0,)), ((), ()))
RW_CHUNK = 64
RW_PAIRS = A_HEADS // 2


def _bdg(a, b, dims):
    return lax.dot_general(a.astype(BF16), b.astype(BF16), dims, preferred_element_type=F32)


def _rwkv_chunk_kernel(nck,
                       pa_ref, sh0_ref, g0_ref, mu_ref, w0_ref, w2_ref, a0_ref, a2_ref, g2_ref,
                       kk_ref, ka_ref, rk_ref, lng_ref, lnb_ref, ones_ref, tri_ref,
                       o_ref, gout_ref,
                       carry_ref, prev_ref, g_s, wu_s, m_s, zy_s, o_s):
    c = pl.program_id(1)
    ck = RW_CHUNK
    ones = ones_ref[...]

    @pl.when(c == 0)
    def _():
        carry_ref[...] = sh0_ref[...]
        g_s[...] = g0_ref[...]

    pa = pa_ref[...]
    rows = pa.shape[0]
    prev_ref[...] = pltpu.roll(pa, 1, axis=0)
    prev_ref[pl.ds(0, 1), :] = carry_ref[...]
    carry_ref[...] = pa[rows - 1:rows, :]
    r, logw, k2, v, an, bn, g = _rwkv_pre(pa, prev_ref[...], mu_ref, w0_ref, w2_ref, a0_ref, a2_ref,
                                          g2_ref, kk_ref, ka_ref, ones)

    tri = tri_ref[...]
    hi, mid, lo = _split3(logw)
    cum =(_dot(tri, lo) + _dot(tri, mid)) + _dot(tri, hi)
    cum_last = jnp.concatenate(
        [jnp.broadcast_to(cum[(i + 1) * ck - 1:(i + 1) * ck, :], (ck, D_BRANCH)) for i in range(nck)], axis=0)
    gam = jnp.exp(cum)
    inv = jnp.exp(-cum)
    to_end = jnp.exp(cum_last - cum)
    a_t = an * jnp.exp(cum - logw)
    b_t = bn * inv
    k_t = k2 * inv
    r_t = r * gam
    b_e = bn * to_end
    k_e = k2 * to_end

    lane = lax.broadcasted_iota(jnp.int32, (ck, LANES), 1)
    rowi = lax.broadcasted_iota(jnp.int32, (ck, LANES), 0)
    m0 = lane < A_HEAD
    coli = lane & (A_HEAD - 1)
    strict = rowi > coli
    incl = rowi >= coli
    eye_p = (rowi == coli).astype(F32)
    r128 = lax.broadcasted_iota(jnp.int32, (LANES, LANES), 0)
    c128 = lax.broadcasted_iota(jnp.int32, (LANES, LANES), 1)
    blockmask = (r128 < A_HEAD) == (c128 < A_HEAD)
    eye128 = r128 == c128
    bk_t = jnp.concatenate([b_e, k_e], axis=1).T

    def bd(q):
        q = q.astype(BF16)
        z = jnp.zeros_like(q)
        return jnp.concatenate([jnp.where(m0, q, z), jnp.where(m0, z, q)], axis=0)

    probs = [(i, p) for i in range(nck) for p in range(RW_PAIRS)]
    sl = {(i, p): (slice(i * ck, (i + 1) * ck), slice(p * LANES, (p + 1) * LANES)) for i, p in probs}
    l_pow, l_ak, t_inv = {}, {}, {}
    for q in probs:
        rs, ls = sl[q]
        lhs = jnp.concatenate([a_t[rs, ls], r_t[rs, ls]], axis=0).astype(BF16)
        ab = lax.dot_general(lhs, bd(b_t[rs, ls]), _NT, preferred_element_type=F32)
        ak = lax.dot_general(lhs, bd(k_t[rs, ls]), _NT, preferred_element_type=F32)
        l_pow[q] = jnp.where(strict, ab[:ck], 0.0)
        l_ak[q] = jnp.where(strict, ak[:ck], 0.0)
        m_s[q[0], q[1], :, 0:LANES] = jnp.where(incl, ab[ck:], 0.0)
        m_s[q[0], q[1], :, LANES:2 * LANES] = jnp.where(incl, ak[ck:], 0.0)
        t_inv[q] = eye_p + l_pow[q]
    n = 1
    while 2 * n < ck:
        for q in probs:
            l_pow[q] = _bdg(l_pow[q], bd(l_pow[q]), _NN)
        for q in probs:
            t_inv[q] = t_inv[q] + _bdg(t_inv[q], bd(l_pow[q]), _NN)
        n *= 2
    lak_v = {}
    for q in probs:
        rs, ls = sl[q]
        lak_v[q] = _bdg(l_ak[q], bd(v[rs, ls]), _NN)
    for q in probs:
        rs, ls = sl[q]
        wu = _bdg(t_inv[q], jnp.concatenate([bd(a_t[rs, ls]), bd(lak_v[q])], axis=1), _NN)
        wu_s[q[0], q[1]] = wu
        i, p = q
        half = jnp.zeros((ck, LANES), F32)
        place = (lambda x: jnp.concatenate([x, half], axis=0)) if i % 2 == 0 else \
                (lambda x: jnp.concatenate([half, x], axis=0))
        tcols = slice((i // 2) * LANES, (i // 2 + 1) * LANES)
        b_tr = bk_t[p * LANES:(p + 1) * LANES, tcols]
        k_tr = bk_t[D_BRANCH + p * LANES:D_BRANCH + (p + 1) * LANES, tcols]
        z_t = _bdg(b_tr, place(wu[:, 0:LANES]), _NN)
        y_t = _bdg(jnp.concatenate([b_tr, k_tr], axis=1),
                   jnp.concatenate([place(wu[:, LANES:2 * LANES]), place(v[rs, ls])], axis=0), _NN)
        zy_s[i, p, :, 0:LANES] = jnp.where(blockmask, z_t, 0.0)
        zy_s[i, p, :, LANES:2 * LANES] = jnp.where(blockmask, y_t, 0.0)

    for i in range(nck):
        rs = slice(i * ck, (i + 1) * ck)
        for p in range(RW_PAIRS):
            ls = slice(p * LANES, (p + 1) * LANES)
            h_p = g_s[p]
            h_bf = h_p.astype(BF16)
            gcol = jnp.sum(jnp.where(eye128, gam[(i + 1) * ck - 1:(i + 1) * ck, ls], 0.0), axis=1, keepdims=True)
            g_s[p] = (h_p * gcol + _dot(zy_s[i, p, :, 0:LANES].astype(BF16), h_bf)) + zy_s[i, p, :, LANES:2 * LANES]
            wu = wu_s[i, p]
            u = _dot(wu[:, 0:LANES].astype(BF16), h_bf) + wu[:, LANES:2 * LANES]
            o_s[rs, ls] = _dot(r_t[rs, ls].astype(BF16), h_bf) \
                + _bdg(m_s[i, p], jnp.concatenate([bd(u), bd(v[rs, ls])], axis=0), _NN)

    o_ref[...] = _rwkv_post(o_s[...], r, k2, v, g, rk_ref, lng_ref, lnb_ref, ones).astype(o_ref.dtype)
    gout_ref[...] = g_s[...]


def _rwkv_chunked(pa, shift0, s0, prm, batch, seq, nck):
    lb = nck * RW_CHUNK
    nstep = seq // lb
    s0p = s0.reshape(batch, RW_PAIRS, 2, A_HEAD, A_HEAD)
    eye2 = jnp.eye(2, dtype=s0.dtype)
    g0 = jnp.einsum("bpjvk,ji->bpjkiv", s0p, eye2).reshape(batch, RW_PAIRS, LANES, LANES)
    tri = np.kron(np.eye(nck), np.tril(np.ones((RW_CHUNK, RW_CHUNK)))).astype(np.float32)
    params = [prm[n] for n in ("mu", "w0", "w2", "a0", "a2", "g2", "kk", "ka", "rk", "lng", "lnb", "ones")]
    params.append(jnp.asarray(tri, BF16))

    def full(arr):
        nd = arr.ndim
        return pl.BlockSpec(arr.shape, lambda i, c: (0,) * nd)

    o, g_out = pl.pallas_call(
        functools.partial(_rwkv_chunk_kernel, nck),
        out_shape=(jax.ShapeDtypeStruct((batch * seq, D_BRANCH), BF16),
                   jax.ShapeDtypeStruct((batch, RW_PAIRS, LANES, LANES), F32)),
        grid=(batch, nstep),
        in_specs=[pl.BlockSpec((lb, A_PROJ), lambda i, c: (i * nstep + c, 0)),
                  pl.BlockSpec((None, 1, A_PROJ), lambda i, c: (i, 0, 0)),
                  pl.BlockSpec((None, RW_PAIRS, LANES, LANES), lambda i, c: (i, 0, 0, 0))]
                 + [full(p) for p in params],
        out_specs=(pl.BlockSpec((lb, D_BRANCH), lambda i, c: (i * nstep + c, 0)),
                   pl.BlockSpec((None, RW_PAIRS, LANES, LANES), lambda i, c: (i, 0, 0, 0))),
        scratch_shapes=[pltpu.VMEM((1, A_PROJ), F32),
                        pltpu.VMEM((lb, A_PROJ), F32),
                        pltpu.VMEM((RW_PAIRS, LANES, LANES), F32),
                        pltpu.VMEM((nck, RW_PAIRS, RW_CHUNK, 2 * LANES), F32),
                        pltpu.VMEM((nck, RW_PAIRS, RW_CHUNK, 2 * LANES), F32),
                        pltpu.VMEM((nck, RW_PAIRS, LANES, 2 * LANES), F32),
                        pltpu.VMEM((lb, D_BRANCH), F32)],
        compiler_params=_cparams(("arbitrary", "arbitrary")),
    )(pa, shift0.reshape(batch, 1, A_PROJ), g0, *params)
    g5 = g_out.reshape(batch, RW_PAIRS, 2, A_HEAD, 2, A_HEAD)
    s_new = jnp.stack([g5[:, :, 0, :, 0, :], g5[:, :, 1, :, 1, :]], axis=2).swapaxes(-1, -2)
    return o, s_new.reshape(batch, A_HEADS, A_HEAD, A_HEAD)


def _rwkv_step_kernel(nb, lb,
                      pa_ref, sh0_ref, s0_ref, mu_ref, w0_ref, w2_ref, a0_ref, a2_ref, g2_ref,
                      kk_ref, ka_ref, rk_ref, lng_ref, lnb_ref, ones_ref,
                      o_ref, sout_ref):
    ones = ones_ref[...]
    pa = pa_ref[...].reshape(lb * nb, A_PROJ)
    prev = jnp.concatenate([sh0_ref[...], pa[:(lb - 1) * nb, :]], axis=0)
    r, logw, k2, v, an, bn, g = _rwkv_pre(pa, prev, mu_ref, w0_ref, w2_ref, a0_ref, a2_ref,
                                          g2_ref, kk_ref, ka_ref, ones)
    w = jnp.exp(logw)
    srows = nb * A_HEAD
    rowi = lax.broadcasted_iota(jnp.int32, (srows, D_BRANCH), 0)
    lane = lax.broadcasted_iota(jnp.int32, (srows, D_BRANCH), 1)
    eye = (rowi & (A_HEAD - 1)) == (lane & (A_HEAD - 1))

    def per_seq(x, t):
        xt = x[t * nb:(t + 1) * nb, :]
        return jnp.concatenate([jnp.broadcast_to(xt[b:b + 1, :], (A_HEAD, D_BRANCH)) for b in range(nb)],
                               axis=0)

    s = s0_ref[...].reshape(srows, D_BRANCH)
    outs = []
    for t in range(lb):
        sa = _segsum(s * per_seq(an, t), ones)
        vcol = _segsum(jnp.where(eye, per_seq(v, t), 0.0), ones)
        s = s * per_seq(w, t) + sa * per_seq(bn, t) + vcol * per_seq(k2, t)
        out = _segsum(s * per_seq(r, t), ones)
        outs.append(jnp.sum(jnp.where(eye, out, 0.0).reshape(nb, A_HEAD, D_BRANCH), axis=1))
    o = jnp.concatenate(outs, axis=0)
    res = _rwkv_post(o, r, k2, v, g, rk_ref, lng_ref, lnb_ref, ones).astype(o_ref.dtype)
    o_ref[...] = res.reshape(lb, nb, D_BRANCH)
    sout_ref[...] = s.reshape(nb, A_HEAD, D_BRANCH)


def _rwkv_steps(pa, shift0, s0, prm, batch, seq, nb):
    pa_tm = pa.reshape(batch, seq, pa.shape[1]).transpose(1, 0, 2)
    s0k = s0.transpose(0, 2, 1, 3).reshape(batch, A_HEAD, D_BRANCH)

    def full(arr):
        nd = arr.ndim
        return pl.BlockSpec(arr.shape, lambda i: (0,) * nd)

    params = [prm[n] for n in ("mu", "w0", "w2", "a0", "a2", "g2", "kk", "ka", "rk", "lng", "lnb", "ones")]
    o, s_out = pl.pallas_call(
        functools.partial(_rwkv_step_kernel, nb, seq),
        out_shape=(jax.ShapeDtypeStruct((seq, batch, D_BRANCH), BF16),
                   jax.ShapeDtypeStruct((batch, A_HEAD, D_BRANCH), F32)),
        grid=(batch // nb,),
        in_specs=[pl.BlockSpec((seq, nb, A_PROJ), lambda i: (0, i, 0)),
                  pl.BlockSpec((nb, A_PROJ), lambda i: (i, 0)),
                  pl.BlockSpec((nb, A_HEAD, D_BRANCH), lambda i: (i, 0, 0))]
                 + [full(p) for p in params],
        out_specs=(pl.BlockSpec((seq, nb, D_BRANCH), lambda i: (0, i, 0)),
                   pl.BlockSpec((nb, A_HEAD, D_BRANCH), lambda i: (i, 0, 0))),
        compiler_params=_cparams(("arbitrary",)),
    )(pa_tm, shift0, s0k, *params)
    o = o.transpose(1, 0, 2).reshape(batch * seq, D_BRANCH)
    s_new = s_out.reshape(batch, A_HEAD, A_HEADS, A_HEAD).transpose(0, 2, 1, 3)
    return o, s_new


def _ret_kernel(nb, lb,
                rq_ref, cos_ref, sin_ref, dm_ref, kd_ref, qd_ref, cd_ref, s0_ref,
                o_ref, sout_ref, s_ref):
    c = pl.program_id(1)

    @pl.when(c == 0)
    def _():
        s_ref[...] = s0_ref[...]

    cos = cos_ref[...]
    sin = sin_ref[...]
    scale = R_HEAD ** -0.5
    half = R_HEAD // 2
    row8 = lax.broadcasted_iota(jnp.int32, (8, R_HEAD), 0)
    for h in range(R_HEADS):
        lo, hi = h * R_HEAD, (h + 1) * R_HEAD
        q = rq_ref[:, lo:hi]
        k = rq_ref[:, 512 + lo:512 + hi]
        v = rq_ref[:, 1024 + lo:1024 + hi]
        gate = rq_ref[:, 1536 + lo:1536 + hi]
        qh = q * cos + pltpu.roll(q, half, axis=1) * sin
        kh = (k * cos + pltpu.roll(k, half, axis=1) * sin) * scale
        scores = lax.dot_general(qh.astype(BF16), kh.astype(BF16), (((1,), (1,)), ((), ())),
                                 preferred_element_type=F32) * dm_ref[h]
        o = _bdot(scores, v)
        qd = qh * qd_ref[:, lo:hi]
        ku = kh * kd_ref[:, lo:hi]
        cd = cd_ref[h]
        if nb == 1:
            s = s_ref[0, h]
            o = o + _bdot(qd, s)
            upd = lax.dot_general(ku.astype(BF16), v.astype(BF16), (((0,), (0,)), ((), ())),
                                  preferred_element_type=F32)
            s_ref[0, h] = s * cd + upd
        else:
            per_tile = 8 // lb
            inter = []
            for i in range(RET_ROWS // 8):
                qd_t = qd[i * 8:(i + 1) * 8, :]
                ku_t = ku[i * 8:(i + 1) * 8, :]
                v_t = v[i * 8:(i + 1) * 8, :].astype(BF16)
                acc = None
                for j in range(per_tile):
                    b = i * per_tile + j
                    m = (row8 >= j * lb) & (row8 < (j + 1) * lb)
                    s = s_ref[b, h]
                    part = _bdot(jnp.where(m, qd_t, 0.0), s)
                    acc = part if acc is None else acc + part
                    upd = lax.dot_general(jnp.where(m, ku_t, 0.0).astype(BF16), v_t,
                                          (((0,), (0,)), ((), ())), preferred_element_type=F32)
                    s_ref[b, h] = s * cd + upd
                inter.append(acc)
            o = o + jnp.concatenate(inter, axis=0)
        oc = o - jnp.mean(o, axis=-1, keepdims=True)
        on = oc * lax.rsqrt(jnp.mean(oc * oc, axis=-1, keepdims=True) + R_NORM_EPS)
        o_ref[:, lo:hi] = (gate * jax.nn.sigmoid(gate) * on).astype(o_ref.dtype)
    sout_ref[...] = s_ref[...]


def _retention(rq, s0, tabs, batch, seq):
    lb = min(R_CHUNK, seq)
    nb = RET_ROWS // lb
    nchunk = seq // lb
    cos, sin, dm, kd, qd, cd = tabs
    ntab = cos.shape[0] // RET_ROWS

    def const(arr):
        nd = arr.ndim
        return pl.BlockSpec(arr.shape, lambda i, c: (0,) * nd)

    tab_idx = (lambda i, c: (c, 0)) if ntab > 1 else (lambda i, c: (0, 0))
    return pl.pallas_call(
        functools.partial(_ret_kernel, nb, lb),
        out_shape=(jax.ShapeDtypeStruct((batch * seq, D_BRANCH), BF16),
                   jax.ShapeDtypeStruct((batch, R_HEADS, R_HEAD, R_HEAD), F32)),
        grid=(batch // nb, nchunk),
        in_specs=[pl.BlockSpec((RET_ROWS, 4 * D_BRANCH), lambda i, c: (i * nchunk + c, 0)),
                  pl.BlockSpec((RET_ROWS, R_HEAD), tab_idx),
                  pl.BlockSpec((RET_ROWS, R_HEAD), tab_idx),
                  const(dm), const(kd), const(qd), const(cd),
                  pl.BlockSpec((nb, R_HEADS, R_HEAD, R_HEAD), lambda i, c: (i, 0, 0, 0))],
        out_specs=(pl.BlockSpec((RET_ROWS, D_BRANCH), lambda i, c: (i * nchunk + c, 0)),
                   pl.BlockSpec((nb, R_HEADS, R_HEAD, R_HEAD), lambda i, c: (i, 0, 0, 0))),
        scratch_shapes=[pltpu.VMEM((nb, R_HEADS, R_HEAD, R_HEAD), F32)],
        compiler_params=_cparams(("arbitrary", "arbitrary")),
    )(rq, cos, sin, dm, kd, qd, cd, s0)


def _ret_tables(seq, pos0):
    lb = min(R_CHUNK, seq)
    nb = RET_ROWS // lb
    half = R_HEAD // 2
    pos = pos0 + jnp.arange(seq, dtype=jnp.int32)
    inv = ROPE_BASE ** (-jnp.arange(half, dtype=F32) / half)
    ang = pos.astype(F32)[:, None] * inv[None, :]
    cos, sin = jnp.cos(ang), jnp.sin(ang)
    cos2 = jnp.concatenate([cos, cos], axis=-1)
    sin2 = jnp.concatenate([-sin, sin], axis=-1)
    if nb > 1:
        cos2 = jnp.tile(cos2, (nb, 1))
        sin2 = jnp.tile(sin2, (nb, 1))
    log_g = jnp.log1p(-jnp.exp2(-5.0 - jnp.arange(R_HEADS, dtype=F32)))
    idx = jnp.arange(lb, dtype=F32)
    diff = idx[:, None] - idx[None, :]
    dmask = jnp.where(diff >= 0, jnp.exp(log_g[:, None, None] * jnp.maximum(diff, 0.0)), 0.0)
    if nb > 1:
        dmask = jnp.einsum("ab,hij->haibj", jnp.eye(nb, dtype=F32), dmask).reshape(
            R_HEADS, RET_ROWS, RET_ROWS)
    k_decay = jnp.exp(log_g[None, :] * (lb - 1.0 - idx)[:, None])
    q_decay = jnp.exp(log_g[None, :] * (idx + 1.0)[:, None])
    kd = jnp.tile(jnp.repeat(k_decay, R_HEAD, axis=1), (nb, 1))
    qd = jnp.tile(jnp.repeat(q_decay, R_HEAD, axis=1), (nb, 1))
    cd = jnp.broadcast_to(jnp.exp(log_g * lb)[:, None, None], (R_HEADS, 1, R_HEAD))
    return cos2, sin2, dmask, kd, qd, cd


def _lru_kernel(nb, lb, three_d,
                cx_ref, conv0_ref, h0_ref, cw_ref, cb_ref, wri_ref, br_ref, bi_ref, lam_ref,
                o_ref, hout_ref,
                cc_ref, hc_ref, x1_ref, x2_ref, x3_ref, a_ref, b_ref):
    c = pl.program_id(1)
    rows = nb * lb

    @pl.when(c == 0)
    def _():
        cc_ref[...] = conv0_ref[...]
        hc_ref[...] = h0_ref[...]

    cx = cx_ref[...]
    if three_d:
        cx = cx.reshape(rows, 2 * D_BRANCH)
    xb = cx[:, :D_BRANCH]
    gb = cx[:, D_BRANCH:]
    x1_ref[...] = pltpu.roll(xb, 1, axis=0)
    x2_ref[...] = pltpu.roll(xb, 2, axis=0)
    x3_ref[...] = pltpu.roll(xb, 3, axis=0)
    for b in range(nb):
        r0 = b * lb
        c0 = cc_ref[b, 0:1, :]
        c1 = cc_ref[b, 1:2, :]
        c2 = cc_ref[b, 2:3, :]
        x1_ref[pl.ds(r0, 1), :] = c2
        x2_ref[pl.ds(r0, 1), :] = c1
        x2_ref[pl.ds(r0 + 1, 1), :] = c2
        x3_ref[pl.ds(r0, 1), :] = c0
        x3_ref[pl.ds(r0 + 1, 1), :] = c1
        x3_ref[pl.ds(r0 + 2, 1), :] = c2
        cc_ref[b] = xb[r0 + lb - 3:r0 + lb, :]
    cw = cw_ref[...]
    xc = cb_ref[...] + (((x3_ref[...] * cw[0:1, :] + x2_ref[...] * cw[1:2, :]) + x1_ref[...] * cw[2:3, :])
                        + xb * cw[3:4, :])
    ri = _bdot(xc, wri_ref[...])
    r = jax.nn.sigmoid(ri[:, :D_BRANCH] + br_ref[...])
    i = jax.nn.sigmoid(ri[:, D_BRANCH:] + bi_ref[...])
    log_a = LRU_C * r * jax.nn.log_sigmoid(lam_ref[...])
    a = jnp.exp(log_a)
    bb = jnp.sqrt(-jnp.tanh(log_a) * (a * a + 1.0)) * (i * xc)
    a_ref[...] = a
    b_ref[...] = bb
    for b in range(nb):
        r0 = b * lb
        b_ref[pl.ds(r0, 1), :] = bb[r0:r0 + 1, :] + a[r0:r0 + 1, :] * hc_ref[b]
    a = a_ref[...]
    bb = b_ref[...]
    t_idx = lax.broadcasted_iota(jnp.int32, (rows, D_BRANCH), 0) % lb
    s = 1
    while s < lb:
        keep = t_idx >= s
        a_sh = jnp.where(keep, pltpu.roll(a, s, axis=0), 1.0)
        b_sh = jnp.where(keep, pltpu.roll(bb, s, axis=0), 0.0)
        bb = a * b_sh + bb
        a = a * a_sh
        s *= 2
    h = bb
    for b in range(nb):
        r0 = b * lb
        hc_ref[b] = h[r0 + lb - 1:r0 + lb, :]
    res = (h * jax.nn.gelu(gb)).astype(o_ref.dtype)
    if three_d:
        res = res.reshape(nb, lb, D_BRANCH)
    o_ref[...] = res
    hout_ref[...] = hc_ref[...]


def _rglru(cx, conv0, h0, prm, batch, seq, nb, lb):
    nchunk = seq // lb
    rows = nb * lb
    three_d = nb > 1 and nchunk > 1
    if three_d:
        cx_in = cx.reshape(batch, seq, 2 * D_BRANCH)
        cx_spec = pl.BlockSpec((nb, lb, 2 * D_BRANCH), lambda i, c: (i, c, 0))
        o_shape = jax.ShapeDtypeStruct((batch, seq, D_BRANCH), BF16)
        o_spec = pl.BlockSpec((nb, lb, D_BRANCH), lambda i, c: (i, c, 0))
    else:
        cx_in = cx
        cx_spec = pl.BlockSpec((rows, 2 * D_BRANCH), lambda i, c: (i * nchunk + c, 0))
        o_shape = jax.ShapeDtypeStruct((batch * seq, D_BRANCH), BF16)
        o_spec = pl.BlockSpec((rows, D_BRANCH), lambda i, c: (i * nchunk + c, 0))

    def full(arr):
        nd = arr.ndim
        return pl.BlockSpec(arr.shape, lambda i, c: (0,) * nd)

    params = [prm[n] for n in ("cw", "cb", "wri", "br", "bi", "lam")]
    o, h_out = pl.pallas_call(
        functools.partial(_lru_kernel, nb, lb, three_d),
        out_shape=(o_shape, jax.ShapeDtypeStruct((batch, 1, D_BRANCH), F32)),
        grid=(batch // nb, nchunk),
        in_specs=[cx_spec,
                  pl.BlockSpec((nb, CONV_W - 1, D_BRANCH), lambda i, c: (i, 0, 0)),
                  pl.BlockSpec((nb, 1, D_BRANCH), lambda i, c: (i, 0, 0))]
                 + [full(p) for p in params],
        out_specs=(o_spec, pl.BlockSpec((nb, 1, D_BRANCH), lambda i, c: (i, 0, 0))),
        scratch_shapes=[pltpu.VMEM((nb, CONV_W - 1, D_BRANCH), F32),
                        pltpu.VMEM((nb, 1, D_BRANCH), F32)]
                       + [pltpu.VMEM((rows, D_BRANCH), F32) for _ in range(5)],
        compiler_params=_cparams(("arbitrary", "arbitrary")),
    )(cx_in, conv0, h0.reshape(batch, 1, D_BRANCH), *params)
    return o.reshape(batch * seq, D_BRANCH), h_out.reshape(batch, D_BRANCH)


def _merge_kernel(h_ref, oa_ref, ob_ref, oc_ref, x_ref, gt_ref, wgm_ref, wb_ref, wout_ref, o_ref):
    gm = _dot(h_ref[...], wgm_ref[...])
    merged = None
    for n, br_ref in enumerate((oa_ref, ob_ref, oc_ref)):
        br = _dot(br_ref[...], wb_ref[n])
        term = jax.nn.sigmoid(gm[:, n * D_MODEL:(n + 1) * D_MODEL]) * br
        merged = term if merged is None else merged + term
    y = _bdot(merged, wout_ref[...])
    o_ref[...] = x_ref[...] + gt_ref[...] * y


def _merge_out(h, oa, ob, oc, x, grp, w3, l, wb, wout, tm):
    row = lambda i: (i, 0)
    return pl.pallas_call(
        _merge_kernel,
        out_shape=jax.ShapeDtypeStruct((grp.rows, D_MODEL), F32),
        grid=(grp.rows // tm,),
        in_specs=[pl.BlockSpec((tm, D_MODEL), row),
                  pl.BlockSpec((tm, D_BRANCH), row),
                  pl.BlockSpec((tm, D_BRANCH), row),
                  pl.BlockSpec((tm, D_BRANCH), row),
                  pl.BlockSpec((tm, D_MODEL), row),
                  grp.mod_spec(2, tm),
                  pl.BlockSpec((None, D_MODEL, 3 * D_MODEL), lambda i: (l, 0, W_GM // (3 * D_MODEL))),
                  pl.BlockSpec(wb.shape, lambda i: (0, 0, 0)),
                  pl.BlockSpec(wout.shape, lambda i: (0, 0))],
        out_specs=pl.BlockSpec((tm, D_MODEL), row),
        compiler_params=_cparams(("arbitrary",)),
    )(h, oa, ob, oc, x, grp.mod4, w3, wb, wout)


def _group_member(x, k, lane):
    ge = EXPERTS_PER_GROUP
    pos = lane & (ge - 1)
    return jnp.where(pos + k < ge, pltpu.roll(x, LANES - k, axis=1), pltpu.roll(x, ge - k, axis=1))


def _in_group_top2(sel, lane):
    ge = EXPERTS_PER_GROUP
    pos = lane & (ge - 1)
    n_ahead = jnp.zeros(sel.shape, jnp.int32)
    for k in range(1, ge):
        other = _group_member(sel, k, lane)
        lower_index = pos + k >= ge
        n_ahead = n_ahead + ((other > sel) | ((other == sel) & lower_index)).astype(jnp.int32)
    return n_ahead < TOP_K


def _best_group(sel, in_top2, lane):
    ge = EXPERTS_PER_GROUP
    kept = jnp.where(in_top2, sel, 0.0)
    score = kept
    for k in range(1, ge):
        score = score + _group_member(kept, k, lane)
    best = lane < N_EXPERTS
    for d in range(1, N_GROUPS):
        later = pltpu.roll(score, LANES - ge * d, axis=1)
        earlier = pltpu.roll(score, ge * d, axis=1)
        best = best & (later <= score) & (earlier < score)
    return best


def _router_probs(h_bf16, wr_ref, rb_ref, lane):
    logits = _dot(h_bf16, wr_ref[...])
    logits = jnp.where(lane < N_EXPERTS, logits, -jnp.inf)
    mx = jnp.max(logits, axis=-1, keepdims=True)
    ex = jnp.exp(logits - mx)
    probs = ex / jnp.sum(ex, axis=-1, keepdims=True)
    sel = jnp.where(lane < N_EXPERTS, probs + rb_ref[...], -jnp.inf)
    return probs, sel


def _route(probs, sel, lane):
    in_top2 = _in_group_top2(sel, lane)
    pk = jnp.where(_best_group(sel, in_top2, lane) & in_top2, probs, 0.0)
    return pk / jnp.sum(pk, axis=-1, keepdims=True)


def _moe_kernel(final, x_ref, g_ref, sc_ref, sh_ref, gt_ref, gf_ref, wr_ref, rb_ref, wg_ref, wu_ref, wd_ref,
                o_ref, h_s, gate_s):
    e = pl.program_id(1)
    tm = x_ref.shape[0]
    lane = lax.broadcasted_iota(jnp.int32, (tm, LANES), 1)

    @pl.when(e == 0)
    def _():
        y = _rms(x_ref[...], g_ref[...])
        h = (y * (1.0 + sc_ref[...]) + sh_ref[...]).astype(BF16)
        h_s[...] = h
        probs, selv = _router_probs(h, wr_ref, rb_ref, lane)
        gate_s[...] = _route(probs, selv, lane)
        o_ref[...] = jnp.zeros_like(o_ref)

    h = h_s[...]
    gcol = jnp.sum(jnp.where(lane == e, gate_s[...], 0.0), axis=-1, keepdims=True)
    hg = _bdot(h, wg_ref[...])
    hu = _bdot(h, wu_ref[...])
    act = hg * jax.nn.sigmoid(hg) * hu * gcol
    o_ref[...] += _bdot(act, wd_ref[...])

    @pl.when(e == N_EXPERTS - 1)
    def _():
        res = x_ref[...] + gt_ref[...] * o_ref[...]
        o_ref[...] = _rms(res, gf_ref[...]) if final else res


def _moe(x, g, grp, wr, rb, wg, wu, wd, l, tm, gf=None):
    row = lambda i, e: (i, 0)
    final = gf is not None
    gain = (gf if final else g).reshape(1, D_MODEL)
    return pl.pallas_call(
        functools.partial(_moe_kernel, final),
        out_shape=jax.ShapeDtypeStruct((grp.rows, D_MODEL), F32),
        grid=(grp.rows // tm, N_EXPERTS),
        in_specs=[pl.BlockSpec((tm, D_MODEL), row),
                  pl.BlockSpec((1, D_MODEL), lambda i, e: (0, 0)),
                  grp.mod_spec(4, tm),
                  grp.mod_spec(3, tm),
                  grp.mod_spec(5, tm),
                  pl.BlockSpec((1, D_MODEL), lambda i, e: (0, 0)),
                  pl.BlockSpec((D_MODEL, LANES), lambda i, e: (0, 0)),
                  pl.BlockSpec((1, LANES), lambda i, e: (0, 0)),
                  pl.BlockSpec((None, None, D_MODEL, D_EXPERT), lambda i, e: (l, e, 0, 0)),
                  pl.BlockSpec((None, None, D_MODEL, D_EXPERT), lambda i, e: (l, e, 0, 0)),
                  pl.BlockSpec((None, None, D_EXPERT, D_MODEL), lambda i, e: (l, e, 0, 0))],
        out_specs=pl.BlockSpec((tm, D_MODEL), row),
        scratch_shapes=[pltpu.VMEM((tm, D_MODEL), BF16),
                        pltpu.VMEM((tm, LANES), F32)],
        compiler_params=_cparams(("arbitrary", "arbitrary"), MOE_VMEM_LIMIT),
    )(x, g.reshape(1, D_MODEL), grp.mod4, grp.mod4, grp.mod4, gain, wr, rb, wg, wu, wd)


MOE_SORT_TILE = 1024
MOE_DMA_ROWS = 2048


def _moe_router_kernel(x_ref, g_ref, sc_ref, sh_ref, wr_ref, rb_ref, tri_ref,
                       h_ref, gr_ref, cnt_ref, base_s):
    i = pl.program_id(0)
    tm = x_ref.shape[0]
    lane = lax.broadcasted_iota(jnp.int32, (tm, LANES), 1)

    @pl.when(i == 0)
    def _():
        base_s[...] = jnp.zeros_like(base_s)

    y = _rms(x_ref[...], g_ref[...])
    h = y * (1.0 + sc_ref[...]) + sh_ref[...]
    h_ref[...] = h
    probs, sel = _router_probs(h.astype(BF16), wr_ref, rb_ref, lane)
    best = _best_group(sel, _in_group_top2(sel, lane), lane)
    first = best & ((lane & (EXPERTS_PER_GROUP - 1)) == 0)
    gid = jnp.sum(jnp.where(first, lane >> 2, 0).astype(F32), axis=-1, keepdims=True)
    onehot = (lane.astype(F32) == gid).astype(BF16)
    incl = _dot(tri_ref[...], onehot) + base_s[...]
    rank = jnp.sum(jnp.where(lane.astype(F32) == gid, incl - 1.0, 0.0), axis=-1, keepdims=True)
    gr_ref[...] = jnp.where(lane == 0, gid, jnp.where(lane == 1, rank, 0.0)).astype(jnp.int32)
    base_s[...] = incl[tm - 1:tm, :]
    cnt_ref[...] = incl[tm - 1:tm, :].astype(jnp.int32)


def _scatter_rows_kernel(dest_ref, src_ref, zeros_hbm, dst_hbm, sem):
    del zeros_hbm
    base = pl.program_id(0) * MOE_DMA_ROWS

    def issue(t, carry):
        pltpu.make_async_copy(src_ref.at[pl.ds(t, 1)], dst_hbm.at[pl.ds(dest_ref[base + t], 1)], sem).start()
        return carry

    def drain(t, carry):
        pltpu.make_async_copy(src_ref.at[pl.ds(0, 1)], dst_hbm.at[pl.ds(0, 1)], sem).wait()
        return carry

    lax.fori_loop(0, MOE_DMA_ROWS, issue, 0, unroll=8)
    lax.fori_loop(0, MOE_DMA_ROWS, drain, 0, unroll=8)


def _moe_group_kernel(tg_ref, nv_ref, xs_ref, wr_ref, rb_ref, wg_ref, wu_ref, wd_ref, ys_ref, h_s, gate_s):
    i = pl.program_id(0)
    j = pl.program_id(1)
    tm = xs_ref.shape[0]
    lane = lax.broadcasted_iota(jnp.int32, (tm, LANES), 1)
    grp = tg_ref[i]

    @pl.when((i >= nv_ref[0]) & (j == 0))
    def _():
        ys_ref[...] = jnp.zeros_like(ys_ref)

    @pl.when(i < nv_ref[0])
    def _():
        @pl.when(j == 0)
        def _():
            h = xs_ref[...].astype(BF16)
            h_s[...] = h
            probs, sel = _router_probs(h, wr_ref, rb_ref, lane)
            pk = jnp.where(_in_group_top2(sel, lane) & ((lane >> 2) == grp), probs, 0.0)
            psum = jnp.sum(pk, axis=-1, keepdims=True)
            gate_s[...] = pk / jnp.where(psum > 0.0, psum, 1.0)
            ys_ref[...] = jnp.zeros_like(ys_ref)

        h = h_s[...]
        gcol = jnp.sum(jnp.where(lane == grp * EXPERTS_PER_GROUP + j, gate_s[...], 0.0), axis=-1, keepdims=True)
        hg = _bdot(h, wg_ref[...])
        hu = _bdot(h, wu_ref[...])
        act = hg * jax.nn.sigmoid(hg) * hu * gcol
        ys_ref[...] += _bdot(act, wd_ref[...])


def _gather_residual_kernel(final, dest_ref, x_ref, gt_ref, gf_ref, ys_hbm, o_ref, buf, sem):
    i = pl.program_id(0)
    n = pl.num_programs(0)
    tm = x_ref.shape[0]

    def start_tile(tile, slot):
        def issue(t, carry):
            pltpu.make_async_copy(ys_hbm.at[pl.ds(dest_ref[tile * tm + t], 1)], buf.at[slot, pl.ds(t, 1)],
                                  sem.at[slot]).start()
            return carry
        lax.fori_loop(0, tm, issue, 0, unroll=8)

    @pl.when(i == 0)
    def _():
        start_tile(0, 0)

    @pl.when(i + 1 < n)
    def _():
        start_tile(i + 1, (i + 1) % 2)

    slot = i % 2

    def drain(t, carry):
        pltpu.make_async_copy(ys_hbm.at[pl.ds(0, 1)], buf.at[slot, pl.ds(0, 1)], sem.at[slot]).wait()
        return carry

    lax.fori_loop(0, tm, drain, 0, unroll=8)
    res = x_ref[...] + gt_ref[...] * buf[slot]
    if final:
        res = _rms(res, gf_ref[...])
    o_ref[...] = res


def _moe_sorted(x, g, grp, wr, rb, wg, wu, wd, l, tm, gf=None):
    rows = grp.rows
    te = MOE_SORT_TILE
    ntile = rows // te + N_GROUPS
    cap = ntile * te
    tri = jnp.asarray(np.tril(np.ones((tm, tm), np.float32)), BF16)
    row = lambda i: (i, 0)
    h, gr, cnt = pl.pallas_call(
        _moe_router_kernel,
        out_shape=(jax.ShapeDtypeStruct((rows, D_MODEL), F32),
                   jax.ShapeDtypeStruct((rows, LANES), jnp.int32),
                   jax.ShapeDtypeStruct((1, LANES), jnp.int32)),
        grid=(rows // tm,),
        in_specs=[pl.BlockSpec((tm, D_MODEL), row),
                  pl.BlockSpec((1, D_MODEL), lambda i: (0, 0)),
                  grp.mod_spec(4, tm),
                  grp.mod_spec(3, tm),
                  pl.BlockSpec((D_MODEL, LANES), lambda i: (0, 0)),
                  pl.BlockSpec((1, LANES), lambda i: (0, 0)),
                  pl.BlockSpec((tm, tm), lambda i: (0, 0))],
        out_specs=(pl.BlockSpec((tm, D_MODEL), row),
                   pl.BlockSpec((tm, LANES), row),
                   pl.BlockSpec((1, LANES), lambda i: (0, 0))),
        scratch_shapes=[pltpu.VMEM((1, LANES), F32)],
        compiler_params=_cparams(("arbitrary",)),
    )(x, g.reshape(1, D_MODEL), grp.mod4, grp.mod4, wr, rb, tri)

    counts = cnt[0, :N_GROUPS]
    padded = ((counts + te - 1) // te) * te
    ends = jnp.cumsum(padded)
    dest = (ends - padded)[gr[:, 0]] + gr[:, 1]
    tile_group = jnp.minimum(jnp.searchsorted(ends, jnp.arange(ntile, dtype=jnp.int32) * te, side="right"),
                             N_GROUPS - 1).astype(jnp.int32)
    n_valid = (ends[N_GROUPS - 1] // te).astype(jnp.int32).reshape(1)

    xs = pl.pallas_call(
        _scatter_rows_kernel,
        out_shape=jax.ShapeDtypeStruct((cap, D_MODEL), F32),
        grid_spec=pltpu.PrefetchScalarGridSpec(
            num_scalar_prefetch=1,
            grid=(rows // MOE_DMA_ROWS,),
            in_specs=[pl.BlockSpec((MOE_DMA_ROWS, D_MODEL), lambda i, d: (i, 0)),
                      pl.BlockSpec(memory_space=pl.ANY)],
            out_specs=pl.BlockSpec(memory_space=pl.ANY),
            scratch_shapes=[pltpu.SemaphoreType.DMA(())]),
        input_output_aliases={2: 0},
        compiler_params=_cparams(("arbitrary",)),
    )(dest, h, jnp.zeros((cap, D_MODEL), F32))

    ys = pl.pallas_call(
        _moe_group_kernel,
        out_shape=jax.ShapeDtypeStruct((cap, D_MODEL), F32),
        grid_spec=pltpu.PrefetchScalarGridSpec(
            num_scalar_prefetch=2,
            grid=(ntile, EXPERTS_PER_GROUP),
            in_specs=[pl.BlockSpec((te, D_MODEL), lambda i, j, tg, nv: (i, 0)),
                      pl.BlockSpec((D_MODEL, LANES), lambda i, j, tg, nv: (0, 0)),
                      pl.BlockSpec((1, LANES), lambda i, j, tg, nv: (0, 0)),
                      pl.BlockSpec((None, None, D_MODEL, D_EXPERT),
                                   lambda i, j, tg, nv: (l, tg[i] * EXPERTS_PER_GROUP + j, 0, 0)),
                      pl.BlockSpec((None, None, D_MODEL, D_EXPERT),
                                   lambda i, j, tg, nv: (l, tg[i] * EXPERTS_PER_GROUP + j, 0, 0)),
                      pl.BlockSpec((None, None, D_EXPERT, D_MODEL),
                                   lambda i, j, tg, nv: (l, tg[i] * EXPERTS_PER_GROUP + j, 0, 0))],
            out_specs=pl.BlockSpec((te, D_MODEL), lambda i, j, tg, nv: (i, 0)),
            scratch_shapes=[pltpu.VMEM((te, D_MODEL), BF16), pltpu.VMEM((te, LANES), F32)]),
        compiler_params=_cparams(("arbitrary", "arbitrary"), MOE_VMEM_LIMIT),
    )(tile_group, n_valid, xs, wr, rb, wg, wu, wd)

    gt_spec = grp.mod_spec(5, tm)
    final = gf is not None
    gain = (gf if final else g).reshape(1, D_MODEL)
    return pl.pallas_call(
        functools.partial(_gather_residual_kernel, final),
        out_shape=jax.ShapeDtypeStruct((rows, D_MODEL), F32),
        grid_spec=pltpu.PrefetchScalarGridSpec(
            num_scalar_prefetch=1,
            grid=(rows // tm,),
            in_specs=[pl.BlockSpec((tm, D_MODEL), lambda i, d: (i, 0)),
                      pl.BlockSpec(gt_spec.block_shape, lambda i, d: gt_spec.index_map(i)),
                      pl.BlockSpec((1, D_MODEL), lambda i, d: (0, 0)),
                      pl.BlockSpec(memory_space=pl.ANY)],
            out_specs=pl.BlockSpec((tm, D_MODEL), lambda i, d: (i, 0)),
            scratch_shapes=[pltpu.VMEM((2, tm, D_MODEL), F32), pltpu.SemaphoreType.DMA((2,))]),
        compiler_params=_cparams(("arbitrary",)),
    )(dest, x, grp.mod4, gain, ys)


def _block_diag(w):
    eye = jnp.eye(C_BLOCKS, dtype=w.dtype)
    return jnp.einsum("hg,hij->higj", eye, w).reshape(D_BRANCH, D_BRANCH)


def _layer_params(l, p):
    row = lambda a: a[l].reshape(1, -1)
    ones = jnp.asarray(np.kron(np.eye(4), np.ones((A_HEAD, A_HEAD))), BF16)
    pad_lo = lambda w: jnp.pad(w, ((0, 64), (0, 0)))
    pad_hi = lambda w: jnp.pad(w, ((64, 0), (0, 0)))
    rw = dict(mu=row(p["a_mu"]), w0=row(p["a_w0"]), w2=pad_lo(p["a_w2"][l]).astype(BF16),
              a0=row(p["a_a0"]), a2=pad_hi(p["a_a2"][l]).astype(BF16), g2=p["a_g2"][l].astype(BF16),
              kk=row(p["a_kk"]), ka=row(p["a_ka"]), rk=row(p["a_rk"]), lng=row(p["a_ln_g"]),
              lnb=row(p["a_ln_b"]), ones=ones)
    lru = dict(cw=p["c_conv_w"][l], cb=row(p["c_conv_b"]),
               wri=jnp.concatenate([_block_diag(p["c_wr"][l]), _block_diag(p["c_wi"][l])], axis=1).astype(BF16),
               br=row(p["c_br"]), bi=row(p["c_bi"]), lam=row(p["c_lam"]))
    return dict(
        rw=rw, lru=lru,
        w_branch=p["w_branch"][l].astype(BF16),
        w_out=p["w_out"][l].astype(BF16),
        wg=p["moe_wg"], wu=p["moe_wu"], wd=p["moe_wd"],
        norm_mix=p["norm_mix"][l], norm_ffn=p["norm_ffn"][l])


def kernel(x_prompt, x_sample, c_prompt, c_sample, state_rwkv_shift, state_rwkv_wkv, state_ret, state_lru_h, state_lru_conv, norm_mix, norm_ffn, norm_final, ada_w, ada_b, w_in, a_mu, a_w0, a_w2, a_a0, a_a2, a_g2, a_kk, a_ka, a_rk, a_ln_g, a_ln_b, c_conv_w, c_conv_b, c_wr, c_br, c_wi, c_bi, c_lam, w_branch, w_out, w_router, router_bias, moe_wg, moe_wu, moe_wd):
    p = dict(norm_mix=norm_mix, norm_ffn=norm_ffn, w_in=w_in, a_mu=a_mu, a_w0=a_w0, a_w2=a_w2,
             a_a0=a_a0, a_a2=a_a2, a_g2=a_g2, a_kk=a_kk, a_ka=a_ka, a_rk=a_rk, a_ln_g=a_ln_g,
             a_ln_b=a_ln_b, c_conv_w=c_conv_w, c_conv_b=c_conv_b, c_wr=c_wr, c_br=c_br, c_wi=c_wi,
             c_bi=c_bi, c_lam=c_lam, w_branch=w_branch, w_out=w_out, moe_wg=moe_wg, moe_wu=moe_wu,
             moe_wd=moe_wd)
    bp, lp_, _ = x_prompt.shape
    bs, ls, _ = x_sample.shape
    layers = [_layer_params(l, p) for l in range(DEPTH)]
    w3 = _arrange_w_in(w_in)
    router = (jnp.pad(w_router, ((0, 0), (0, LANES - N_EXPERTS))).astype(BF16),
              jnp.pad(router_bias, (0, LANES - N_EXPERTS)).reshape(1, LANES))

    n_c = bp + bs
    pad_c = (-n_c) % 16
    c_all = jnp.pad(jnp.concatenate([c_prompt, c_sample], axis=0), ((0, pad_c), (0, 0)))
    mods = _ada(c_all, ada_w, ada_b)

    def run(x, batch, seq, mod, states, pos0, cfg):
        xs = x.reshape(batch * seq, D_MODEL)
        grps = []
        for l in range(DEPTH):
            if cfg["per_token"]:
                m = jnp.repeat(mod[l], seq, axis=0).reshape(batch * seq, 6, D_MODEL)
                m4 = m.transpose(1, 0, 2)[None]
            else:
                m4 = mod[l].reshape(batch, 6, 1, D_MODEL)
            grps.append(_Group(batch, seq, m4))
        return _trunk_layers(xs, grps, states, layers, w3, router, norm_final, pos0, cfg)

    zeros = lambda s: jnp.zeros((DEPTH, bp) + s.shape[2:], x_prompt.dtype)
    st_prompt = (zeros(state_rwkv_shift), zeros(state_rwkv_wkv), zeros(state_ret),
                 zeros(state_lru_h), zeros(state_lru_conv))
    st_sample = (state_rwkv_shift, state_rwkv_wkv, state_ret, state_lru_h, state_lru_conv)
    cfg_p = dict(per_token=False, tm=min(512, lp_), tm_mg=min(256, lp_), tm_mm=min(1024, lp_),
                 tm_moe=min(1024, lp_), moe_sorted=(bp * lp_) % MOE_DMA_ROWS == 0, rwkv_nb=8, lru_nb=1, lru_lb=min(256, lp_))
    cfg_s = dict(per_token=True, tm=min(512, bs * ls), tm_mg=min(256, bs * ls), tm_mm=min(512, bs * ls),
                 tm_moe=min(512, bs * ls), moe_sorted=False, rwkv_nb=16, lru_nb=16, lru_lb=ls)
    y_p, new_p = run(x_prompt, bp, lp_, mods[:, :bp], st_prompt, 0, cfg_p)
    y_s, new_s = run(x_sample, bs, ls, mods[:, bp:bp + bs], st_sample, PAST_LEN, cfg_s)
    return (y_p, y_s) + new_p + new_s


def _trunk_layers(x, grps, states, layers, w3, router, norm_final, pos0, cfg):
    batch, seq = grps[0].batch, grps[0].seq
    tabs = _ret_tables(seq, pos0)
    wr, rb = router
    outs = [[] for _ in range(5)]
    for l, lp in enumerate(layers):
        grp = grps[l]
        h = _norm_mod(x, lp["norm_mix"], grp, 1, 0, cfg["tm"])
        pa = _matmul(h, w3, l, W_PA, W_PA_PAD, cfg["tm_mm"], 1024)
        rq = _matmul(h, w3, l, W_RQ, 4 * D_BRANCH, cfg["tm_mm"], 1024)
        cx = _matmul(h, w3, l, W_CX, 2 * D_BRANCH, cfg["tm_mm"], 1024)
        if seq % RW_CHUNK == 0:
            o_a, wkv = _rwkv_chunked(pa, states[0][l], states[1][l], lp["rw"], batch, seq,
                                     min(4, seq // RW_CHUNK))
        else:
            o_a, wkv = _rwkv_steps(pa, states[0][l], states[1][l], lp["rw"], batch, seq, cfg["rwkv_nb"])
        o_b, ret = _retention(rq, states[2][l], tabs, batch, seq)
        o_c, lru_h = _rglru(cx, states[4][l], states[3][l], lp["lru"], batch, seq,
                            cfg["lru_nb"], cfg["lru_lb"])
        x = _merge_out(h, o_a, o_b, o_c, x, grp, w3, l, lp["w_branch"], lp["w_out"], cfg["tm_mg"])
        moe = _moe_sorted if cfg["moe_sorted"] else _moe
        x = moe(x, lp["norm_ffn"], grp, wr, rb, lp["wg"], lp["wu"], lp["wd"], l, cfg["tm_moe"],
                gf=norm_final if l == DEPTH - 1 else None)
        outs[0].append(pa.reshape(batch, seq, W_PA_PAD)[:, -1, :A_PROJ])
        outs[1].append(wkv)
        outs[2].append(ret)
        outs[3].append(lru_h)
        outs[4].append(cx.reshape(batch, seq, 2 * D_BRANCH)[:, seq - (CONV_W - 1):, :D_BRANCH])
    return x.reshape(batch, seq, D_MODEL), tuple(jnp.stack(o) for o in outs)
```

```python
import functools
import math

import numpy as np
import jax
import jax.numpy as jnp
from jax import lax
from jax.experimental import pallas as pl
from jax.experimental.pallas import tpu as pltpu

F32 = jnp.float32
BF16 = jnp.bfloat16

D_MODEL = 1024
DEPTH = 2
PAST_LEN = 16384
D_BRANCH = 512
A_HEAD = 64
A_HEADS = 8
A_PROJ = 1792
A_NORM_EPS = 64e-5
A_KK_EPS = 1e-12
R_HEAD = 128
R_HEADS = 4
R_CHUNK = 128
R_NORM_EPS = 1e-6
ROPE_BASE = 10000.0
C_BLOCK = 64
C_BLOCKS = 8
CONV_W = 4
LRU_C = 8.0
N_EXPERTS = 16
N_GROUPS = 4
EXPERTS_PER_GROUP = 4
TOP_K = 2
D_EXPERT = 512
NORM_EPS = 1e-6

LANES = 128
RET_ROWS = 128
RET_CHUNKS_PER_STEP = 4
VMEM_LIMIT = 48 * 1024 * 1024
MOE_VMEM_LIMIT = 56 * 1024 * 1024


def _cparams(sem, vmem=VMEM_LIMIT):
    return pltpu.CompilerParams(dimension_semantics=sem, vmem_limit_bytes=vmem)


def _dot(a, b):
    return jnp.dot(a, b, preferred_element_type=F32)


def _bdot(a, b):
    return jnp.dot(a.astype(BF16), b.astype(BF16), preferred_element_type=F32)


def _split3(x):
    hi = x.astype(BF16)
    r1 = x - hi.astype(F32)
    mid = r1.astype(BF16)
    lo = (r1 - mid.astype(F32)).astype(BF16)
    return hi, mid, lo


def _segsum(x, ones_bf16):
    w = ones_bf16.shape[0]
    hi, mid, lo = _split3(x)
    parts = []
    for j in range(x.shape[1] // w):
        c = slice(j * w, (j + 1) * w)
        parts.append((_dot(lo[:, c], ones_bf16) + _dot(mid[:, c], ones_bf16)) + _dot(hi[:, c], ones_bf16))
    return jnp.concatenate(parts, axis=1)


def _ada_kernel(c_ref, w_ref, b_ref, o_ref):
    c = c_ref[...]
    s = c * jax.nn.sigmoid(c)
    o_ref[...] = _bdot(s, w_ref[...]) + b_ref[...]


def _ada(c, ada_w, ada_b):
    rows = c.shape[0]
    tn = 1536
    return pl.pallas_call(
        _ada_kernel,
        out_shape=jax.ShapeDtypeStruct((DEPTH, rows, 6 * D_MODEL), F32),
        grid=(DEPTH, 6 * D_MODEL // tn),
        in_specs=[
            pl.BlockSpec((rows, D_MODEL), lambda l, j: (0, 0)),
            pl.BlockSpec((None, D_MODEL, tn), lambda l, j: (l, 0, j)),
            pl.BlockSpec((None, 1, tn), lambda l, j: (l, 0, j)),
        ],
        out_specs=pl.BlockSpec((None, rows, tn), lambda l, j: (l, 0, j)),
        compiler_params=_cparams(("arbitrary", "arbitrary")),
    )(c, ada_w, ada_b.reshape(DEPTH, 1, 6 * D_MODEL))


class _Group:
    def __init__(self, batch, seq, mod4):
        self.batch = batch
        self.seq = seq
        self.rows = batch * seq
        self.mod4 = mod4
        self.per_token = mod4.shape[0] == 1 and mod4.shape[2] != 1

    def mod_spec(self, k, tm):
        if self.per_token:
            return pl.BlockSpec((None, None, tm, D_MODEL), lambda i, *_: (0, k, i, 0))
        seq = self.seq
        return pl.BlockSpec((None, None, 1, D_MODEL), lambda i, *_: ((i * tm) // seq, k, 0, 0))


def _rms(x, g):
    return x * lax.rsqrt(jnp.mean(x * x, axis=-1, keepdims=True) + NORM_EPS) * g


def _norm_mod_kernel(x_ref, g_ref, sc_ref, sh_ref, o_ref):
    y = _rms(x_ref[...], g_ref[...])
    o_ref[...] = (y * (1.0 + sc_ref[...]) + sh_ref[...]).astype(o_ref.dtype)


def _norm_mod(x, g, grp, k_sc, k_sh, tm):
    return pl.pallas_call(
        _norm_mod_kernel,
        out_shape=jax.ShapeDtypeStruct((grp.rows, D_MODEL), BF16),
        grid=(grp.rows // tm,),
        in_specs=[
            pl.BlockSpec((tm, D_MODEL), lambda i: (i, 0)),
            pl.BlockSpec((1, D_MODEL), lambda i: (0, 0)),
            grp.mod_spec(k_sc, tm),
            grp.mod_spec(k_sh, tm),
        ],
        out_specs=pl.BlockSpec((tm, D_MODEL), lambda i: (i, 0)),
        compiler_params=_cparams(("arbitrary",)),
    )(x, g.reshape(1, D_MODEL), grp.mod4, grp.mod4)


def _mm_kernel(a_ref, w_ref, o_ref):
    o_ref[...] = _dot(a_ref[...], w_ref[...])


def _matmul(a, w3, l, col0, n, tm, tn):
    rows, k = a.shape
    c0 = col0 // tn
    return pl.pallas_call(
        _mm_kernel,
        out_shape=jax.ShapeDtypeStruct((rows, n), F32),
        grid=(n // tn, rows // tm),
        in_specs=[
            pl.BlockSpec((tm, k), lambda j, i: (i, 0)),
            pl.BlockSpec((None, k, tn), lambda j, i: (l, 0, c0 + j)),
        ],
        out_specs=pl.BlockSpec((tm, tn), lambda j, i: (i, j)),
        compiler_params=_cparams(("arbitrary", "arbitrary")),
    )(a, w3)


W_RQ, W_CX, W_GM, W_PA, W_PA_PAD = 0, 2048, 3072, 6144, 2048


W_BLK = 256


def _arrange_kernel(src_ref, w_ref, o_ref):
    del src_ref
    j = pl.program_id(1)
    real = (W_PA + A_PROJ) // W_BLK

    @pl.when(j < real)
    def _():
        o_ref[...] = w_ref[...].astype(BF16)

    @pl.when(j >= real)
    def _():
        o_ref[...] = jnp.zeros_like(o_ref)


def _arrange_w_in(w_in):
    depth, k, n = w_in.shape
    nblk = n // W_BLK
    first = A_PROJ // W_BLK
    src = list(range(first, nblk)) + list(range(first))
    nout = (W_PA + W_PA_PAD) // W_BLK
    src = jnp.asarray(src + [0] * (nout - len(src)), jnp.int32)
    return pl.pallas_call(
        _arrange_kernel,
        out_shape=jax.ShapeDtypeStruct((depth, k, nout * W_BLK), BF16),
        grid_spec=pltpu.PrefetchScalarGridSpec(
            num_scalar_prefetch=1,
            grid=(depth, nout),
            in_specs=[pl.BlockSpec((None, k, W_BLK), lambda l, j, s: (l, 0, s[j]))],
            out_specs=pl.BlockSpec((None, k, W_BLK), lambda l, j, s: (l, 0, j))),
        compiler_params=_cparams(("arbitrary", "arbitrary")),
    )(src, w_in)


def _rwkv_pre(pa, prev, mu_ref, w0_ref, w2_ref, a0_ref, a2_ref, g2_ref, kk_ref, ka_ref, ones):
    pm = pa + (prev - pa) * mu_ref[...]
    r = pm[:, 0:512]
    k = pm[:, 512:1024]
    v = pm[:, 1024:1536]
    xwa = pm[:, 1536:1664]
    xg = pm[:, 1664:1792]
    w_log = -jax.nn.softplus(-(w0_ref[...] + _bdot(jnp.tanh(xwa), w2_ref[...]))) - 0.5
    logw = -jnp.exp(w_log)
    a = jax.nn.sigmoid(a0_ref[...] + _bdot(xwa, a2_ref[...]))
    g = _bdot(jax.nn.sigmoid(xg), g2_ref[...])
    kk = k * kk_ref[...]
    kk = kk * lax.rsqrt(_segsum(kk * kk, ones) + A_KK_EPS)
    k2 = k * (1.0 + (a - 1.0) * ka_ref[...])
    return r, logw, k2, v, -kk, kk * a, g


def _rwkv_post(o, r, k2, v, g, rk_ref, lng_ref, lnb_ref, ones):
    mean = _segsum(o, ones) * (1.0 / A_HEAD)
    oc = o - mean
    var = _segsum(oc * oc, ones) * (1.0 / A_HEAD)
    o = oc * lax.rsqrt(var + A_NORM_EPS) * lng_ref[...] + lnb_ref[...]
    bonus = _segsum(r * k2 * rk_ref[...], ones) * v
    return (o + bonus) * g


_NN = (((1,), (0,)), ((), ()))
_NT = (((1,), (1,)), ((), ()))
_TN = (((0,), (0,)), ((), ()))
RW_CHUNK = 64
RW_PAIRS = A_HEADS // 2


def _bdg(a, b, dims):
    return lax.dot_general(a.astype(BF16), b.astype(BF16), dims, preferred_element_type=F32)


def _rwkv_chunk_kernel(nck,
                       pa_ref, sh0_ref, g0_ref, mu_ref, w0_ref, w2_ref, a0_ref, a2_ref, g2_ref,
                       kk_ref, ka_ref, rk_ref, lng_ref, lnb_ref, ones_ref, tri_ref,
                       o_ref, gout_ref,
                       carry_ref, prev_ref, g_s, wu_s, m_s, zy_s, o_s):
    c = pl.program_id(1)
    ck = RW_CHUNK
    ones = ones_ref[...]

    @pl.when(c == 0)
    def _():
        carry_ref[...] = sh0_ref[...]
        g_s[...] = g0_ref[...]

    pa = pa_ref[...]
    rows = pa.shape[0]
    prev_ref[...] = pltpu.roll(pa, 1, axis=0)
    prev_ref[pl.ds(0, 1), :] = carry_ref[...]
    carry_ref[...] = pa[rows - 1:rows, :]
    r, logw, k2, v, an, bn, g = _rwkv_pre(pa, prev_ref[...], mu_ref, w0_ref, w2_ref, a0_ref, a2_ref,
                                          g2_ref, kk_ref, ka_ref, ones)

    tri = tri_ref[...]
    hi, mid, lo = _split3(logw)
    cum =(_dot(tri, lo) + _dot(tri, mid)) + _dot(tri, hi)
    cum_last = jnp.concatenate(
        [jnp.broadcast_to(cum[(i + 1) * ck - 1:(i + 1) * ck, :], (ck, D_BRANCH)) for i in range(nck)], axis=0)
    gam = jnp.exp(cum)
    inv = jnp.exp(-cum)
    to_end = jnp.exp(cum_last - cum)
    a_t = an * jnp.exp(cum - logw)
    b_t = bn * inv
    k_t = k2 * inv
    r_t = r * gam
    b_e = bn * to_end
    k_e = k2 * to_end

    lane = lax.broadcasted_iota(jnp.int32, (ck, LANES), 1)
    rowi = lax.broadcasted_iota(jnp.int32, (ck, LANES), 0)
    m0 = lane < A_HEAD
    coli = lane & (A_HEAD - 1)
    strict = rowi > coli
    incl = rowi >= coli
    eye_p = (rowi == coli).astype(F32)
    r128 = lax.broadcasted_iota(jnp.int32, (LANES, LANES), 0)
    c128 = lax.broadcasted_iota(jnp.int32, (LANES, LANES), 1)
    blockmask = (r128 < A_HEAD) == (c128 < A_HEAD)
    eye128 = r128 == c128
    bk_t = jnp.concatenate([b_e, k_e], axis=1).T

    def bd(q):
        q = q.astype(BF16)
        z = jnp.zeros_like(q)
        return jnp.concatenate([jnp.where(m0, q, z), jnp.where(m0, z, q)], axis=0)

    probs = [(i, p) for i in range(nck) for p in range(RW_PAIRS)]
    sl = {(i, p): (slice(i * ck, (i + 1) * ck), slice(p * LANES, (p + 1) * LANES)) for i, p in probs}
    l_pow, l_ak, t_inv = {}, {}, {}
    for q in probs:
        rs, ls = sl[q]
        lhs = jnp.concatenate([a_t[rs, ls], r_t[rs, ls]], axis=0).astype(BF16)
        ab = lax.dot_general(lhs, bd(b_t[rs, ls]), _NT, preferred_element_type=F32)
        ak = lax.dot_general(lhs, bd(k_t[rs, ls]), _NT, preferred_element_type=F32)
        l_pow[q] = jnp.where(strict, ab[:ck], 0.0)
        l_ak[q] = jnp.where(strict, ak[:ck], 0.0)
        m_s[q[0], q[1], :, 0:LANES] = jnp.where(incl, ab[ck:], 0.0)
        m_s[q[0], q[1], :, LANES:2 * LANES] = jnp.where(incl, ak[ck:], 0.0)
        t_inv[q] = eye_p + l_pow[q]
    n = 1
    while 2 * n < ck:
        for q in probs:
            l_pow[q] = _bdg(l_pow[q], bd(l_pow[q]), _NN)
        for q in probs:
            t_inv[q] = t_inv[q] + _bdg(t_inv[q], bd(l_pow[q]), _NN)
        n *= 2
    lak_v = {}
    for q in probs:
        rs, ls = sl[q]
        lak_v[q] = _bdg(l_ak[q], bd(v[rs, ls]), _NN)
    for q in probs:
        rs, ls = sl[q]
        wu = _bdg(t_inv[q], jnp.concatenate([bd(a_t[rs, ls]), bd(lak_v[q])], axis=1), _NN)
        wu_s[q[0], q[1]] = wu
        i, p = q
        half = jnp.zeros((ck, LANES), F32)
        place = (lambda x: jnp.concatenate([x, half], axis=0)) if i % 2 == 0 else \
                (lambda x: jnp.concatenate([half, x], axis=0))
        tcols = slice((i // 2) * LANES, (i // 2 + 1) * LANES)
        b_tr = bk_t[p * LANES:(p + 1) * LANES, tcols]
        k_tr = bk_t[D_BRANCH + p * LANES:D_BRANCH + (p + 1) * LANES, tcols]
        z_t = _bdg(b_tr, place(wu[:, 0:LANES]), _NN)
        y_t = _bdg(jnp.concatenate([b_tr, k_tr], axis=1),
                   jnp.concatenate([place(wu[:, LANES:2 * LANES]), place(v[rs, ls])], axis=0), _NN)
        zy_s[i, p, :, 0:LANES] = jnp.where(blockmask, z_t, 0.0)
        zy_s[i, p, :, LANES:2 * LANES] = jnp.where(blockmask, y_t, 0.0)

    for i in range(nck):
        rs = slice(i * ck, (i + 1) * ck)
        for p in range(RW_PAIRS):
            ls = slice(p * LANES, (p + 1) * LANES)
            h_p = g_s[p]
            h_bf = h_p.astype(BF16)
            gcol = jnp.sum(jnp.where(eye128, gam[(i + 1) * ck - 1:(i + 1) * ck, ls], 0.0), axis=1, keepdims=True)
            g_s[p] = (h_p * gcol + _dot(zy_s[i, p, :, 0:LANES].astype(BF16), h_bf)) + zy_s[i, p, :, LANES:2 * LANES]
            wu = wu_s[i, p]
            u = _dot(wu[:, 0:LANES].astype(BF16), h_bf) + wu[:, LANES:2 * LANES]
            o_s[rs, ls] = _dot(r_t[rs, ls].astype(BF16), h_bf) \
                + _bdg(m_s[i, p], jnp.concatenate([bd(u), bd(v[rs, ls])], axis=0), _NN)

    o_ref[...] = _rwkv_post(o_s[...], r, k2, v, g, rk_ref, lng_ref, lnb_ref, ones).astype(o_ref.dtype)
    gout_ref[...] = g_s[...]


def _rwkv_chunked(pa, shift0, s0, prm, batch, seq, nck):
    lb = nck * RW_CHUNK
    nstep = seq // lb
    s0p = s0.reshape(batch, RW_PAIRS, 2, A_HEAD, A_HEAD)
    eye2 = jnp.eye(2, dtype=s0.dtype)
    g0 = jnp.einsum("bpjvk,ji->bpjkiv", s0p, eye2).reshape(batch, RW_PAIRS, LANES, LANES)
    tri = np.kron(np.eye(nck), np.tril(np.ones((RW_CHUNK, RW_CHUNK)))).astype(np.float32)
    params = [prm[n] for n in ("mu", "w0", "w2", "a0", "a2", "g2", "kk", "ka", "rk", "lng", "lnb", "ones")]
    params.append(jnp.asarray(tri, BF16))

    def full(arr):
        nd = arr.ndim
        return pl.BlockSpec(arr.shape, lambda i, c: (0,) * nd)

    o, g_out = pl.pallas_call(
        functools.partial(_rwkv_chunk_kernel, nck),
        out_shape=(jax.ShapeDtypeStruct((batch * seq, D_BRANCH), BF16),
                   jax.ShapeDtypeStruct((batch, RW_PAIRS, LANES, LANES), F32)),
        grid=(batch, nstep),
        in_specs=[pl.BlockSpec((lb, A_PROJ), lambda i, c: (i * nstep + c, 0)),
                  pl.BlockSpec((None, 1, A_PROJ), lambda i, c: (i, 0, 0)),
                  pl.BlockSpec((None, RW_PAIRS, LANES, LANES), lambda i, c: (i, 0, 0, 0))]
                 + [full(p) for p in params],
        out_specs=(pl.BlockSpec((lb, D_BRANCH), lambda i, c: (i * nstep + c, 0)),
                   pl.BlockSpec((None, RW_PAIRS, LANES, LANES), lambda i, c: (i, 0, 0, 0))),
        scratch_shapes=[pltpu.VMEM((1, A_PROJ), F32),
                        pltpu.VMEM((lb, A_PROJ), F32),
                        pltpu.VMEM((RW_PAIRS, LANES, LANES), F32),
                        pltpu.VMEM((nck, RW_PAIRS, RW_CHUNK, 2 * LANES), F32),
                        pltpu.VMEM((nck, RW_PAIRS, RW_CHUNK, 2 * LANES), F32),
                        pltpu.VMEM((nck, RW_PAIRS, LANES, 2 * LANES), F32),
                        pltpu.VMEM((lb, D_BRANCH), F32)],
        compiler_params=_cparams(("arbitrary", "arbitrary")),
    )(pa, shift0.reshape(batch, 1, A_PROJ), g0, *params)
    g5 = g_out.reshape(batch, RW_PAIRS, 2, A_HEAD, 2, A_HEAD)
    s_new = jnp.stack([g5[:, :, 0, :, 0, :], g5[:, :, 1, :, 1, :]], axis=2).swapaxes(-1, -2)
    return o, s_new.reshape(batch, A_HEADS, A_HEAD, A_HEAD)


def _rwkv_step_kernel(nb, lb,
                      pa_ref, sh0_ref, s0_ref, mu_ref, w0_ref, w2_ref, a0_ref, a2_ref, g2_ref,
                      kk_ref, ka_ref, rk_ref, lng_ref, lnb_ref, ones_ref,
                      o_ref, sout_ref):
    ones = ones_ref[...]
    pa = pa_ref[...].reshape(lb * nb, A_PROJ)
    prev = jnp.concatenate([sh0_ref[...], pa[:(lb - 1) * nb, :]], axis=0)
    r, logw, k2, v, an, bn, g = _rwkv_pre(pa, prev, mu_ref, w0_ref, w2_ref, a0_ref, a2_ref,
                                          g2_ref, kk_ref, ka_ref, ones)
    w = jnp.exp(logw)
    srows = nb * A_HEAD
    rowi = lax.broadcasted_iota(jnp.int32, (srows, D_BRANCH), 0)
    lane = lax.broadcasted_iota(jnp.int32, (srows, D_BRANCH), 1)
    eye = (rowi & (A_HEAD - 1)) == (lane & (A_HEAD - 1))

    def per_seq(x, t):
        xt = x[t * nb:(t + 1) * nb, :]
        return jnp.concatenate([jnp.broadcast_to(xt[b:b + 1, :], (A_HEAD, D_BRANCH)) for b in range(nb)],
                               axis=0)

    s = s0_ref[...].reshape(srows, D_BRANCH)
    outs = []
    for t in range(lb):
        sa = _segsum(s * per_seq(an, t), ones)
        vcol = _segsum(jnp.where(eye, per_seq(v, t), 0.0), ones)
        s = s * per_seq(w, t) + sa * per_seq(bn, t) + vcol * per_seq(k2, t)
        out = _segsum(s * per_seq(r, t), ones)
        outs.append(jnp.sum(jnp.where(eye, out, 0.0).reshape(nb, A_HEAD, D_BRANCH), axis=1))
    o = jnp.concatenate(outs, axis=0)
    res = _rwkv_post(o, r, k2, v, g, rk_ref, lng_ref, lnb_ref, ones).astype(o_ref.dtype)
    o_ref[...] = res.reshape(lb, nb, D_BRANCH)
    sout_ref[...] = s.reshape(nb, A_HEAD, D_BRANCH)


def _rwkv_steps(pa, shift0, s0, prm, batch, seq, nb):
    pa_tm = pa.reshape(batch, seq, pa.shape[1]).transpose(1, 0, 2)
    s0k = s0.transpose(0, 2, 1, 3).reshape(batch, A_HEAD, D_BRANCH)

    def full(arr):
        nd = arr.ndim
        return pl.BlockSpec(arr.shape, lambda i: (0,) * nd)

    params = [prm[n] for n in ("mu", "w0", "w2", "a0", "a2", "g2", "kk", "ka", "rk", "lng", "lnb", "ones")]
    o, s_out = pl.pallas_call(
        functools.partial(_rwkv_step_kernel, nb, seq),
        out_shape=(jax.ShapeDtypeStruct((seq, batch, D_BRANCH), BF16),
                   jax.ShapeDtypeStruct((batch, A_HEAD, D_BRANCH), F32)),
        grid=(batch // nb,),
        in_specs=[pl.BlockSpec((seq, nb, A_PROJ), lambda i: (0, i, 0)),
                  pl.BlockSpec((nb, A_PROJ), lambda i: (i, 0)),
                  pl.BlockSpec((nb, A_HEAD, D_BRANCH), lambda i: (i, 0, 0))]
                 + [full(p) for p in params],
        out_specs=(pl.BlockSpec((seq, nb, D_BRANCH), lambda i: (0, i, 0)),
                   pl.BlockSpec((nb, A_HEAD, D_BRANCH), lambda i: (i, 0, 0))),
        compiler_params=_cparams(("arbitrary",)),
    )(pa_tm, shift0, s0k, *params)
    o = o.transpose(1, 0, 2).reshape(batch * seq, D_BRANCH)
    s_new = s_out.reshape(batch, A_HEAD, A_HEADS, A_HEAD).transpose(0, 2, 1, 3)
    return o, s_new


def _ret_kernel(nb, lb, nck,
                rq_ref, cos_ref, sin_ref, dm_ref, kd_ref, qd_ref, cd_ref, s0_ref,
                o_ref, sout_ref, s_ref):
    c = pl.program_id(1)

    @pl.when(c == 0)
    def _():
        s_ref[...] = s0_ref[...]

    scale = R_HEAD ** -0.5
    half = R_HEAD // 2
    row8 = lax.broadcasted_iota(jnp.int32, (8, R_HEAD), 0)
    pre = {}
    for ci in range(nck):
        rs = slice(ci * RET_ROWS, (ci + 1) * RET_ROWS)
        cos = cos_ref[rs, :]
        sin = sin_ref[rs, :]
        for h in range(R_HEADS):
            lo, hi = h * R_HEAD, (h + 1) * R_HEAD
            q = rq_ref[rs, lo:hi]
            k = rq_ref[rs, 512 + lo:512 + hi]
            v = rq_ref[rs, 1024 + lo:1024 + hi]
            qh = q * cos + pltpu.roll(q, half, axis=1) * sin
            kh = (k * cos + pltpu.roll(k, half, axis=1) * sin) * scale
            scores = lax.dot_general(qh.astype(BF16), kh.astype(BF16), (((1,), (1,)), ((), ())),
                                     preferred_element_type=F32) * dm_ref[h]
            pre[ci, h] = (_bdot(scores, v), qh * qd_ref[:, lo:hi], kh * kd_ref[:, lo:hi], v)
    for ci, h in sorted(pre):
        rs = slice(ci * RET_ROWS, (ci + 1) * RET_ROWS)
        lo, hi = h * R_HEAD, (h + 1) * R_HEAD
        o, qd, ku, v = pre[ci, h]
        gate = rq_ref[rs, 1536 + lo:1536 + hi]
        cd = cd_ref[h]
        if nb == 1:
            s = s_ref[0, h]
            o = o + _bdot(qd, s)
            upd = lax.dot_general(ku.astype(BF16), v.astype(BF16), (((0,), (0,)), ((), ())),
                                  preferred_element_type=F32)
            s_ref[0, h] = s * cd + upd
        else:
            per_tile = 8 // lb
            inter = []
            for i in range(RET_ROWS // 8):
                qd_t = qd[i * 8:(i + 1) * 8, :]
                ku_t = ku[i * 8:(i + 1) * 8, :]
                v_t = v[i * 8:(i + 1) * 8, :].astype(BF16)
                acc = None
                for j in range(per_tile):
                    b = i * per_tile + j
                    m = (row8 >= j * lb) & (row8 < (j + 1) * lb)
                    s = s_ref[b, h]
                    part = _bdot(jnp.where(m, qd_t, 0.0), s)
                    acc = part if acc is None else acc + part
                    upd = lax.dot_general(jnp.where(m, ku_t, 0.0).astype(BF16), v_t,
                                          (((0,), (0,)), ((), ())), preferred_element_type=F32)
                    s_ref[b, h] = s * cd + upd
                inter.append(acc)
            o = o + jnp.concatenate(inter, axis=0)
        oc = o - jnp.mean(o, axis=-1, keepdims=True)
        on = oc * lax.rsqrt(jnp.mean(oc * oc, axis=-1, keepdims=True) + R_NORM_EPS)
        o_ref[rs, lo:hi] = (gate * jax.nn.sigmoid(gate) * on).astype(o_ref.dtype)
    sout_ref[...] = s_ref[...]


def _retention(rq, s0_all, l, tabs, batch, seq):
    lb = min(R_CHUNK, seq)
    nb = RET_ROWS // lb
    nck = RET_CHUNKS_PER_STEP if (nb == 1 and (seq // lb) % RET_CHUNKS_PER_STEP == 0) else 1
    rows = RET_ROWS * nck
    nchunk = seq // (lb * nck)
    cos, sin, dm, kd, qd, cd = tabs
    ntab = cos.shape[0] // rows

    def const(arr):
        nd = arr.ndim
        return pl.BlockSpec(arr.shape, lambda i, c: (0,) * nd)

    tab_idx = (lambda i, c: (c, 0)) if ntab > 1 else (lambda i, c: (0, 0))
    return pl.pallas_call(
        functools.partial(_ret_kernel, nb, lb, nck),
        out_shape=(jax.ShapeDtypeStruct((batch * seq, D_BRANCH), BF16),
                   jax.ShapeDtypeStruct((batch, R_HEADS, R_HEAD, R_HEAD), F32)),
        grid=(batch // nb, nchunk),
        in_specs=[pl.BlockSpec((rows, 4 * D_BRANCH), lambda i, c: (i * nchunk + c, 0)),
                  pl.BlockSpec((rows, R_HEAD), tab_idx),
                  pl.BlockSpec((rows, R_HEAD), tab_idx),
                  const(dm), const(kd), const(qd), const(cd),
                  pl.BlockSpec((None, nb, R_HEADS, R_HEAD, R_HEAD), lambda i, c: (l, i, 0, 0, 0))],
        out_specs=(pl.BlockSpec((rows, D_BRANCH), lambda i, c: (i * nchunk + c, 0)),
                   pl.BlockSpec((nb, R_HEADS, R_HEAD, R_HEAD), lambda i, c: (i, 0, 0, 0))),
        scratch_shapes=[pltpu.VMEM((nb, R_HEADS, R_HEAD, R_HEAD), F32)],
        compiler_params=_cparams(("arbitrary", "arbitrary")),
    )(rq, cos, sin, dm, kd, qd, cd, s0_all)


def _ret_tables(seq, pos0):
    lb = min(R_CHUNK, seq)
    nb = RET_ROWS // lb
    half = R_HEAD // 2
    pos = pos0 + jnp.arange(seq, dtype=jnp.int32)
    inv = ROPE_BASE ** (-jnp.arange(half, dtype=F32) / half)
    ang = pos.astype(F32)[:, None] * inv[None, :]
    cos, sin = jnp.cos(ang), jnp.sin(ang)
    cos2 = jnp.concatenate([cos, cos], axis=-1)
    sin2 = jnp.concatenate([-sin, sin], axis=-1)
    if nb > 1:
        cos2 = jnp.tile(cos2, (nb, 1))
        sin2 = jnp.tile(sin2, (nb, 1))
    log_g = jnp.log1p(-jnp.exp2(-5.0 - jnp.arange(R_HEADS, dtype=F32)))
    idx = jnp.arange(lb, dtype=F32)
    diff = idx[:, None] - idx[None, :]
    dmask = jnp.where(diff >= 0, jnp.exp(log_g[:, None, None] * jnp.maximum(diff, 0.0)), 0.0)
    if nb > 1:
        dmask = jnp.einsum("ab,hij->haibj", jnp.eye(nb, dtype=F32), dmask).reshape(
            R_HEADS, RET_ROWS, RET_ROWS)
    k_decay = jnp.exp(log_g[None, :] * (lb - 1.0 - idx)[:, None])
    q_decay = jnp.exp(log_g[None, :] * (idx + 1.0)[:, None])
    kd = jnp.tile(jnp.repeat(k_decay, R_HEAD, axis=1), (nb, 1))
    qd = jnp.tile(jnp.repeat(q_decay, R_HEAD, axis=1), (nb, 1))
    cd = jnp.broadcast_to(jnp.exp(log_g * lb)[:, None, None], (R_HEADS, 1, R_HEAD))
    return cos2, sin2, dmask, kd, qd, cd


def _lru_kernel(nb, lb, three_d,
                cx_ref, conv0_ref, h0_ref, cw_ref, cb_ref, wri_ref, br_ref, bi_ref, lam_ref,
                o_ref, hout_ref,
                cc_ref, hc_ref, x1_ref, x2_ref, x3_ref, a_ref, b_ref):
    c = pl.program_id(1)
    rows = nb * lb

    @pl.when(c == 0)
    def _():
        cc_ref[...] = conv0_ref[...]
        hc_ref[...] = h0_ref[...]

    cx = cx_ref[...]
    if three_d:
        cx = cx.reshape(rows, 2 * D_BRANCH)
    xb = cx[:, :D_BRANCH]
    gb = cx[:, D_BRANCH:]
    x1_ref[...] = pltpu.roll(xb, 1, axis=0)
    x2_ref[...] = pltpu.roll(xb, 2, axis=0)
    x3_ref[...] = pltpu.roll(xb, 3, axis=0)
    for b in range(nb):
        r0 = b * lb
        c0 = cc_ref[b, 0:1, :]
        c1 = cc_ref[b, 1:2, :]
        c2 = cc_ref[b, 2:3, :]
        x1_ref[pl.ds(r0, 1), :] = c2
        x2_ref[pl.ds(r0, 1), :] = c1
        x2_ref[pl.ds(r0 + 1, 1), :] = c2
        x3_ref[pl.ds(r0, 1), :] = c0
        x3_ref[pl.ds(r0 + 1, 1), :] = c1
        x3_ref[pl.ds(r0 + 2, 1), :] = c2
        cc_ref[b] = xb[r0 + lb - 3:r0 + lb, :]
    cw = cw_ref[...]
    xc = cb_ref[...] + (((x3_ref[...] * cw[0:1, :] + x2_ref[...] * cw[1:2, :]) + x1_ref[...] * cw[2:3, :])
                        + xb * cw[3:4, :])
    ri = _bdot(xc, wri_ref[...])
    r = jax.nn.sigmoid(ri[:, :D_BRANCH] + br_ref[...])
    i = jax.nn.sigmoid(ri[:, D_BRANCH:] + bi_ref[...])
    log_a = LRU_C * r * jax.nn.log_sigmoid(lam_ref[...])
    a = jnp.exp(log_a)
    bb = jnp.sqrt(-jnp.tanh(log_a) * (a * a + 1.0)) * (i * xc)
    a_ref[...] = a
    b_ref[...] = bb
    for b in range(nb):
        r0 = b * lb
        b_ref[pl.ds(r0, 1), :] = bb[r0:r0 + 1, :] + a[r0:r0 + 1, :] * hc_ref[b]
    a = a_ref[...]
    bb = b_ref[...]
    t_idx = lax.broadcasted_iota(jnp.int32, (rows, D_BRANCH), 0) % lb
    s = 1
    while s < lb:
        keep = t_idx >= s
        a_sh = jnp.where(keep, pltpu.roll(a, s, axis=0), 1.0)
        b_sh = jnp.where(keep, pltpu.roll(bb, s, axis=0), 0.0)
        bb = a * b_sh + bb
        a = a * a_sh
        s *= 2
    h = bb
    for b in range(nb):
        r0 = b * lb
        hc_ref[b] = h[r0 + lb - 1:r0 + lb, :]
    res = (h * jax.nn.gelu(gb)).astype(o_ref.dtype)
    if three_d:
        res = res.reshape(nb, lb, D_BRANCH)
    o_ref[...] = res
    hout_ref[...] = hc_ref[...]


def _rglru(cx, conv0, h0, prm, batch, seq, nb, lb):
    nchunk = seq // lb
    rows = nb * lb
    three_d = nb > 1 and nchunk > 1
    if three_d:
        cx_in = cx.reshape(batch, seq, 2 * D_BRANCH)
        cx_spec = pl.BlockSpec((nb, lb, 2 * D_BRANCH), lambda i, c: (i, c, 0))
        o_shape = jax.ShapeDtypeStruct((batch, seq, D_BRANCH), BF16)
        o_spec = pl.BlockSpec((nb, lb, D_BRANCH), lambda i, c: (i, c, 0))
    else:
        cx_in = cx
        cx_spec = pl.BlockSpec((rows, 2 * D_BRANCH), lambda i, c: (i * nchunk + c, 0))
        o_shape = jax.ShapeDtypeStruct((batch * seq, D_BRANCH), BF16)
        o_spec = pl.BlockSpec((rows, D_BRANCH), lambda i, c: (i * nchunk + c, 0))

    def full(arr):
        nd = arr.ndim
        return pl.BlockSpec(arr.shape, lambda i, c: (0,) * nd)

    params = [prm[n] for n in ("cw", "cb", "wri", "br", "bi", "lam")]
    o, h_out = pl.pallas_call(
        functools.partial(_lru_kernel, nb, lb, three_d),
        out_shape=(o_shape, jax.ShapeDtypeStruct((batch, 1, D_BRANCH), F32)),
        grid=(batch // nb, nchunk),
        in_specs=[cx_spec,
                  pl.BlockSpec((nb, CONV_W - 1, D_BRANCH), lambda i, c: (i, 0, 0)),
                  pl.BlockSpec((nb, 1, D_BRANCH), lambda i, c: (i, 0, 0))]
                 + [full(p) for p in params],
        out_specs=(o_spec, pl.BlockSpec((nb, 1, D_BRANCH), lambda i, c: (i, 0, 0))),
        scratch_shapes=[pltpu.VMEM((nb, CONV_W - 1, D_BRANCH), F32),
                        pltpu.VMEM((nb, 1, D_BRANCH), F32)]
                       + [pltpu.VMEM((rows, D_BRANCH), F32) for _ in range(5)],
        compiler_params=_cparams(("arbitrary", "arbitrary")),
    )(cx_in, conv0, h0.reshape(batch, 1, D_BRANCH), *params)
    return o.reshape(batch * seq, D_BRANCH), h_out.reshape(batch, D_BRANCH)


def _merge_kernel(h_ref, oa_ref, ob_ref, oc_ref, x_ref, gt_ref, wgm_ref, wb_ref, wout_ref, o_ref):
    gm = _dot(h_ref[...], wgm_ref[...])
    merged = None
    for n, br_ref in enumerate((oa_ref, ob_ref, oc_ref)):
        br = _dot(br_ref[...], wb_ref[n])
        term = jax.nn.sigmoid(gm[:, n * D_MODEL:(n + 1) * D_MODEL]) * br
        merged = term if merged is None else merged + term
    y = _bdot(merged, wout_ref[...])
    o_ref[...] = x_ref[...] + gt_ref[...] * y


def _merge_out(h, oa, ob, oc, x, grp, w3, l, wb, wout, tm):
    row = lambda i: (i, 0)
    return pl.pallas_call(
        _merge_kernel,
        out_shape=jax.ShapeDtypeStruct((grp.rows, D_MODEL), F32),
        grid=(grp.rows // tm,),
        in_specs=[pl.BlockSpec((tm, D_MODEL), row),
                  pl.BlockSpec((tm, D_BRANCH), row),
                  pl.BlockSpec((tm, D_BRANCH), row),
                  pl.BlockSpec((tm, D_BRANCH), row),
                  pl.BlockSpec((tm, D_MODEL), row),
                  grp.mod_spec(2, tm),
                  pl.BlockSpec((None, D_MODEL, 3 * D_MODEL), lambda i: (l, 0, W_GM // (3 * D_MODEL))),
                  pl.BlockSpec(wb.shape, lambda i: (0, 0, 0)),
                  pl.BlockSpec(wout.shape, lambda i: (0, 0))],
        out_specs=pl.BlockSpec((tm, D_MODEL), row),
        compiler_params=_cparams(("arbitrary",)),
    )(h, oa, ob, oc, x, grp.mod4, w3, wb, wout)


def _group_member(x, k, lane):
    ge = EXPERTS_PER_GROUP
    pos = lane & (ge - 1)
    return jnp.where(pos + k < ge, pltpu.roll(x, LANES - k, axis=1), pltpu.roll(x, ge - k, axis=1))


def _in_group_top2(sel, lane):
    ge = EXPERTS_PER_GROUP
    pos = lane & (ge - 1)
    n_ahead = jnp.zeros(sel.shape, jnp.int32)
    for k in range(1, ge):
        other = _group_member(sel, k, lane)
        lower_index = pos + k >= ge
        n_ahead = n_ahead + ((other > sel) | ((other == sel) & lower_index)).astype(jnp.int32)
    return n_ahead < TOP_K


def _best_group(sel, in_top2, lane):
    ge = EXPERTS_PER_GROUP
    kept = jnp.where(in_top2, sel, 0.0)
    score = kept
    for k in range(1, ge):
        score = score + _group_member(kept, k, lane)
    best = lane < N_EXPERTS
    for d in range(1, N_GROUPS):
        later = pltpu.roll(score, LANES - ge * d, axis=1)
        earlier = pltpu.roll(score, ge * d, axis=1)
        best = best & (later <= score) & (earlier < score)
    return best


def _router_probs(h_bf16, wr_ref, rb_ref, lane):
    logits = _dot(h_bf16, wr_ref[...])
    logits = jnp.where(lane < N_EXPERTS, logits, -jnp.inf)
    mx = jnp.max(logits, axis=-1, keepdims=True)
    ex = jnp.exp(logits - mx)
    probs = ex / jnp.sum(ex, axis=-1, keepdims=True)
    sel = jnp.where(lane < N_EXPERTS, probs + rb_ref[...], -jnp.inf)
    return probs, sel


def _route(probs, sel, lane):
    in_top2 = _in_group_top2(sel, lane)
    pk = jnp.where(_best_group(sel, in_top2, lane) & in_top2, probs, 0.0)
    return pk / jnp.sum(pk, axis=-1, keepdims=True)


def _moe_kernel(final, x_ref, g_ref, sc_ref, sh_ref, gt_ref, gf_ref, wr_ref, rb_ref, wg_ref, wu_ref, wd_ref,
                o_ref, h_s, gate_s):
    e = pl.program_id(1)
    tm = x_ref.shape[0]
    lane = lax.broadcasted_iota(jnp.int32, (tm, LANES), 1)

    @pl.when(e == 0)
    def _():
        y = _rms(x_ref[...], g_ref[...])
        h = (y * (1.0 + sc_ref[...]) + sh_ref[...]).astype(BF16)
        h_s[...] = h
        probs, selv = _router_probs(h, wr_ref, rb_ref, lane)
        gate_s[...] = _route(probs, selv, lane)
        o_ref[...] = jnp.zeros_like(o_ref)

    h = h_s[...]
    gcol = jnp.sum(jnp.where(lane == e, gate_s[...], 0.0), axis=-1, keepdims=True)
    hg = _bdot(h, wg_ref[...])
    hu = _bdot(h, wu_ref[...])
    act = hg * jax.nn.sigmoid(hg) * hu * gcol
    o_ref[...] += _bdot(act, wd_ref[...])

    @pl.when(e == N_EXPERTS - 1)
    def _():
        res = x_ref[...] + gt_ref[...] * o_ref[...]
        o_ref[...] = _rms(res, gf_ref[...]) if final else res


def _moe(x, g, grp, wr, rb, wg, wu, wd, l, tm, gf=None):
    row = lambda i, e: (i, 0)
    final = gf is not None
    gain = (gf if final else g).reshape(1, D_MODEL)
    return pl.pallas_call(
        functools.partial(_moe_kernel, final),
        out_shape=jax.ShapeDtypeStruct((grp.rows, D_MODEL), F32),
        grid=(grp.rows // tm, N_EXPERTS),
        in_specs=[pl.BlockSpec((tm, D_MODEL), row),
                  pl.BlockSpec((1, D_MODEL), lambda i, e: (0, 0)),
                  grp.mod_spec(4, tm),
                  grp.mod_spec(3, tm),
                  grp.mod_spec(5, tm),
                  pl.BlockSpec((1, D_MODEL), lambda i, e: (0, 0)),
                  pl.BlockSpec((D_MODEL, LANES), lambda i, e: (0, 0)),
                  pl.BlockSpec((1, LANES), lambda i, e: (0, 0)),
                  pl.BlockSpec((None, None, D_MODEL, D_EXPERT), lambda i, e: (l, e, 0, 0)),
                  pl.BlockSpec((None, None, D_MODEL, D_EXPERT), lambda i, e: (l, e, 0, 0)),
                  pl.BlockSpec((None, None, D_EXPERT, D_MODEL), lambda i, e: (l, e, 0, 0))],
        out_specs=pl.BlockSpec((tm, D_MODEL), row),
        scratch_shapes=[pltpu.VMEM((tm, D_MODEL), BF16),
                        pltpu.VMEM((tm, LANES), F32)],
        compiler_params=_cparams(("arbitrary", "arbitrary"), MOE_VMEM_LIMIT),
    )(x, g.reshape(1, D_MODEL), grp.mod4, grp.mod4, grp.mod4, gain, wr, rb, wg, wu, wd)


MOE_SORT_TILE = 1024
MOE_DMA_ROWS = 2048


def _moe_router_kernel(x_ref, g_ref, sc_ref, sh_ref, wr_ref, rb_ref, tri_ref,
                       h_ref, gr_ref, cnt_ref, base_s):
    i = pl.program_id(0)
    tm = x_ref.shape[0]
    lane = lax.broadcasted_iota(jnp.int32, (tm, LANES), 1)

    @pl.when(i == 0)
    def _():
        base_s[...] = jnp.zeros_like(base_s)

    y = _rms(x_ref[...], g_ref[...])
    h = y * (1.0 + sc_ref[...]) + sh_ref[...]
    h_ref[...] = h
    probs, sel = _router_probs(h.astype(BF16), wr_ref, rb_ref, lane)
    best = _best_group(sel, _in_group_top2(sel, lane), lane)
    first = best & ((lane & (EXPERTS_PER_GROUP - 1)) == 0)
    gid = jnp.sum(jnp.where(first, lane >> 2, 0).astype(F32), axis=-1, keepdims=True)
    onehot = (lane.astype(F32) == gid).astype(BF16)
    incl = _dot(tri_ref[...], onehot) + base_s[...]
    rank = jnp.sum(jnp.where(lane.astype(F32) == gid, incl - 1.0, 0.0), axis=-1, keepdims=True)
    gr_ref[...] = jnp.where(lane == 0, gid, jnp.where(lane == 1, rank, 0.0)).astype(jnp.int32)
    base_s[...] = incl[tm - 1:tm, :]
    cnt_ref[...] = incl[tm - 1:tm, :].astype(jnp.int32)


def _scatter_rows_kernel(dest_ref, src_ref, zeros_hbm, dst_hbm, sem):
    del zeros_hbm
    base = pl.program_id(0) * MOE_DMA_ROWS

    def issue(t, carry):
        pltpu.make_async_copy(src_ref.at[pl.ds(t, 1)], dst_hbm.at[pl.ds(dest_ref[base + t], 1)], sem).start()
        return carry

    def drain(t, carry):
        pltpu.make_async_copy(src_ref.at[pl.ds(0, 1)], dst_hbm.at[pl.ds(0, 1)], sem).wait()
        return carry

    lax.fori_loop(0, MOE_DMA_ROWS, issue, 0, unroll=8)
    lax.fori_loop(0, MOE_DMA_ROWS, drain, 0, unroll=8)


def _moe_group_kernel(tg_ref, nv_ref, xs_ref, wr_ref, rb_ref, wg_ref, wu_ref, wd_ref, ys_ref, h_s, gate_s):
    i = pl.program_id(0)
    j = pl.program_id(1)
    tm = xs_ref.shape[0]
    lane = lax.broadcasted_iota(jnp.int32, (tm, LANES), 1)
    grp = tg_ref[i]

    @pl.when((i >= nv_ref[0]) & (j == 0))
    def _():
        ys_ref[...] = jnp.zeros_like(ys_ref)

    @pl.when(i < nv_ref[0])
    def _():
        @pl.when(j == 0)
        def _():
            h = xs_ref[...].astype(BF16)
            h_s[...] = h
            probs, sel = _router_probs(h, wr_ref, rb_ref, lane)
            pk = jnp.where(_in_group_top2(sel, lane) & ((lane >> 2) == grp), probs, 0.0)
            psum = jnp.sum(pk, axis=-1, keepdims=True)
            gate_s[...] = pk / jnp.where(psum > 0.0, psum, 1.0)
            ys_ref[...] = jnp.zeros_like(ys_ref)

        h = h_s[...]
        gcol = jnp.sum(jnp.where(lane == grp * EXPERTS_PER_GROUP + j, gate_s[...], 0.0), axis=-1, keepdims=True)
        hg = _bdot(h, wg_ref[...])
        hu = _bdot(h, wu_ref[...])
        act = hg * jax.nn.sigmoid(hg) * hu * gcol
        ys_ref[...] += _bdot(act, wd_ref[...])


def _gather_residual_kernel(final, dest_ref, x_ref, gt_ref, gf_ref, ys_hbm, o_ref, buf, sem):
    i = pl.program_id(0)
    n = pl.num_programs(0)
    tm = x_ref.shape[0]

    def start_tile(tile, slot):
        def issue(t, carry):
            pltpu.make_async_copy(ys_hbm.at[pl.ds(dest_ref[tile * tm + t], 1)], buf.at[slot, pl.ds(t, 1)],
                                  sem.at[slot]).start()
            return carry
        lax.fori_loop(0, tm, issue, 0, unroll=8)

    @pl.when(i == 0)
    def _():
        start_tile(0, 0)

    @pl.when(i + 1 < n)
    def _():
        start_tile(i + 1, (i + 1) % 2)

    slot = i % 2

    def drain(t, carry):
        pltpu.make_async_copy(ys_hbm.at[pl.ds(0, 1)], buf.at[slot, pl.ds(0, 1)], sem.at[slot]).wait()
        return carry

    lax.fori_loop(0, tm, drain, 0, unroll=8)
    res = x_ref[...] + gt_ref[...] * buf[slot]
    if final:
        res = _rms(res, gf_ref[...])
    o_ref[...] = res


def _moe_sorted(x, g, grp, wr, rb, wg, wu, wd, l, tm, gf=None):
    rows = grp.rows
    te = MOE_SORT_TILE
    ntile = rows // te + N_GROUPS
    cap = ntile * te
    tri = jnp.asarray(np.tril(np.ones((tm, tm), np.float32)), BF16)
    row = lambda i: (i, 0)
    h, gr, cnt = pl.pallas_call(
        _moe_router_kernel,
        out_shape=(jax.ShapeDtypeStruct((rows, D_MODEL), F32),
                   jax.ShapeDtypeStruct((rows, LANES), jnp.int32),
                   jax.ShapeDtypeStruct((1, LANES), jnp.int32)),
        grid=(rows // tm,),
        in_specs=[pl.BlockSpec((tm, D_MODEL), row),
                  pl.BlockSpec((1, D_MODEL), lambda i: (0, 0)),
                  grp.mod_spec(4, tm),
                  grp.mod_spec(3, tm),
                  pl.BlockSpec((D_MODEL, LANES), lambda i: (0, 0)),
                  pl.BlockSpec((1, LANES), lambda i: (0, 0)),
                  pl.BlockSpec((tm, tm), lambda i: (0, 0))],
        out_specs=(pl.BlockSpec((tm, D_MODEL), row),
                   pl.BlockSpec((tm, LANES), row),
                   pl.BlockSpec((1, LANES), lambda i: (0, 0))),
        scratch_shapes=[pltpu.VMEM((1, LANES), F32)],
        compiler_params=_cparams(("arbitrary",)),
    )(x, g.reshape(1, D_MODEL), grp.mod4, grp.mod4, wr, rb, tri)

    counts = cnt[0, :N_GROUPS]
    padded = ((counts + te - 1) // te) * te
    ends = jnp.cumsum(padded)
    dest = (ends - padded)[gr[:, 0]] + gr[:, 1]
    tile_group = jnp.minimum(jnp.searchsorted(ends, jnp.arange(ntile, dtype=jnp.int32) * te, side="right"),
                             N_GROUPS - 1).astype(jnp.int32)
    n_valid = (ends[N_GROUPS - 1] // te).astype(jnp.int32).reshape(1)

    xs = pl.pallas_call(
        _scatter_rows_kernel,
        out_shape=jax.ShapeDtypeStruct((cap, D_MODEL), F32),
        grid_spec=pltpu.PrefetchScalarGridSpec(
            num_scalar_prefetch=1,
            grid=(rows // MOE_DMA_ROWS,),
            in_specs=[pl.BlockSpec((MOE_DMA_ROWS, D_MODEL), lambda i, d: (i, 0)),
                      pl.BlockSpec(memory_space=pl.ANY)],
            out_specs=pl.BlockSpec(memory_space=pl.ANY),
            scratch_shapes=[pltpu.SemaphoreType.DMA(())]),
        input_output_aliases={2: 0},
        compiler_params=_cparams(("arbitrary",)),
    )(dest, h, jnp.zeros((cap, D_MODEL), F32))

    ys = pl.pallas_call(
        _moe_group_kernel,
        out_shape=jax.ShapeDtypeStruct((cap, D_MODEL), F32),
        grid_spec=pltpu.PrefetchScalarGridSpec(
            num_scalar_prefetch=2,
            grid=(ntile, EXPERTS_PER_GROUP),
            in_specs=[pl.BlockSpec((te, D_MODEL), lambda i, j, tg, nv: (i, 0)),
                      pl.BlockSpec((D_MODEL, LANES), lambda i, j, tg, nv: (0, 0)),
                      pl.BlockSpec((1, LANES), lambda i, j, tg, nv: (0, 0)),
                      pl.BlockSpec((None, None, D_MODEL, D_EXPERT),
                                   lambda i, j, tg, nv: (l, tg[i] * EXPERTS_PER_GROUP + j, 0, 0)),
                      pl.BlockSpec((None, None, D_MODEL, D_EXPERT),
                                   lambda i, j, tg, nv: (l, tg[i] * EXPERTS_PER_GROUP + j, 0, 0)),
                      pl.BlockSpec((None, None, D_EXPERT, D_MODEL),
                                   lambda i, j, tg, nv: (l, tg[i] * EXPERTS_PER_GROUP + j, 0, 0))],
            out_specs=pl.BlockSpec((te, D_MODEL), lambda i, j, tg, nv: (i, 0)),
            scratch_shapes=[pltpu.VMEM((te, D_MODEL), BF16), pltpu.VMEM((te, LANES), F32)]),
        compiler_params=_cparams(("arbitrary", "arbitrary"), MOE_VMEM_LIMIT),
    )(tile_group, n_valid, xs, wr, rb, wg, wu, wd)

    gt_spec = grp.mod_spec(5, tm)
    final = gf is not None
    gain = (gf if final else g).reshape(1, D_MODEL)
    return pl.pallas_call(
        functools.partial(_gather_residual_kernel, final),
        out_shape=jax.ShapeDtypeStruct((rows, D_MODEL), F32),
        grid_spec=pltpu.PrefetchScalarGridSpec(
            num_scalar_prefetch=1,
            grid=(rows // tm,),
            in_specs=[pl.BlockSpec((tm, D_MODEL), lambda i, d: (i, 0)),
                      pl.BlockSpec(gt_spec.block_shape, lambda i, d: gt_spec.index_map(i)),
                      pl.BlockSpec((1, D_MODEL), lambda i, d: (0, 0)),
                      pl.BlockSpec(memory_space=pl.ANY)],
            out_specs=pl.BlockSpec((tm, D_MODEL), lambda i, d: (i, 0)),
            scratch_shapes=[pltpu.VMEM((2, tm, D_MODEL), F32), pltpu.SemaphoreType.DMA((2,))]),
        compiler_params=_cparams(("arbitrary",)),
    )(dest, x, grp.mod4, gain, ys)


def _block_diag(w):
    eye = jnp.eye(C_BLOCKS, dtype=w.dtype)
    return jnp.einsum("hg,hij->higj", eye, w).reshape(D_BRANCH, D_BRANCH)


def _layer_params(l, p):
    row = lambda a: a[l].reshape(1, -1)
    ones = jnp.asarray(np.kron(np.eye(4), np.ones((A_HEAD, A_HEAD))), BF16)
    pad_lo = lambda w: jnp.pad(w, ((0, 64), (0, 0)))
    pad_hi = lambda w: jnp.pad(w, ((64, 0), (0, 0)))
    rw = dict(mu=row(p["a_mu"]), w0=row(p["a_w0"]), w2=pad_lo(p["a_w2"][l]).astype(BF16),
              a0=row(p["a_a0"]), a2=pad_hi(p["a_a2"][l]).astype(BF16), g2=p["a_g2"][l].astype(BF16),
              kk=row(p["a_kk"]), ka=row(p["a_ka"]), rk=row(p["a_rk"]), lng=row(p["a_ln_g"]),
              lnb=row(p["a_ln_b"]), ones=ones)
    lru = dict(cw=p["c_conv_w"][l], cb=row(p["c_conv_b"]),
               wri=jnp.concatenate([_block_diag(p["c_wr"][l]), _block_diag(p["c_wi"][l])], axis=1).astype(BF16),
               br=row(p["c_br"]), bi=row(p["c_bi"]), lam=row(p["c_lam"]))
    return dict(
        rw=rw, lru=lru,
        w_branch=p["w_branch"][l].astype(BF16),
        w_out=p["w_out"][l].astype(BF16),
        wg=p["moe_wg"], wu=p["moe_wu"], wd=p["moe_wd"],
        norm_mix=p["norm_mix"][l], norm_ffn=p["norm_ffn"][l])


def kernel(x_prompt, x_sample, c_prompt, c_sample, state_rwkv_shift, state_rwkv_wkv, state_ret, state_lru_h, state_lru_conv, norm_mix, norm_ffn, norm_final, ada_w, ada_b, w_in, a_mu, a_w0, a_w2, a_a0, a_a2, a_g2, a_kk, a_ka, a_rk, a_ln_g, a_ln_b, c_conv_w, c_conv_b, c_wr, c_br, c_wi, c_bi, c_lam, w_branch, w_out, w_router, router_bias, moe_wg, moe_wu, moe_wd):
    p = dict(norm_mix=norm_mix, norm_ffn=norm_ffn, w_in=w_in, a_mu=a_mu, a_w0=a_w0, a_w2=a_w2,
             a_a0=a_a0, a_a2=a_a2, a_g2=a_g2, a_kk=a_kk, a_ka=a_ka, a_rk=a_rk, a_ln_g=a_ln_g,
             a_ln_b=a_ln_b, c_conv_w=c_conv_w, c_conv_b=c_conv_b, c_wr=c_wr, c_br=c_br, c_wi=c_wi,
             c_bi=c_bi, c_lam=c_lam, w_branch=w_branch, w_out=w_out, moe_wg=moe_wg, moe_wu=moe_wu,
             moe_wd=moe_wd)
    bp, lp_, _ = x_prompt.shape
    bs, ls, _ = x_sample.shape
    layers = [_layer_params(l, p) for l in range(DEPTH)]
    w3 = _arrange_w_in(w_in)
    router = (jnp.pad(w_router, ((0, 0), (0, LANES - N_EXPERTS))).astype(BF16),
              jnp.pad(router_bias, (0, LANES - N_EXPERTS)).reshape(1, LANES))

    n_c = bp + bs
    pad_c = (-n_c) % 16
    c_all = jnp.pad(jnp.concatenate([c_prompt, c_sample], axis=0), ((0, pad_c), (0, 0)))
    mods = _ada(c_all, ada_w, ada_b)

    def run(x, batch, seq, mod, states, pos0, cfg):
        xs = x.reshape(batch * seq, D_MODEL)
        grps = []
        for l in range(DEPTH):
            if cfg["per_token"]:
                m = jnp.repeat(mod[l], seq, axis=0).reshape(batch * seq, 6, D_MODEL)
                m4 = m.transpose(1, 0, 2)[None]
            else:
                m4 = mod[l].reshape(batch, 6, 1, D_MODEL)
            grps.append(_Group(batch, seq, m4))
        return _trunk_layers(xs, grps, states, layers, w3, router, norm_final, pos0, cfg)

    zeros = lambda s: jnp.zeros((DEPTH, bp) + s.shape[2:], x_prompt.dtype)
    st_prompt = (zeros(state_rwkv_shift), zeros(state_rwkv_wkv), zeros(state_ret),
                 zeros(state_lru_h), zeros(state_lru_conv))
    st_sample = (state_rwkv_shift, state_rwkv_wkv, state_ret, state_lru_h, state_lru_conv)
    cfg_p = dict(per_token=False, tm=min(512, lp_), tm_mg=min(256, lp_), tm_mm=min(1024, lp_),
                 tm_moe=min(1024, lp_), moe_sorted=(bp * lp_) % MOE_DMA_ROWS == 0, rwkv_nb=8, lru_nb=1, lru_lb=min(256, lp_))
    cfg_s = dict(per_token=True, tm=min(512, bs * ls), tm_mg=min(256, bs * ls), tm_mm=min(512, bs * ls),
                 tm_moe=min(512, bs * ls), moe_sorted=False, rwkv_nb=16, lru_nb=16, lru_lb=ls)
    y_p, new_p = run(x_prompt, bp, lp_, mods[:, :bp], st_prompt, 0, cfg_p)
    y_s, new_s = run(x_sample, bs, ls, mods[:, bp:bp + bs], st_sample, PAST_LEN, cfg_s)
    return (y_p, y_s) + new_p + new_s


def _trunk_layers(x, grps, states, layers, w3, router, norm_final, pos0, cfg):
    batch, seq = grps[0].batch, grps[0].seq
    tabs = _ret_tables(seq, pos0)
    wr, rb = router
    outs = [[] for _ in range(5)]
    for l, lp in enumerate(layers):
        grp = grps[l]
        h = _norm_mod(x, lp["norm_mix"], grp, 1, 0, cfg["tm"])
        pa = _matmul(h, w3, l, W_PA, W_PA_PAD, cfg["tm_mm"], 1024)
        rq = _matmul(h, w3, l, W_RQ, 4 * D_BRANCH, cfg["tm_mm"], 1024)
        cx = _matmul(h, w3, l, W_CX, 2 * D_BRANCH, cfg["tm_mm"], 1024)
        if seq % RW_CHUNK == 0:
            o_a, wkv = _rwkv_chunked(pa, states[0][l], states[1][l], lp["rw"], batch, seq,
                                     min(4, seq // RW_CHUNK))
        else:
            o_a, wkv = _rwkv_steps(pa, states[0][l], states[1][l], lp["rw"], batch, seq, cfg["rwkv_nb"])
        o_b, ret = _retention(rq, states[2], l, tabs, batch, seq)
        o_c, lru_h = _rglru(cx, states[4][l], states[3][l], lp["lru"], batch, seq,
                            cfg["lru_nb"], cfg["lru_lb"])
        x = _merge_out(h, o_a, o_b, o_c, x, grp, w3, l, lp["w_branch"], lp["w_out"], cfg["tm_mg"])
        moe = _moe_sorted if cfg["moe_sorted"] else _moe
        x = moe(x, lp["norm_ffn"], grp, wr, rb, lp["wg"], lp["wu"], lp["wd"], l, cfg["tm_moe"],
                gf=norm_final if l == DEPTH - 1 else None)
        outs[0].append(pa.reshape(batch, seq, W_PA_PAD)[:, -1, :A_PROJ])
        outs[1].append(wkv)
        outs[2].append(ret)
        outs[3].append(lru_h)
        outs[4].append(cx.reshape(batch, seq, 2 * D_BRANCH)[:, seq - (CONV_W - 1):, :D_BRANCH])
    return x.reshape(batch, seq, D_MODEL), tuple(jnp.stack(o) for o in outs)
```

```python
import functools
import math

import numpy as np
import jax
import jax.numpy as jnp
from jax import lax
from jax.experimental import pallas as pl
from jax.experimental.pallas import tpu as pltpu

F32 = jnp.float32
BF16 = jnp.bfloat16

D_MODEL = 1024
DEPTH = 2
PAST_LEN = 16384
D_BRANCH = 512
A_HEAD = 64
A_HEADS = 8
A_PROJ = 1792
A_NORM_EPS = 64e-5
A_KK_EPS = 1e-12
R_HEAD = 128
R_HEADS = 4
R_CHUNK = 128
R_NORM_EPS = 1e-6
ROPE_BASE = 10000.0
C_BLOCK = 64
C_BLOCKS = 8
CONV_W = 4
LRU_C = 8.0
N_EXPERTS = 16
N_GROUPS = 4
EXPERTS_PER_GROUP = 4
TOP_K = 2
D_EXPERT = 512
NORM_EPS = 1e-6

LANES = 128
RET_ROWS = 128
RET_CHUNKS_PER_STEP = 4
VMEM_LIMIT = 48 * 1024 * 1024
MOE_VMEM_LIMIT = 56 * 1024 * 1024


def _cparams(sem, vmem=VMEM_LIMIT):
    return pltpu.CompilerParams(dimension_semantics=sem, vmem_limit_bytes=vmem)


def _dot(a, b):
    return jnp.dot(a, b, preferred_element_type=F32)


def _bdot(a, b):
    return jnp.dot(a.astype(BF16), b.astype(BF16), preferred_element_type=F32)


def _split3(x):
    hi = x.astype(BF16)
    r1 = x - hi.astype(F32)
    mid = r1.astype(BF16)
    lo = (r1 - mid.astype(F32)).astype(BF16)
    return hi, mid, lo


def _segsum(x, ones_bf16):
    w = ones_bf16.shape[0]
    hi, mid, lo = _split3(x)
    parts = []
    for j in range(x.shape[1] // w):
        c = slice(j * w, (j + 1) * w)
        parts.append((_dot(lo[:, c], ones_bf16) + _dot(mid[:, c], ones_bf16)) + _dot(hi[:, c], ones_bf16))
    return jnp.concatenate(parts, axis=1)


def _ada_kernel(c_ref, w_ref, b_ref, o_ref):
    c = c_ref[...]
    s = c * jax.nn.sigmoid(c)
    o_ref[...] = _bdot(s, w_ref[...]) + b_ref[...]


def _ada(c, ada_w, ada_b):
    rows = c.shape[0]
    tn = 1536
    return pl.pallas_call(
        _ada_kernel,
        out_shape=jax.ShapeDtypeStruct((DEPTH, rows, 6 * D_MODEL), F32),
        grid=(DEPTH, 6 * D_MODEL // tn),
        in_specs=[
            pl.BlockSpec((rows, D_MODEL), lambda l, j: (0, 0)),
            pl.BlockSpec((None, D_MODEL, tn), lambda l, j: (l, 0, j)),
            pl.BlockSpec((None, 1, tn), lambda l, j: (l, 0, j)),
        ],
        out_specs=pl.BlockSpec((None, rows, tn), lambda l, j: (l, 0, j)),
        compiler_params=_cparams(("arbitrary", "arbitrary")),
    )(c, ada_w, ada_b.reshape(DEPTH, 1, 6 * D_MODEL))


class _Group:
    def __init__(self, batch, seq, mod4):
        self.batch = batch
        self.seq = seq
        self.rows = batch * seq
        self.mod4 = mod4
        self.per_token = mod4.shape[0] == 1 and mod4.shape[2] != 1

    def mod_spec(self, k, tm):
        if self.per_token:
            return pl.BlockSpec((None, None, tm, D_MODEL), lambda i, *_: (0, k, i, 0))
        seq = self.seq
        return pl.BlockSpec((None, None, 1, D_MODEL), lambda i, *_: ((i * tm) // seq, k, 0, 0))


def _rms(x, g):
    return x * lax.rsqrt(jnp.mean(x * x, axis=-1, keepdims=True) + NORM_EPS) * g


def _norm_mod_kernel(x_ref, g_ref, sc_ref, sh_ref, o_ref):
    y = _rms(x_ref[...], g_ref[...])
    o_ref[...] = (y * (1.0 + sc_ref[...]) + sh_ref[...]).astype(o_ref.dtype)


def _norm_mod(x, g, grp, k_sc, k_sh, tm):
    return pl.pallas_call(
        _norm_mod_kernel,
        out_shape=jax.ShapeDtypeStruct((grp.rows, D_MODEL), BF16),
        grid=(grp.rows // tm,),
        in_specs=[
            pl.BlockSpec((tm, D_MODEL), lambda i: (i, 0)),
            pl.BlockSpec((1, D_MODEL), lambda i: (0, 0)),
            grp.mod_spec(k_sc, tm),
            grp.mod_spec(k_sh, tm),
        ],
        out_specs=pl.BlockSpec((tm, D_MODEL), lambda i: (i, 0)),
        compiler_params=_cparams(("arbitrary",)),
    )(x, g.reshape(1, D_MODEL), grp.mod4, grp.mod4)


def _mm_kernel(a_ref, w_ref, o_ref):
    o_ref[...] = _dot(a_ref[...], w_ref[...])


def _matmul(a, w3, l, col0, n, tm, tn):
    rows, k = a.shape
    c0 = col0 // tn
    return pl.pallas_call(
        _mm_kernel,
        out_shape=jax.ShapeDtypeStruct((rows, n), F32),
        grid=(n // tn, rows // tm),
        in_specs=[
            pl.BlockSpec((tm, k), lambda j, i: (i, 0)),
            pl.BlockSpec((None, k, tn), lambda j, i: (l, 0, c0 + j)),
        ],
        out_specs=pl.BlockSpec((tm, tn), lambda j, i: (i, j)),
        compiler_params=_cparams(("arbitrary", "arbitrary")),
    )(a, w3)


W_RQ, W_CX, W_GM, W_PA, W_PA_PAD = 0, 2048, 3072, 6144, 2048


W_BLK = 256


def _arrange_kernel(src_ref, w_ref, o_ref):
    del src_ref
    j = pl.program_id(1)
    real = (W_PA + A_PROJ) // W_BLK

    @pl.when(j < real)
    def _():
        o_ref[...] = w_ref[...].astype(BF16)

    @pl.when(j >= real)
    def _():
        o_ref[...] = jnp.zeros_like(o_ref)


def _arrange_w_in(w_in):
    depth, k, n = w_in.shape
    nblk = n // W_BLK
    first = A_PROJ // W_BLK
    src = list(range(first, nblk)) + list(range(first))
    nout = (W_PA + W_PA_PAD) // W_BLK
    src = jnp.asarray(src + [0] * (nout - len(src)), jnp.int32)
    return pl.pallas_call(
        _arrange_kernel,
        out_shape=jax.ShapeDtypeStruct((depth, k, nout * W_BLK), BF16),
        grid_spec=pltpu.PrefetchScalarGridSpec(
            num_scalar_prefetch=1,
            grid=(depth, nout),
            in_specs=[pl.BlockSpec((None, k, W_BLK), lambda l, j, s: (l, 0, s[j]))],
            out_specs=pl.BlockSpec((None, k, W_BLK), lambda l, j, s: (l, 0, j))),
        compiler_params=_cparams(("arbitrary", "arbitrary")),
    )(src, w_in)


def _rwkv_pre(pa, prev, mu_ref, w0_ref, w2_ref, a0_ref, a2_ref, g2_ref, kk_ref, ka_ref, ones):
    pm = pa + (prev - pa) * mu_ref[...]
    r = pm[:, 0:512]
    k = pm[:, 512:1024]
    v = pm[:, 1024:1536]
    xwa = pm[:, 1536:1664]
    xg = pm[:, 1664:1792]
    w_log = -jax.nn.softplus(-(w0_ref[...] + _bdot(jnp.tanh(xwa), w2_ref[...]))) - 0.5
    logw = -jnp.exp(w_log)
    a = jax.nn.sigmoid(a0_ref[...] + _bdot(xwa, a2_ref[...]))
    g = _bdot(jax.nn.sigmoid(xg), g2_ref[...])
    kk = k * kk_ref[...]
    kk = kk * lax.rsqrt(_segsum(kk * kk, ones) + A_KK_EPS)
    k2 = k * (1.0 + (a - 1.0) * ka_ref[...])
    return r, logw, k2, v, -kk, kk * a, g


def _rwkv_post(o, r, k2, v, g, rk_ref, lng_ref, lnb_ref, ones):
    mean = _segsum(o, ones) * (1.0 / A_HEAD)
    oc = o - mean
    var = _segsum(oc * oc, ones) * (1.0 / A_HEAD)
    o = oc * lax.rsqrt(var + A_NORM_EPS) * lng_ref[...] + lnb_ref[...]
    bonus = _segsum(r * k2 * rk_ref[...], ones) * v
    return (o + bonus) * g


_NN = (((1,), (0,)), ((), ()))
_NT = (((1,), (1,)), ((), ()))
_TN = (((0,), (0,)), ((), ()))
RW_CHUNK = 64
RW_PAIRS = A_HEADS // 2


def _bdg(a, b, dims):
    return lax.dot_general(a.astype(BF16), b.astype(BF16), dims, preferred_element_type=F32)


def _rwkv_chunk_kernel(nck,
                       pa_ref, sh0_ref, g0_ref, mu_ref, w0_ref, w2_ref, a0_ref, a2_ref, g2_ref,
                       kk_ref, ka_ref, rk_ref, lng_ref, lnb_ref, ones_ref, tri_ref,
                       o_ref, gout_ref,
                       carry_ref, prev_ref, g_s, wu_s, m_s, zy_s, o_s):
    c = pl.program_id(1)
    ck = RW_CHUNK
    ones = ones_ref[...]

    @pl.when(c == 0)
    def _():
        carry_ref[...] = sh0_ref[...]
        g_s[...] = g0_ref[...]

    pa = pa_ref[...]
    rows = pa.shape[0]
    prev_ref[...] = pltpu.roll(pa, 1, axis=0)
    prev_ref[pl.ds(0, 1), :] = carry_ref[...]
    carry_ref[...] = pa[rows - 1:rows, :]
    r, logw, k2, v, an, bn, g = _rwkv_pre(pa, prev_ref[...], mu_ref, w0_ref, w2_ref, a0_ref, a2_ref,
                                          g2_ref, kk_ref, ka_ref, ones)

    tri = tri_ref[...]
    hi, mid, lo = _split3(logw)
    cum =(_dot(tri, lo) + _dot(tri, mid)) + _dot(tri, hi)
    cum_last = jnp.concatenate(
        [jnp.broadcast_to(cum[(i + 1) * ck - 1:(i + 1) * ck, :], (ck, D_BRANCH)) for i in range(nck)], axis=0)
    gam = jnp.exp(cum)
    inv = jnp.exp(-cum)
    to_end = jnp.exp(cum_last - cum)
    a_t = an * jnp.exp(cum - logw)
    b_t = bn * inv
    k_t = k2 * inv
    r_t = r * gam
    b_e = bn * to_end
    k_e = k2 * to_end

    lane = lax.broadcasted_iota(jnp.int32, (ck, LANES), 1)
    rowi = lax.broadcasted_iota(jnp.int32, (ck, LANES), 0)
    m0 = lane < A_HEAD
    coli = lane & (A_HEAD - 1)
    strict = rowi > coli
    incl = rowi >= coli
    eye_p = (rowi == coli).astype(F32)
    r128 = lax.broadcasted_iota(jnp.int32, (LANES, LANES), 0)
    c128 = lax.broadcasted_iota(jnp.int32, (LANES, LANES), 1)
    blockmask = (r128 < A_HEAD) == (c128 < A_HEAD)
    eye128 = r128 == c128
    bk_t = jnp.concatenate([b_e, k_e], axis=1).T

    def bd(q):
        q = q.astype(BF16)
        z = jnp.zeros_like(q)
        return jnp.concatenate([jnp.where(m0, q, z), jnp.where(m0, z, q)], axis=0)

    probs = [(i, p) for i in range(nck) for p in range(RW_PAIRS)]
    sl = {(i, p): (slice(i * ck, (i + 1) * ck), slice(p * LANES, (p + 1) * LANES)) for i, p in probs}
    l_pow, l_ak, t_inv = {}, {}, {}
    for q in probs:
        rs, ls = sl[q]
        lhs = jnp.concatenate([a_t[rs, ls], r_t[rs, ls]], axis=0).astype(BF16)
        ab = lax.dot_general(lhs, bd(b_t[rs, ls]), _NT, preferred_element_type=F32)
        ak = lax.dot_general(lhs, bd(k_t[rs, ls]), _NT, preferred_element_type=F32)
        l_pow[q] = jnp.where(strict, ab[:ck], 0.0)
        l_ak[q] = jnp.where(strict, ak[:ck], 0.0)
        m_s[q[0], q[1], :, 0:LANES] = jnp.where(incl, ab[ck:], 0.0)
        m_s[q[0], q[1], :, LANES:2 * LANES] = jnp.where(incl, ak[ck:], 0.0)
        t_inv[q] = eye_p + l_pow[q]
    n = 1
    while 2 * n < ck:
        for q in probs:
            l_pow[q] = _bdg(l_pow[q], bd(l_pow[q]), _NN)
        for q in probs:
            t_inv[q] = t_inv[q] + _bdg(t_inv[q], bd(l_pow[q]), _NN)
        n *= 2
    lak_v = {}
    for q in probs:
        rs, ls = sl[q]
        lak_v[q] = _bdg(l_ak[q], bd(v[rs, ls]), _NN)
    for q in probs:
        rs, ls = sl[q]
        wu = _bdg(t_inv[q], jnp.concatenate([bd(a_t[rs, ls]), bd(lak_v[q])], axis=1), _NN)
        wu_s[q[0], q[1]] = wu
        i, p = q
        half = jnp.zeros((ck, LANES), F32)
        place = (lambda x: jnp.concatenate([x, half], axis=0)) if i % 2 == 0 else \
                (lambda x: jnp.concatenate([half, x], axis=0))
        tcols = slice((i // 2) * LANES, (i // 2 + 1) * LANES)
        b_tr = bk_t[p * LANES:(p + 1) * LANES, tcols]
        k_tr = bk_t[D_BRANCH + p * LANES:D_BRANCH + (p + 1) * LANES, tcols]
        z_t = _bdg(b_tr, place(wu[:, 0:LANES]), _NN)
        y_t = _bdg(jnp.concatenate([b_tr, k_tr], axis=1),
                   jnp.concatenate([place(wu[:, LANES:2 * LANES]), place(v[rs, ls])], axis=0), _NN)
        zy_s[i, p, :, 0:LANES] = jnp.where(blockmask, z_t, 0.0)
        zy_s[i, p, :, LANES:2 * LANES] = jnp.where(blockmask, y_t, 0.0)

    for i in range(nck):
        rs = slice(i * ck, (i + 1) * ck)
        for p in range(RW_PAIRS):
            ls = slice(p * LANES, (p + 1) * LANES)
            h_p = g_s[p]
            h_bf = h_p.astype(BF16)
            gcol = jnp.sum(jnp.where(eye128, gam[(i + 1) * ck - 1:(i + 1) * ck, ls], 0.0), axis=1, keepdims=True)
            g_s[p] = (h_p * gcol + _dot(zy_s[i, p, :, 0:LANES].astype(BF16), h_bf)) + zy_s[i, p, :, LANES:2 * LANES]
            wu = wu_s[i, p]
            u = _dot(wu[:, 0:LANES].astype(BF16), h_bf) + wu[:, LANES:2 * LANES]
            o_s[rs, ls] = _dot(r_t[rs, ls].astype(BF16), h_bf) \
                + _bdg(m_s[i, p], jnp.concatenate([bd(u), bd(v[rs, ls])], axis=0), _NN)

    o_ref[...] = _rwkv_post(o_s[...], r, k2, v, g, rk_ref, lng_ref, lnb_ref, ones).astype(o_ref.dtype)
    gout_ref[...] = g_s[...]


def _rwkv_chunked(pa, shift0, s0, prm, batch, seq, nck):
    lb = nck * RW_CHUNK
    nstep = seq // lb
    s0p = s0.reshape(batch, RW_PAIRS, 2, A_HEAD, A_HEAD)
    eye2 = jnp.eye(2, dtype=s0.dtype)
    g0 = jnp.einsum("bpjvk,ji->bpjkiv", s0p, eye2).reshape(batch, RW_PAIRS, LANES, LANES)
    tri = np.kron(np.eye(nck), np.tril(np.ones((RW_CHUNK, RW_CHUNK)))).astype(np.float32)
    params = [prm[n] for n in ("mu", "w0", "w2", "a0", "a2", "g2", "kk", "ka", "rk", "lng", "lnb", "ones")]
    params.append(jnp.asarray(tri, BF16))

    def full(arr):
        nd = arr.ndim
        return pl.BlockSpec(arr.shape, lambda i, c: (0,) * nd)

    o, g_out = pl.pallas_call(
        functools.partial(_rwkv_chunk_kernel, nck),
        out_shape=(jax.ShapeDtypeStruct((batch * seq, D_BRANCH), BF16),
                   jax.ShapeDtypeStruct((batch, RW_PAIRS, LANES, LANES), F32)),
        grid=(batch, nstep),
        in_specs=[pl.BlockSpec((lb, A_PROJ), lambda i, c: (i * nstep + c, 0)),
                  pl.BlockSpec((None, 1, A_PROJ), lambda i, c: (i, 0, 0)),
                  pl.BlockSpec((None, RW_PAIRS, LANES, LANES), lambda i, c: (i, 0, 0, 0))]
                 + [full(p) for p in params],
        out_specs=(pl.BlockSpec((lb, D_BRANCH), lambda i, c: (i * nstep + c, 0)),
                   pl.BlockSpec((None, RW_PAIRS, LANES, LANES), lambda i, c: (i, 0, 0, 0))),
        scratch_shapes=[pltpu.VMEM((1, A_PROJ), F32),
                        pltpu.VMEM((lb, A_PROJ), F32),
                        pltpu.VMEM((RW_PAIRS, LANES, LANES), F32),
                        pltpu.VMEM((nck, RW_PAIRS, RW_CHUNK, 2 * LANES), F32),
                        pltpu.VMEM((nck, RW_PAIRS, RW_CHUNK, 2 * LANES), F32),
                        pltpu.VMEM((nck, RW_PAIRS, LANES, 2 * LANES), F32),
                        pltpu.VMEM((lb, D_BRANCH), F32)],
        compiler_params=_cparams(("arbitrary", "arbitrary")),
    )(pa, shift0.reshape(batch, 1, A_PROJ), g0, *params)
    g5 = g_out.reshape(batch, RW_PAIRS, 2, A_HEAD, 2, A_HEAD)
    s_new = jnp.stack([g5[:, :, 0, :, 0, :], g5[:, :, 1, :, 1, :]], axis=2).swapaxes(-1, -2)
    return o, s_new.reshape(batch, A_HEADS, A_HEAD, A_HEAD)


def _rwkv_step_kernel(nb, lb,
                      pa_ref, sh0_ref, s0_ref, mu_ref, w0_ref, w2_ref, a0_ref, a2_ref, g2_ref,
                      kk_ref, ka_ref, rk_ref, lng_ref, lnb_ref, ones_ref,
                      o_ref, sout_ref):
    ones = ones_ref[...]
    pa = pa_ref[...].reshape(lb * nb, A_PROJ)
    prev = jnp.concatenate([sh0_ref[...], pa[:(lb - 1) * nb, :]], axis=0)
    r, logw, k2, v, an, bn, g = _rwkv_pre(pa, prev, mu_ref, w0_ref, w2_ref, a0_ref, a2_ref,
                                          g2_ref, kk_ref, ka_ref, ones)
    w = jnp.exp(logw)
    srows = nb * A_HEAD
    rowi = lax.broadcasted_iota(jnp.int32, (srows, D_BRANCH), 0)
    lane = lax.broadcasted_iota(jnp.int32, (srows, D_BRANCH), 1)
    eye = (rowi & (A_HEAD - 1)) == (lane & (A_HEAD - 1))

    def per_seq(x, t):
        xt = x[t * nb:(t + 1) * nb, :]
        return jnp.concatenate([jnp.broadcast_to(xt[b:b + 1, :], (A_HEAD, D_BRANCH)) for b in range(nb)],
                               axis=0)

    s = s0_ref[...].reshape(srows, D_BRANCH)
    outs = []
    for t in range(lb):
        sa = _segsum(s * per_seq(an, t), ones)
        vcol = _segsum(jnp.where(eye, per_seq(v, t), 0.0), ones)
        s = s * per_seq(w, t) + sa * per_seq(bn, t) + vcol * per_seq(k2, t)
        out = _segsum(s * per_seq(r, t), ones)
        outs.append(jnp.sum(jnp.where(eye, out, 0.0).reshape(nb, A_HEAD, D_BRANCH), axis=1))
    o = jnp.concatenate(outs, axis=0)
    res = _rwkv_post(o, r, k2, v, g, rk_ref, lng_ref, lnb_ref, ones).astype(o_ref.dtype)
    o_ref[...] = res.reshape(lb, nb, D_BRANCH)
    sout_ref[...] = s.reshape(nb, A_HEAD, D_BRANCH)


def _rwkv_steps(pa, shift0, s0, prm, batch, seq, nb):
    pa_tm = pa.reshape(batch, seq, pa.shape[1]).transpose(1, 0, 2)
    s0k = s0.transpose(0, 2, 1, 3).reshape(batch, A_HEAD, D_BRANCH)

    def full(arr):
        nd = arr.ndim
        return pl.BlockSpec(arr.shape, lambda i: (0,) * nd)

    params = [prm[n] for n in ("mu", "w0", "w2", "a0", "a2", "g2", "kk", "ka", "rk", "lng", "lnb", "ones")]
    o, s_out = pl.pallas_call(
        functools.partial(_rwkv_step_kernel, nb, seq),
        out_shape=(jax.ShapeDtypeStruct((seq, batch, D_BRANCH), BF16),
                   jax.ShapeDtypeStruct((batch, A_HEAD, D_BRANCH), F32)),
        grid=(batch // nb,),
        in_specs=[pl.BlockSpec((seq, nb, A_PROJ), lambda i: (0, i, 0)),
                  pl.BlockSpec((nb, A_PROJ), lambda i: (i, 0)),
                  pl.BlockSpec((nb, A_HEAD, D_BRANCH), lambda i: (i, 0, 0))]
                 + [full(p) for p in params],
        out_specs=(pl.BlockSpec((seq, nb, D_BRANCH), lambda i: (0, i, 0)),
                   pl.BlockSpec((nb, A_HEAD, D_BRANCH), lambda i: (i, 0, 0))),
        compiler_params=_cparams(("arbitrary",)),
    )(pa_tm, shift0, s0k, *params)
    o = o.transpose(1, 0, 2).reshape(batch * seq, D_BRANCH)
    s_new = s_out.reshape(batch, A_HEAD, A_HEADS, A_HEAD).transpose(0, 2, 1, 3)
    return o, s_new


def _ret_kernel(nb, lb, nck,
                rq_ref, cos_ref, sin_ref, dm_ref, kd_ref, qd_ref, cd_ref, s0_ref,
                o_ref, sout_ref, s_ref):
    c = pl.program_id(1)

    @pl.when(c == 0)
    def _():
        s_ref[...] = s0_ref[...]

    scale = R_HEAD ** -0.5
    half = R_HEAD // 2
    row8 = lax.broadcasted_iota(jnp.int32, (8, R_HEAD), 0)
    pre = {}
    for ci in range(nck):
        rs = slice(ci * RET_ROWS, (ci + 1) * RET_ROWS)
        cos = cos_ref[rs, :]
        sin = sin_ref[rs, :]
        for h in range(R_HEADS):
            lo, hi = h * R_HEAD, (h + 1) * R_HEAD
            q = rq_ref[rs, lo:hi]
            k = rq_ref[rs, 512 + lo:512 + hi]
            v = rq_ref[rs, 1024 + lo:1024 + hi]
            qh = q * cos + pltpu.roll(q, half, axis=1) * sin
            kh = (k * cos + pltpu.roll(k, half, axis=1) * sin) * scale
            scores = lax.dot_general(qh.astype(BF16), kh.astype(BF16), (((1,), (1,)), ((), ())),
                                     preferred_element_type=F32) * dm_ref[h]
            pre[ci, h] = (_bdot(scores, v), qh * qd_ref[:, lo:hi], kh * kd_ref[:, lo:hi], v)
    for ci, h in sorted(pre):
        rs = slice(ci * RET_ROWS, (ci + 1) * RET_ROWS)
        lo, hi = h * R_HEAD, (h + 1) * R_HEAD
        o, qd, ku, v = pre[ci, h]
        gate = rq_ref[rs, 1536 + lo:1536 + hi]
        cd = cd_ref[h]
        if nb == 1:
            s = s_ref[0, h]
            o = o + _bdot(qd, s)
            upd = lax.dot_general(ku.astype(BF16), v.astype(BF16), (((0,), (0,)), ((), ())),
                                  preferred_element_type=F32)
            s_ref[0, h] = s * cd + upd
        else:
            per_tile = 8 // lb
            inter = []
            for i in range(RET_ROWS // 8):
                qd_t = qd[i * 8:(i + 1) * 8, :]
                ku_t = ku[i * 8:(i + 1) * 8, :]
                v_t = v[i * 8:(i + 1) * 8, :].astype(BF16)
                acc = None
                for j in range(per_tile):
                    b = i * per_tile + j
                    m = (row8 >= j * lb) & (row8 < (j + 1) * lb)
                    s = s_ref[b, h]
                    part = _bdot(jnp.where(m, qd_t, 0.0), s)
                    acc = part if acc is None else acc + part
                    upd = lax.dot_general(jnp.where(m, ku_t, 0.0).astype(BF16), v_t,
                                          (((0,), (0,)), ((), ())), preferred_element_type=F32)
                    s_ref[b, h] = s * cd + upd
                inter.append(acc)
            o = o + jnp.concatenate(inter, axis=0)
        oc = o - jnp.mean(o, axis=-1, keepdims=True)
        on = oc * lax.rsqrt(jnp.mean(oc * oc, axis=-1, keepdims=True) + R_NORM_EPS)
        o_ref[rs, lo:hi] = (gate * jax.nn.sigmoid(gate) * on).astype(o_ref.dtype)
    sout_ref[...] = s_ref[...]


def _retention(rq, s0_all, l, tabs, batch, seq):
    lb = min(R_CHUNK, seq)
    nb = RET_ROWS // lb
    nck = RET_CHUNKS_PER_STEP if (nb == 1 and (seq // lb) % RET_CHUNKS_PER_STEP == 0) else 1
    rows = RET_ROWS * nck
    nchunk = seq // (lb * nck)
    cos, sin, dm, kd, qd, cd = tabs
    ntab = cos.shape[0] // rows

    def const(arr):
        nd = arr.ndim
        return pl.BlockSpec(arr.shape, lambda i, c: (0,) * nd)

    tab_idx = (lambda i, c: (c, 0)) if ntab > 1 else (lambda i, c: (0, 0))
    return pl.pallas_call(
        functools.partial(_ret_kernel, nb, lb, nck),
        out_shape=(jax.ShapeDtypeStruct((batch * seq, D_BRANCH), BF16),
                   jax.ShapeDtypeStruct((batch, R_HEADS, R_HEAD, R_HEAD), F32)),
        grid=(batch // nb, nchunk),
        in_specs=[pl.BlockSpec((rows, 4 * D_BRANCH), lambda i, c: (i * nchunk + c, 0)),
                  pl.BlockSpec((rows, R_HEAD), tab_idx),
                  pl.BlockSpec((rows, R_HEAD), tab_idx),
                  const(dm), const(kd), const(qd), const(cd),
                  pl.BlockSpec((None, nb, R_HEADS, R_HEAD, R_HEAD), lambda i, c: (l, i, 0, 0, 0))],
        out_specs=(pl.BlockSpec((rows, D_BRANCH), lambda i, c: (i * nchunk + c, 0)),
                   pl.BlockSpec((nb, R_HEADS, R_HEAD, R_HEAD), lambda i, c: (i, 0, 0, 0))),
        scratch_shapes=[pltpu.VMEM((nb, R_HEADS, R_HEAD, R_HEAD), F32)],
        compiler_params=_cparams(("arbitrary", "arbitrary")),
    )(rq, cos, sin, dm, kd, qd, cd, s0_all)


def _ret_tables(seq, pos0):
    lb = min(R_CHUNK, seq)
    nb = RET_ROWS // lb
    half = R_HEAD // 2
    pos = pos0 + jnp.arange(seq, dtype=jnp.int32)
    inv = ROPE_BASE ** (-jnp.arange(half, dtype=F32) / half)
    ang = pos.astype(F32)[:, None] * inv[None, :]
    cos, sin = jnp.cos(ang), jnp.sin(ang)
    cos2 = jnp.concatenate([cos, cos], axis=-1)
    sin2 = jnp.concatenate([-sin, sin], axis=-1)
    if nb > 1:
        cos2 = jnp.tile(cos2, (nb, 1))
        sin2 = jnp.tile(sin2, (nb, 1))
    log_g = jnp.log1p(-jnp.exp2(-5.0 - jnp.arange(R_HEADS, dtype=F32)))
    idx = jnp.arange(lb, dtype=F32)
    diff = idx[:, None] - idx[None, :]
    dmask = jnp.where(diff >= 0, jnp.exp(log_g[:, None, None] * jnp.maximum(diff, 0.0)), 0.0)
    if nb > 1:
        dmask = jnp.einsum("ab,hij->haibj", jnp.eye(nb, dtype=F32), dmask).reshape(
            R_HEADS, RET_ROWS, RET_ROWS)
    k_decay = jnp.exp(log_g[None, :] * (lb - 1.0 - idx)[:, None])
    q_decay = jnp.exp(log_g[None, :] * (idx + 1.0)[:, None])
    kd = jnp.tile(jnp.repeat(k_decay, R_HEAD, axis=1), (nb, 1))
    qd = jnp.tile(jnp.repeat(q_decay, R_HEAD, axis=1), (nb, 1))
    cd = jnp.broadcast_to(jnp.exp(log_g * lb)[:, None, None], (R_HEADS, 1, R_HEAD))
    return cos2, sin2, dmask, kd, qd, cd


def _lru_kernel(nb, lb, three_d,
                cx_ref, conv0_ref, h0_ref, cw_ref, cb_ref, wri_ref, br_ref, bi_ref, lam_ref,
                o_ref, hout_ref,
                cc_ref, hc_ref, x1_ref, x2_ref, x3_ref, a_ref, b_ref):
    c = pl.program_id(1)
    rows = nb * lb

    @pl.when(c == 0)
    def _():
        cc_ref[...] = conv0_ref[...]
        hc_ref[...] = h0_ref[...]

    cx = cx_ref[...]
    if three_d:
        cx = cx.reshape(rows, 2 * D_BRANCH)
    xb = cx[:, :D_BRANCH]
    gb = cx[:, D_BRANCH:]
    x1_ref[...] = pltpu.roll(xb, 1, axis=0)
    x2_ref[...] = pltpu.roll(xb, 2, axis=0)
    x3_ref[...] = pltpu.roll(xb, 3, axis=0)
    for b in range(nb):
        r0 = b * lb
        c0 = cc_ref[b, 0:1, :]
        c1 = cc_ref[b, 1:2, :]
        c2 = cc_ref[b, 2:3, :]
        x1_ref[pl.ds(r0, 1), :] = c2
        x2_ref[pl.ds(r0, 1), :] = c1
        x2_ref[pl.ds(r0 + 1, 1), :] = c2
        x3_ref[pl.ds(r0, 1), :] = c0
        x3_ref[pl.ds(r0 + 1, 1), :] = c1
        x3_ref[pl.ds(r0 + 2, 1), :] = c2
        cc_ref[b] = xb[r0 + lb - 3:r0 + lb, :]
    cw = cw_ref[...]
    xc = cb_ref[...] + (((x3_ref[...] * cw[0:1, :] + x2_ref[...] * cw[1:2, :]) + x1_ref[...] * cw[2:3, :])
                        + xb * cw[3:4, :])
    ri = _bdot(xc, wri_ref[...])
    r = jax.nn.sigmoid(ri[:, :D_BRANCH] + br_ref[...])
    i = jax.nn.sigmoid(ri[:, D_BRANCH:] + bi_ref[...])
    log_a = LRU_C * r * jax.nn.log_sigmoid(lam_ref[...])
    a = jnp.exp(log_a)
    bb = jnp.sqrt(-jnp.tanh(log_a) * (a * a + 1.0)) * (i * xc)
    a_ref[...] = a
    b_ref[...] = bb
    for b in range(nb):
        r0 = b * lb
        b_ref[pl.ds(r0, 1), :] = bb[r0:r0 + 1, :] + a[r0:r0 + 1, :] * hc_ref[b]
    a = a_ref[...]
    bb = b_ref[...]
    t_idx = lax.broadcasted_iota(jnp.int32, (rows, D_BRANCH), 0) % lb
    s = 1
    while s < lb:
        keep = t_idx >= s
        a_sh = jnp.where(keep, pltpu.roll(a, s, axis=0), 1.0)
        b_sh = jnp.where(keep, pltpu.roll(bb, s, axis=0), 0.0)
        bb = a * b_sh + bb
        a = a * a_sh
        s *= 2
    h = bb
    for b in range(nb):
        r0 = b * lb
        hc_ref[b] = h[r0 + lb - 1:r0 + lb, :]
    res = (h * jax.nn.gelu(gb)).astype(o_ref.dtype)
    if three_d:
        res = res.reshape(nb, lb, D_BRANCH)
    o_ref[...] = res
    hout_ref[...] = hc_ref[...]


def _rglru(cx, conv0, h0, prm, batch, seq, nb, lb):
    nchunk = seq // lb
    rows = nb * lb
    three_d = nb > 1 and nchunk > 1
    if three_d:
        cx_in = cx.reshape(batch, seq, 2 * D_BRANCH)
        cx_spec = pl.BlockSpec((nb, lb, 2 * D_BRANCH), lambda i, c: (i, c, 0))
        o_shape = jax.ShapeDtypeStruct((batch, seq, D_BRANCH), BF16)
        o_spec = pl.BlockSpec((nb, lb, D_BRANCH), lambda i, c: (i, c, 0))
    else:
        cx_in = cx
        cx_spec = pl.BlockSpec((rows, 2 * D_BRANCH), lambda i, c: (i * nchunk + c, 0))
        o_shape = jax.ShapeDtypeStruct((batch * seq, D_BRANCH), BF16)
        o_spec = pl.BlockSpec((rows, D_BRANCH), lambda i, c: (i * nchunk + c, 0))

    def full(arr):
        nd = arr.ndim
        return pl.BlockSpec(arr.shape, lambda i, c: (0,) * nd)

    params = [prm[n] for n in ("cw", "cb", "wri", "br", "bi", "lam")]
    o, h_out = pl.pallas_call(
        functools.partial(_lru_kernel, nb, lb, three_d),
        out_shape=(o_shape, jax.ShapeDtypeStruct((batch, 1, D_BRANCH), F32)),
        grid=(batch // nb, nchunk),
        in_specs=[cx_spec,
                  pl.BlockSpec((nb, CONV_W - 1, D_BRANCH), lambda i, c: (i, 0, 0)),
                  pl.BlockSpec((nb, 1, D_BRANCH), lambda i, c: (i, 0, 0))]
                 + [full(p) for p in params],
        out_specs=(o_spec, pl.BlockSpec((nb, 1, D_BRANCH), lambda i, c: (i, 0, 0))),
        scratch_shapes=[pltpu.VMEM((nb, CONV_W - 1, D_BRANCH), F32),
                        pltpu.VMEM((nb, 1, D_BRANCH), F32)]
                       + [pltpu.VMEM((rows, D_BRANCH), F32) for _ in range(5)],
        compiler_params=_cparams(("arbitrary", "arbitrary")),
    )(cx_in, conv0, h0.reshape(batch, 1, D_BRANCH), *params)
    return o.reshape(batch * seq, D_BRANCH), h_out.reshape(batch, D_BRANCH)


def _merge_kernel(h_ref, oa_ref, ob_ref, oc_ref, x_ref, gt_ref, wgm_ref, wb_ref, wout_ref, o_ref):
    gm = _dot(h_ref[...], wgm_ref[...])
    merged = None
    for n, br_ref in enumerate((oa_ref, ob_ref, oc_ref)):
        br = _dot(br_ref[...], wb_ref[n])
        term = jax.nn.sigmoid(gm[:, n * D_MODEL:(n + 1) * D_MODEL]) * br
        merged = term if merged is None else merged + term
    y = _bdot(merged, wout_ref[...])
    o_ref[...] = x_ref[...] + gt_ref[...] * y


def _merge_out(h, oa, ob, oc, x, grp, w3, l, wb, wout, tm):
    row = lambda i: (i, 0)
    return pl.pallas_call(
        _merge_kernel,
        out_shape=jax.ShapeDtypeStruct((grp.rows, D_MODEL), F32),
        grid=(grp.rows // tm,),
        in_specs=[pl.BlockSpec((tm, D_MODEL), row),
                  pl.BlockSpec((tm, D_BRANCH), row),
                  pl.BlockSpec((tm, D_BRANCH), row),
                  pl.BlockSpec((tm, D_BRANCH), row),
                  pl.BlockSpec((tm, D_MODEL), row),
                  grp.mod_spec(2, tm),
                  pl.BlockSpec((None, D_MODEL, 3 * D_MODEL), lambda i: (l, 0, W_GM // (3 * D_MODEL))),
                  pl.BlockSpec(wb.shape, lambda i: (0, 0, 0)),
                  pl.BlockSpec(wout.shape, lambda i: (0, 0))],
        out_specs=pl.BlockSpec((tm, D_MODEL), row),
        compiler_params=_cparams(("arbitrary",)),
    )(h, oa, ob, oc, x, grp.mod4, w3, wb, wout)


def _group_member(x, k, lane):
    ge = EXPERTS_PER_GROUP
    pos = lane & (ge - 1)
    return jnp.where(pos + k < ge, pltpu.roll(x, LANES - k, axis=1), pltpu.roll(x, ge - k, axis=1))


def _in_group_top2(sel, lane):
    ge = EXPERTS_PER_GROUP
    pos = lane & (ge - 1)
    n_ahead = jnp.zeros(sel.shape, jnp.int32)
    for k in range(1, ge):
        other = _group_member(sel, k, lane)
        lower_index = pos + k >= ge
        n_ahead = n_ahead + ((other > sel) | ((other == sel) & lower_index)).astype(jnp.int32)
    return n_ahead < TOP_K


def _best_group(sel, in_top2, lane):
    ge = EXPERTS_PER_GROUP
    kept = jnp.where(in_top2, sel, 0.0)
    score = kept
    for k in range(1, ge):
        score = score + _group_member(kept, k, lane)
    best = lane < N_EXPERTS
    for d in range(1, N_GROUPS):
        later = pltpu.roll(score, LANES - ge * d, axis=1)
        earlier = pltpu.roll(score, ge * d, axis=1)
        best = best & (later <= score) & (earlier < score)
    return best


def _router_probs(h_bf16, wr_ref, rb_ref, lane):
    logits = _dot(h_bf16, wr_ref[...])
    logits = jnp.where(lane < N_EXPERTS, logits, -jnp.inf)
    mx = jnp.max(logits, axis=-1, keepdims=True)
    ex = jnp.exp(logits - mx)
    probs = ex / jnp.sum(ex, axis=-1, keepdims=True)
    sel = jnp.where(lane < N_EXPERTS, probs + rb_ref[...], -jnp.inf)
    return probs, sel


def _route(probs, sel, lane):
    in_top2 = _in_group_top2(sel, lane)
    pk = jnp.where(_best_group(sel, in_top2, lane) & in_top2, probs, 0.0)
    return pk / jnp.sum(pk, axis=-1, keepdims=True)


def _moe_kernel(final, x_ref, g_ref, sc_ref, sh_ref, gt_ref, gf_ref, wr_ref, rb_ref, wg_ref, wu_ref, wd_ref,
                o_ref, h_s, gate_s):
    e = pl.program_id(1)
    tm = x_ref.shape[0]
    lane = lax.broadcasted_iota(jnp.int32, (tm, LANES), 1)

    @pl.when(e == 0)
    def _():
        y = _rms(x_ref[...], g_ref[...])
        h = (y * (1.0 + sc_ref[...]) + sh_ref[...]).astype(BF16)
        h_s[...] = h
        probs, selv = _router_probs(h, wr_ref, rb_ref, lane)
        gate_s[...] = _route(probs, selv, lane)
        o_ref[...] = jnp.zeros_like(o_ref)

    h = h_s[...]
    gcol = jnp.sum(jnp.where(lane == e, gate_s[...], 0.0), axis=-1, keepdims=True)
    hg = _bdot(h, wg_ref[...])
    hu = _bdot(h, wu_ref[...])
    act = hg * jax.nn.sigmoid(hg) * hu * gcol
    o_ref[...] += _bdot(act, wd_ref[...])

    @pl.when(e == N_EXPERTS - 1)
    def _():
        res = x_ref[...] + gt_ref[...] * o_ref[...]
        o_ref[...] = _rms(res, gf_ref[...]) if final else res


def _moe(x, g, grp, wr, rb, wg, wu, wd, l, tm, gf=None):
    row = lambda i, e: (i, 0)
    final = gf is not None
    gain = (gf if final else g).reshape(1, D_MODEL)
    return pl.pallas_call(
        functools.partial(_moe_kernel, final),
        out_shape=jax.ShapeDtypeStruct((grp.rows, D_MODEL), F32),
        grid=(grp.rows // tm, N_EXPERTS),
        in_specs=[pl.BlockSpec((tm, D_MODEL), row),
                  pl.BlockSpec((1, D_MODEL), lambda i, e: (0, 0)),
                  grp.mod_spec(4, tm),
                  grp.mod_spec(3, tm),
                  grp.mod_spec(5, tm),
                  pl.BlockSpec((1, D_MODEL), lambda i, e: (0, 0)),
                  pl.BlockSpec((D_MODEL, LANES), lambda i, e: (0, 0)),
                  pl.BlockSpec((1, LANES), lambda i, e: (0, 0)),
                  pl.BlockSpec((None, None, D_MODEL, D_EXPERT), lambda i, e: (l, e, 0, 0)),
                  pl.BlockSpec((None, None, D_MODEL, D_EXPERT), lambda i, e: (l, e, 0, 0)),
                  pl.BlockSpec((None, None, D_EXPERT, D_MODEL), lambda i, e: (l, e, 0, 0))],
        out_specs=pl.BlockSpec((tm, D_MODEL), row),
        scratch_shapes=[pltpu.VMEM((tm, D_MODEL), BF16),
                        pltpu.VMEM((tm, LANES), F32)],
        compiler_params=_cparams(("arbitrary", "arbitrary"), MOE_VMEM_LIMIT),
    )(x, g.reshape(1, D_MODEL), grp.mod4, grp.mod4, grp.mod4, gain, wr, rb, wg, wu, wd)


MOE_SORT_TILE = 1024
MOE_DMA_ROWS = 2048
DMA_ISSUE_GROUP = 8


def _moe_router_kernel(x_ref, g_ref, sc_ref, sh_ref, wrt_ref, rbt_ref, triu_ref,
                       h_ref, gr_ref, cnt_ref, base_s):
    i = pl.program_id(0)
    tm = x_ref.shape[0]
    ge = EXPERTS_PER_GROUP

    @pl.when(i == 0)
    def _():
        base_s[...] = jnp.zeros_like(base_s)

    y = _rms(x_ref[...], g_ref[...])
    h = y * (1.0 + sc_ref[...]) + sh_ref[...]
    h_ref[...] = h
    logits = lax.dot_general(wrt_ref[...], h.astype(BF16), _NT, preferred_element_type=F32)
    ex = jnp.exp(logits - jnp.max(logits, axis=0, keepdims=True))
    probs = ex / jnp.sum(ex, axis=0, keepdims=True)
    sel = probs + rbt_ref[:, 0:1]
    row = lax.broadcasted_iota(jnp.int32, (N_EXPERTS, tm), 0)
    pos = row & (ge - 1)
    grp_of = row >> 2

    def member(x, k):
        return jnp.where(pos + k < ge, pltpu.roll(x, N_EXPERTS - k, axis=0), pltpu.roll(x, ge - k, axis=0))

    n_ahead = jnp.zeros(sel.shape, jnp.int32)
    for k in range(1, ge):
        other = member(sel, k)
        n_ahead = n_ahead + ((other > sel) | ((other == sel) & (pos + k >= ge))).astype(jnp.int32)
    kept = jnp.where(n_ahead < TOP_K, sel, 0.0)
    score = kept
    for k in range(1, ge):
        score = score + member(kept, k)
    best = pos == 0
    for d in range(1, N_GROUPS):
        later = pltpu.roll(score, N_EXPERTS - ge * d, axis=0)
        earlier = pltpu.roll(score, ge * d, axis=0)
        best = best & ((grp_of + d >= N_GROUPS) | (later <= score)) & ((grp_of < d) | (earlier < score))
    gid = jnp.sum(jnp.where(best, grp_of, 0), axis=0, keepdims=True)
    onehot = (row == gid).astype(BF16)
    incl = _dot(onehot, triu_ref[...]) + base_s[:, 0:1]
    rank = jnp.sum(jnp.where(row == gid, incl - 1.0, 0.0), axis=0, keepdims=True).astype(jnp.int32)
    row8 = lax.broadcasted_iota(jnp.int32, (8, tm), 0)
    gr_ref[...] = jnp.where(row8 == 0, gid, jnp.where(row8 == 1, rank, 0))
    total = jnp.broadcast_to(incl[:, tm - 1:tm], (N_EXPERTS, LANES))
    base_s[...] = total
    cnt_ref[...] = total.astype(jnp.int32)


def _scatter_rows_kernel(dest_ref, src_ref, zeros_hbm, dst_hbm, sem):
    del zeros_hbm
    base = pl.program_id(0) * MOE_DMA_ROWS

    def issue(t8, carry):
        for k in range(DMA_ISSUE_GROUP):
            t = t8 * DMA_ISSUE_GROUP + k
            pltpu.make_async_copy(src_ref.at[pl.ds(t, 1)], dst_hbm.at[pl.ds(dest_ref[base + t], 1)],
                                  sem).start(priority=k % 2)
        return carry

    def drain(t, carry):
        pltpu.make_async_copy(src_ref.at[pl.ds(0, 1)], dst_hbm.at[pl.ds(0, 1)], sem).wait()
        return carry

    lax.fori_loop(0, MOE_DMA_ROWS // DMA_ISSUE_GROUP, issue, 0)
    lax.fori_loop(0, MOE_DMA_ROWS, drain, 0, unroll=8)


def _moe_group_kernel(tg_ref, nv_ref, xs_ref, wr_ref, rb_ref, wg_ref, wu_ref, wd_ref, ys_ref, h_s, gate_s):
    i = pl.program_id(0)
    j = pl.program_id(1)
    tm = xs_ref.shape[0]
    lane = lax.broadcasted_iota(jnp.int32, (tm, LANES), 1)
    grp = tg_ref[i]

    @pl.when((i >= nv_ref[0]) & (j == 0))
    def _():
        ys_ref[...] = jnp.zeros_like(ys_ref)

    @pl.when(i < nv_ref[0])
    def _():
        @pl.when(j == 0)
        def _():
            h = xs_ref[...].astype(BF16)
            h_s[...] = h
            probs, sel = _router_probs(h, wr_ref, rb_ref, lane)
            pk = jnp.where(_in_group_top2(sel, lane) & ((lane >> 2) == grp), probs, 0.0)
            psum = jnp.sum(pk, axis=-1, keepdims=True)
            gate_s[...] = pk / jnp.where(psum > 0.0, psum, 1.0)
            ys_ref[...] = jnp.zeros_like(ys_ref)

        h = h_s[...]
        gcol = jnp.sum(jnp.where(lane == grp * EXPERTS_PER_GROUP + j, gate_s[...], 0.0), axis=-1, keepdims=True)
        hg = _bdot(h, wg_ref[...])
        hu = _bdot(h, wu_ref[...])
        act = hg * jax.nn.sigmoid(hg) * hu * gcol
        ys_ref[...] += _bdot(act, wd_ref[...])


def _gather_residual_kernel(final, dest_ref, x_ref, gt_ref, gf_ref, ys_hbm, o_ref, buf, sem):
    i = pl.program_id(0)
    n = pl.num_programs(0)
    tm = x_ref.shape[0]

    def start_tile(tile, slot):
        def issue(t8, carry):
            for k in range(DMA_ISSUE_GROUP):
                t = t8 * DMA_ISSUE_GROUP + k
                pltpu.make_async_copy(ys_hbm.at[pl.ds(dest_ref[tile * tm + t], 1)], buf.at[slot, pl.ds(t, 1)],
                                      sem.at[slot]).start(priority=k % 2)
            return carry
        lax.fori_loop(0, tm // DMA_ISSUE_GROUP, issue, 0)

    @pl.when(i == 0)
    def _():
        start_tile(0, 0)

    @pl.when(i + 1 < n)
    def _():
        start_tile(i + 1, (i + 1) % 2)

    slot = i % 2

    def drain(t, carry):
        pltpu.make_async_copy(ys_hbm.at[pl.ds(0, 1)], buf.at[slot, pl.ds(0, 1)], sem.at[slot]).wait()
        return carry

    lax.fori_loop(0, tm, drain, 0, unroll=8)
    res = x_ref[...] + gt_ref[...] * buf[slot]
    if final:
        res = _rms(res, gf_ref[...])
    o_ref[...] = res


def _moe_sorted(x, g, grp, wr, rb, wg, wu, wd, l, tm, gf=None):
    rows = grp.rows
    te = MOE_SORT_TILE
    ntile = rows // te + N_GROUPS
    cap = ntile * te
    triu = jnp.asarray(np.triu(np.ones((tm, tm), np.float32)), BF16)
    wrt = wr[:, :N_EXPERTS].T
    rbt = jnp.broadcast_to(rb[0, :N_EXPERTS, None], (N_EXPERTS, LANES))
    row = lambda i: (i, 0)
    h, gr, cnt = pl.pallas_call(
        _moe_router_kernel,
        out_shape=(jax.ShapeDtypeStruct((rows, D_MODEL), F32),
                   jax.ShapeDtypeStruct((rows // tm, 8, tm), jnp.int32),
                   jax.ShapeDtypeStruct((N_EXPERTS, LANES), jnp.int32)),
        grid=(rows // tm,),
        in_specs=[pl.BlockSpec((tm, D_MODEL), row),
                  pl.BlockSpec((1, D_MODEL), lambda i: (0, 0)),
                  grp.mod_spec(4, tm),
                  grp.mod_spec(3, tm),
                  pl.BlockSpec((N_EXPERTS, D_MODEL), lambda i: (0, 0)),
                  pl.BlockSpec((N_EXPERTS, LANES), lambda i: (0, 0)),
                  pl.BlockSpec((tm, tm), lambda i: (0, 0))],
        out_specs=(pl.BlockSpec((tm, D_MODEL), row),
                   pl.BlockSpec((None, 8, tm), lambda i: (i, 0, 0)),
                   pl.BlockSpec((N_EXPERTS, LANES), lambda i: (0, 0))),
        scratch_shapes=[pltpu.VMEM((N_EXPERTS, LANES), F32)],
        compiler_params=_cparams(("arbitrary",)),
    )(x, g.reshape(1, D_MODEL), grp.mod4, grp.mod4, wrt, rbt, triu)

    counts = cnt[:N_GROUPS, 0]
    padded = ((counts + te - 1) // te) * te
    ends = jnp.cumsum(padded)
    dest = (ends - padded)[gr[:, 0, :].reshape(rows)] + gr[:, 1, :].reshape(rows)
    tile_group = jnp.minimum(jnp.searchsorted(ends, jnp.arange(ntile, dtype=jnp.int32) * te, side="right"),
                             N_GROUPS - 1).astype(jnp.int32)
    n_valid = (ends[N_GROUPS - 1] // te).astype(jnp.int32).reshape(1)

    xs = pl.pallas_call(
        _scatter_rows_kernel,
        out_shape=jax.ShapeDtypeStruct((cap, D_MODEL), F32),
        grid_spec=pltpu.PrefetchScalarGridSpec(
            num_scalar_prefetch=1,
            grid=(rows // MOE_DMA_ROWS,),
            in_specs=[pl.BlockSpec((MOE_DMA_ROWS, D_MODEL), lambda i, d: (i, 0)),
                      pl.BlockSpec(memory_space=pl.ANY)],
            out_specs=pl.BlockSpec(memory_space=pl.ANY),
            scratch_shapes=[pltpu.SemaphoreType.DMA(())]),
        input_output_aliases={2: 0},
        compiler_params=_cparams(("arbitrary",)),
    )(dest, h, jnp.zeros((cap, D_MODEL), F32))

    ys = pl.pallas_call(
        _moe_group_kernel,
        out_shape=jax.ShapeDtypeStruct((cap, D_MODEL), F32),
        grid_spec=pltpu.PrefetchScalarGridSpec(
            num_scalar_prefetch=2,
            grid=(ntile, EXPERTS_PER_GROUP),
            in_specs=[pl.BlockSpec((te, D_MODEL), lambda i, j, tg, nv: (i, 0)),
                      pl.BlockSpec((D_MODEL, LANES), lambda i, j, tg, nv: (0, 0)),
                      pl.BlockSpec((1, LANES), lambda i, j, tg, nv: (0, 0)),
                      pl.BlockSpec((None, None, D_MODEL, D_EXPERT),
                                   lambda i, j, tg, nv: (l, tg[i] * EXPERTS_PER_GROUP + j, 0, 0)),
                      pl.BlockSpec((None, None, D_MODEL, D_EXPERT),
                                   lambda i, j, tg, nv: (l, tg[i] * EXPERTS_PER_GROUP + j, 0, 0)),
                      pl.BlockSpec((None, None, D_EXPERT, D_MODEL),
                                   lambda i, j, tg, nv: (l, tg[i] * EXPERTS_PER_GROUP + j, 0, 0))],
            out_specs=pl.BlockSpec((te, D_MODEL), lambda i, j, tg, nv: (i, 0)),
            scratch_shapes=[pltpu.VMEM((te, D_MODEL), BF16), pltpu.VMEM((te, LANES), F32)]),
        compiler_params=_cparams(("arbitrary", "arbitrary"), MOE_VMEM_LIMIT),
    )(tile_group, n_valid, xs, wr, rb, wg, wu, wd)

    gt_spec = grp.mod_spec(5, tm)
    final = gf is not None
    gain = (gf if final else g).reshape(1, D_MODEL)
    return pl.pallas_call(
        functools.partial(_gather_residual_kernel, final),
        out_shape=jax.ShapeDtypeStruct((rows, D_MODEL), F32),
        grid_spec=pltpu.PrefetchScalarGridSpec(
            num_scalar_prefetch=1,
            grid=(rows // tm,),
            in_specs=[pl.BlockSpec((tm, D_MODEL), lambda i, d: (i, 0)),
                      pl.BlockSpec(gt_spec.block_shape, lambda i, d: gt_spec.index_map(i)),
                      pl.BlockSpec((1, D_MODEL), lambda i, d: (0, 0)),
                      pl.BlockSpec(memory_space=pl.ANY)],
            out_specs=pl.BlockSpec((tm, D_MODEL), lambda i, d: (i, 0)),
            scratch_shapes=[pltpu.VMEM((2, tm, D_MODEL), F32), pltpu.SemaphoreType.DMA((2,))]),
        compiler_params=_cparams(("arbitrary",)),
    )(dest, x, grp.mod4, gain, ys)


def _block_diag(w):
    eye = jnp.eye(C_BLOCKS, dtype=w.dtype)
    return jnp.einsum("hg,hij->higj", eye, w).reshape(D_BRANCH, D_BRANCH)


def _layer_params(l, p):
    row = lambda a: a[l].reshape(1, -1)
    ones = jnp.asarray(np.kron(np.eye(4), np.ones((A_HEAD, A_HEAD))), BF16)
    pad_lo = lambda w: jnp.pad(w, ((0, 64), (0, 0)))
    pad_hi = lambda w: jnp.pad(w, ((64, 0), (0, 0)))
    rw = dict(mu=row(p["a_mu"]), w0=row(p["a_w0"]), w2=pad_lo(p["a_w2"][l]).astype(BF16),
              a0=row(p["a_a0"]), a2=pad_hi(p["a_a2"][l]).astype(BF16), g2=p["a_g2"][l].astype(BF16),
              kk=row(p["a_kk"]), ka=row(p["a_ka"]), rk=row(p["a_rk"]), lng=row(p["a_ln_g"]),
              lnb=row(p["a_ln_b"]), ones=ones)
    lru = dict(cw=p["c_conv_w"][l], cb=row(p["c_conv_b"]),
               wri=jnp.concatenate([_block_diag(p["c_wr"][l]), _block_diag(p["c_wi"][l])], axis=1).astype(BF16),
               br=row(p["c_br"]), bi=row(p["c_bi"]), lam=row(p["c_lam"]))
    return dict(
        rw=rw, lru=lru,
        w_branch=p["w_branch"][l].astype(BF16),
        w_out=p["w_out"][l].astype(BF16),
        wg=p["moe_wg"], wu=p["moe_wu"], wd=p["moe_wd"],
        norm_mix=p["norm_mix"][l], norm_ffn=p["norm_ffn"][l])


def kernel(x_prompt, x_sample, c_prompt, c_sample, state_rwkv_shift, state_rwkv_wkv, state_ret, state_lru_h, state_lru_conv, norm_mix, norm_ffn, norm_final, ada_w, ada_b, w_in, a_mu, a_w0, a_w2, a_a0, a_a2, a_g2, a_kk, a_ka, a_rk, a_ln_g, a_ln_b, c_conv_w, c_conv_b, c_wr, c_br, c_wi, c_bi, c_lam, w_branch, w_out, w_router, router_bias, moe_wg, moe_wu, moe_wd):
    p = dict(norm_mix=norm_mix, norm_ffn=norm_ffn, w_in=w_in, a_mu=a_mu, a_w0=a_w0, a_w2=a_w2,
             a_a0=a_a0, a_a2=a_a2, a_g2=a_g2, a_kk=a_kk, a_ka=a_ka, a_rk=a_rk, a_ln_g=a_ln_g,
             a_ln_b=a_ln_b, c_conv_w=c_conv_w, c_conv_b=c_conv_b, c_wr=c_wr, c_br=c_br, c_wi=c_wi,
             c_bi=c_bi, c_lam=c_lam, w_branch=w_branch, w_out=w_out, moe_wg=moe_wg, moe_wu=moe_wu,
             moe_wd=moe_wd)
    bp, lp_, _ = x_prompt.shape
    bs, ls, _ = x_sample.shape
    layers = [_layer_params(l, p) for l in range(DEPTH)]
    w3 = _arrange_w_in(w_in)
    router = (jnp.pad(w_router, ((0, 0), (0, LANES - N_EXPERTS))).astype(BF16),
              jnp.pad(router_bias, (0, LANES - N_EXPERTS)).reshape(1, LANES))

    n_c = bp + bs
    pad_c = (-n_c) % 16
    c_all = jnp.pad(jnp.concatenate([c_prompt, c_sample], axis=0), ((0, pad_c), (0, 0)))
    mods = _ada(c_all, ada_w, ada_b)

    def run(x, batch, seq, mod, states, pos0, cfg):
        xs = x.reshape(batch * seq, D_MODEL)
        grps = []
        for l in range(DEPTH):
            if cfg["per_token"]:
                m = jnp.repeat(mod[l], seq, axis=0).reshape(batch * seq, 6, D_MODEL)
                m4 = m.transpose(1, 0, 2)[None]
            else:
                m4 = mod[l].reshape(batch, 6, 1, D_MODEL)
            grps.append(_Group(batch, seq, m4))
        return _trunk_layers(xs, grps, states, layers, w3, router, norm_final, pos0, cfg)

    zeros = lambda s: jnp.zeros((DEPTH, bp) + s.shape[2:], x_prompt.dtype)
    st_prompt = (zeros(state_rwkv_shift), zeros(state_rwkv_wkv), zeros(state_ret),
                 zeros(state_lru_h), zeros(state_lru_conv))
    st_sample = (state_rwkv_shift, state_rwkv_wkv, state_ret, state_lru_h, state_lru_conv)
    cfg_p = dict(per_token=False, tm=min(512, lp_), tm_mg=min(256, lp_), tm_mm=min(1024, lp_),
                 tm_moe=min(1024, lp_), moe_sorted=(bp * lp_) % MOE_DMA_ROWS == 0, rwkv_nb=8, lru_nb=1, lru_lb=min(256, lp_))
    cfg_s = dict(per_token=True, tm=min(512, bs * ls), tm_mg=min(256, bs * ls), tm_mm=min(512, bs * ls),
                 tm_moe=min(512, bs * ls), moe_sorted=False, rwkv_nb=16, lru_nb=16, lru_lb=ls)
    y_p, new_p = run(x_prompt, bp, lp_, mods[:, :bp], st_prompt, 0, cfg_p)
    y_s, new_s = run(x_sample, bs, ls, mods[:, bp:bp + bs], st_sample, PAST_LEN, cfg_s)
    return (y_p, y_s) + new_p + new_s


def _trunk_layers(x, grps, states, layers, w3, router, norm_final, pos0, cfg):
    batch, seq = grps[0].batch, grps[0].seq
    tabs = _ret_tables(seq, pos0)
    wr, rb = router
    outs = [[] for _ in range(5)]
    for l, lp in enumerate(layers):
        grp = grps[l]
        h = _norm_mod(x, lp["norm_mix"], grp, 1, 0, cfg["tm"])
        pa = _matmul(h, w3, l, W_PA, W_PA_PAD, cfg["tm_mm"], 1024)
        rq = _matmul(h, w3, l, W_RQ, 4 * D_BRANCH, cfg["tm_mm"], 1024)
        cx = _matmul(h, w3, l, W_CX, 2 * D_BRANCH, cfg["tm_mm"], 1024)
        if seq % RW_CHUNK == 0:
            o_a, wkv = _rwkv_chunked(pa, states[0][l], states[1][l], lp["rw"], batch, seq,
                                     min(4, seq // RW_CHUNK))
        else:
            o_a, wkv = _rwkv_steps(pa, states[0][l], states[1][l], lp["rw"], batch, seq, cfg["rwkv_nb"])
        o_b, ret = _retention(rq, states[2], l, tabs, batch, seq)
        o_c, lru_h = _rglru(cx, states[4][l], states[3][l], lp["lru"], batch, seq,
                            cfg["lru_nb"], cfg["lru_lb"])
        x = _merge_out(h, o_a, o_b, o_c, x, grp, w3, l, lp["w_branch"], lp["w_out"], cfg["tm_mg"])
        moe = _moe_sorted if cfg["moe_sorted"] else _moe
        x = moe(x, lp["norm_ffn"], grp, wr, rb, lp["wg"], lp["wu"], lp["wd"], l, cfg["tm_moe"],
                gf=norm_final if l == DEPTH - 1 else None)
        outs[0].append(pa.reshape(batch, seq, W_PA_PAD)[:, -1, :A_PROJ])
        outs[1].append(wkv)
        outs[2].append(ret)
        outs[3].append(lru_h)
        outs[4].append(cx.reshape(batch, seq, 2 * D_BRANCH)[:, seq - (CONV_W - 1):, :D_BRANCH])
    return x.reshape(batch, seq, D_MODEL), tuple(jnp.stack(o) for o in outs)
```

```python
import functools
import math

import numpy as np
import jax
import jax.numpy as jnp
from jax import lax
from jax.experimental import pallas as pl
from jax.experimental.pallas import tpu as pltpu

F32 = jnp.float32
BF16 = jnp.bfloat16

D_MODEL = 1024
DEPTH = 2
PAST_LEN = 16384
D_BRANCH = 512
A_HEAD = 64
A_HEADS = 8
A_PROJ = 1792
A_NORM_EPS = 64e-5
A_KK_EPS = 1e-12
R_HEAD = 128
R_HEADS = 4
R_CHUNK = 128
R_NORM_EPS = 1e-6
ROPE_BASE = 10000.0
C_BLOCK = 64
C_BLOCKS = 8
CONV_W = 4
LRU_C = 8.0
N_EXPERTS = 16
N_GROUPS = 4
EXPERTS_PER_GROUP = 4
TOP_K = 2
D_EXPERT = 512
NORM_EPS = 1e-6

LANES = 128
RET_ROWS = 128
RET_CHUNKS_PER_STEP = 4
VMEM_LIMIT = 48 * 1024 * 1024
MOE_VMEM_LIMIT = 56 * 1024 * 1024


def _cparams(sem, vmem=VMEM_LIMIT):
    return pltpu.CompilerParams(dimension_semantics=sem, vmem_limit_bytes=vmem)


def _dot(a, b):
    return jnp.dot(a, b, preferred_element_type=F32)


def _bdot(a, b):
    return jnp.dot(a.astype(BF16), b.astype(BF16), preferred_element_type=F32)


def _split3(x):
    hi = x.astype(BF16)
    r1 = x - hi.astype(F32)
    mid = r1.astype(BF16)
    lo = (r1 - mid.astype(F32)).astype(BF16)
    return hi, mid, lo


def _segsum(x, ones_bf16):
    w = ones_bf16.shape[0]
    hi, mid, lo = _split3(x)
    parts = []
    for j in range(x.shape[1] // w):
        c = slice(j * w, (j + 1) * w)
        parts.append((_dot(lo[:, c], ones_bf16) + _dot(mid[:, c], ones_bf16)) + _dot(hi[:, c], ones_bf16))
    return jnp.concatenate(parts, axis=1)


def _ada_kernel(c_ref, w_ref, b_ref, o_ref):
    c = c_ref[...]
    s = c * jax.nn.sigmoid(c)
    o_ref[...] = _bdot(s, w_ref[...]) + b_ref[...]


def _ada(c, ada_w, ada_b):
    rows = c.shape[0]
    tn = 1536
    return pl.pallas_call(
        _ada_kernel,
        out_shape=jax.ShapeDtypeStruct((DEPTH, rows, 6 * D_MODEL), F32),
        grid=(DEPTH, 6 * D_MODEL // tn),
        in_specs=[
            pl.BlockSpec((rows, D_MODEL), lambda l, j: (0, 0)),
            pl.BlockSpec((None, D_MODEL, tn), lambda l, j: (l, 0, j)),
            pl.BlockSpec((None, 1, tn), lambda l, j: (l, 0, j)),
        ],
        out_specs=pl.BlockSpec((None, rows, tn), lambda l, j: (l, 0, j)),
        compiler_params=_cparams(("arbitrary", "arbitrary")),
    )(c, ada_w, ada_b.reshape(DEPTH, 1, 6 * D_MODEL))


class _Group:
    def __init__(self, batch, seq, mod4):
        self.batch = batch
        self.seq = seq
        self.rows = batch * seq
        self.mod4 = mod4
        self.per_token = mod4.shape[0] == 1 and mod4.shape[2] != 1

    def mod_spec(self, k, tm):
        if self.per_token:
            return pl.BlockSpec((None, None, tm, D_MODEL), lambda i, *_: (0, k, i, 0))
        seq = self.seq
        return pl.BlockSpec((None, None, 1, D_MODEL), lambda i, *_: ((i * tm) // seq, k, 0, 0))


def _rms(x, g):
    return x * lax.rsqrt(jnp.mean(x * x, axis=-1, keepdims=True) + NORM_EPS) * g


def _norm_mod_kernel(x_ref, g_ref, sc_ref, sh_ref, o_ref):
    y = _rms(x_ref[...], g_ref[...])
    o_ref[...] = (y * (1.0 + sc_ref[...]) + sh_ref[...]).astype(o_ref.dtype)


def _norm_mod(x, g, grp, k_sc, k_sh, tm):
    return pl.pallas_call(
        _norm_mod_kernel,
        out_shape=jax.ShapeDtypeStruct((grp.rows, D_MODEL), BF16),
        grid=(grp.rows // tm,),
        in_specs=[
            pl.BlockSpec((tm, D_MODEL), lambda i: (i, 0)),
            pl.BlockSpec((1, D_MODEL), lambda i: (0, 0)),
            grp.mod_spec(k_sc, tm),
            grp.mod_spec(k_sh, tm),
        ],
        out_specs=pl.BlockSpec((tm, D_MODEL), lambda i: (i, 0)),
        compiler_params=_cparams(("arbitrary",)),
    )(x, g.reshape(1, D_MODEL), grp.mod4, grp.mod4)


def _mm_kernel(a_ref, w_ref, o_ref):
    o_ref[...] = _dot(a_ref[...], w_ref[...])


def _matmul(a, w3, l, col0, n, tm, tn):
    rows, k = a.shape
    c0 = col0 // tn
    return pl.pallas_call(
        _mm_kernel,
        out_shape=jax.ShapeDtypeStruct((rows, n), F32),
        grid=(n // tn, rows // tm),
        in_specs=[
            pl.BlockSpec((tm, k), lambda j, i: (i, 0)),
            pl.BlockSpec((None, k, tn), lambda j, i: (l, 0, c0 + j)),
        ],
        out_specs=pl.BlockSpec((tm, tn), lambda j, i: (i, j)),
        compiler_params=_cparams(("arbitrary", "arbitrary")),
    )(a, w3)


W_RQ, W_CX, W_GM, W_PA, W_PA_PAD = 0, 2048, 3072, 6144, 2048


W_BLK = 256


def _arrange_kernel(src_ref, w_ref, o_ref):
    del src_ref
    j = pl.program_id(1)
    real = (W_PA + A_PROJ) // W_BLK

    @pl.when(j < real)
    def _():
        o_ref[...] = w_ref[...].astype(BF16)

    @pl.when(j >= real)
    def _():
        o_ref[...] = jnp.zeros_like(o_ref)


def _arrange_w_in(w_in):
    depth, k, n = w_in.shape
    nblk = n // W_BLK
    first = A_PROJ // W_BLK
    src = list(range(first, nblk)) + list(range(first))
    nout = (W_PA + W_PA_PAD) // W_BLK
    src = jnp.asarray(src + [0] * (nout - len(src)), jnp.int32)
    return pl.pallas_call(
        _arrange_kernel,
        out_shape=jax.ShapeDtypeStruct((depth, k, nout * W_BLK), BF16),
        grid_spec=pltpu.PrefetchScalarGridSpec(
            num_scalar_prefetch=1,
            grid=(depth, nout),
            in_specs=[pl.BlockSpec((None, k, W_BLK), lambda l, j, s: (l, 0, s[j]))],
            out_specs=pl.BlockSpec((None, k, W_BLK), lambda l, j, s: (l, 0, j))),
        compiler_params=_cparams(("arbitrary", "arbitrary")),
    )(src, w_in)


def _rwkv_pre(pa, prev, mu_ref, w0_ref, w2_ref, a0_ref, a2_ref, g2_ref, kk_ref, ka_ref, ones):
    pm = pa + (prev - pa) * mu_ref[...]
    r = pm[:, 0:512]
    k = pm[:, 512:1024]
    v = pm[:, 1024:1536]
    xwa = pm[:, 1536:1664]
    xg = pm[:, 1664:1792]
    w_log = -jax.nn.softplus(-(w0_ref[...] + _bdot(jnp.tanh(xwa), w2_ref[...]))) - 0.5
    logw = -jnp.exp(w_log)
    a = jax.nn.sigmoid(a0_ref[...] + _bdot(xwa, a2_ref[...]))
    g = _bdot(jax.nn.sigmoid(xg), g2_ref[...])
    kk = k * kk_ref[...]
    kk = kk * lax.rsqrt(_segsum(kk * kk, ones) + A_KK_EPS)
    k2 = k * (1.0 + (a - 1.0) * ka_ref[...])
    return r, logw, k2, v, -kk, kk * a, g


def _rwkv_post(o, r, k2, v, g, rk_ref, lng_ref, lnb_ref, ones):
    mean = _segsum(o, ones) * (1.0 / A_HEAD)
    oc = o - mean
    var = _segsum(oc * oc, ones) * (1.0 / A_HEAD)
    o = oc * lax.rsqrt(var + A_NORM_EPS) * lng_ref[...] + lnb_ref[...]
    bonus = _segsum(r * k2 * rk_ref[...], ones) * v
    return (o + bonus) * g


_NN = (((1,), (0,)), ((), ()))
_NT = (((1,), (1,)), ((), ()))
_TN = (((0,), (0,)), ((), ()))
RW_CHUNK = 64
RW_PAIRS = A_HEADS // 2


def _bdg(a, b, dims):
    return lax.dot_general(a.astype(BF16), b.astype(BF16), dims, preferred_element_type=F32)


def _rwkv_chunk_kernel(nck,
                       pa_ref, sh0_ref, g0_ref, mu_ref, w0_ref, w2_ref, a0_ref, a2_ref, g2_ref,
                       kk_ref, ka_ref, rk_ref, lng_ref, lnb_ref, ones_ref, tri_ref,
                       o_ref, gout_ref,
                       carry_ref, prev_ref, g_s, wu_s, m_s, zy_s, o_s):
    c = pl.program_id(1)
    ck = RW_CHUNK
    ones = ones_ref[...]

    @pl.when(c == 0)
    def _():
        carry_ref[...] = sh0_ref[...]
        g_s[...] = g0_ref[...]

    pa = pa_ref[...]
    rows = pa.shape[0]
    prev_ref[...] = pltpu.roll(pa, 1, axis=0)
    prev_ref[pl.ds(0, 1), :] = carry_ref[...]
    carry_ref[...] = pa[rows - 1:rows, :]
    r, logw, k2, v, an, bn, g = _rwkv_pre(pa, prev_ref[...], mu_ref, w0_ref, w2_ref, a0_ref, a2_ref,
                                          g2_ref, kk_ref, ka_ref, ones)

    tri = tri_ref[...]
    hi, mid, lo = _split3(logw)
    cum =(_dot(tri, lo) + _dot(tri, mid)) + _dot(tri, hi)
    cum_last = jnp.concatenate(
        [jnp.broadcast_to(cum[(i + 1) * ck - 1:(i + 1) * ck, :], (ck, D_BRANCH)) for i in range(nck)], axis=0)
    gam = jnp.exp(cum)
    inv = jnp.exp(-cum)
    to_end = jnp.exp(cum_last - cum)
    a_t = an * jnp.exp(cum - logw)
    b_t = bn * inv
    k_t = k2 * inv
    r_t = r * gam
    b_e = bn * to_end
    k_e = k2 * to_end

    lane = lax.broadcasted_iota(jnp.int32, (ck, LANES), 1)
    rowi = lax.broadcasted_iota(jnp.int32, (ck, LANES), 0)
    m0 = lane < A_HEAD
    coli = lane & (A_HEAD - 1)
    strict = rowi > coli
    incl = rowi >= coli
    eye_p = (rowi == coli).astype(F32)
    r128 = lax.broadcasted_iota(jnp.int32, (LANES, LANES), 0)
    c128 = lax.broadcasted_iota(jnp.int32, (LANES, LANES), 1)
    blockmask = (r128 < A_HEAD) == (c128 < A_HEAD)
    eye128 = r128 == c128
    bk_t = jnp.concatenate([b_e, k_e], axis=1).T

    def bd(q):
        q = q.astype(BF16)
        z = jnp.zeros_like(q)
        return jnp.concatenate([jnp.where(m0, q, z), jnp.where(m0, z, q)], axis=0)

    probs = [(i, p) for i in range(nck) for p in range(RW_PAIRS)]
    sl = {(i, p): (slice(i * ck, (i + 1) * ck), slice(p * LANES, (p + 1) * LANES)) for i, p in probs}
    l_pow, l_ak, t_inv = {}, {}, {}
    for q in probs:
        rs, ls = sl[q]
        lhs = jnp.concatenate([a_t[rs, ls], r_t[rs, ls]], axis=0).astype(BF16)
        ab = lax.dot_general(lhs, bd(b_t[rs, ls]), _NT, preferred_element_type=F32)
        ak = lax.dot_general(lhs, bd(k_t[rs, ls]), _NT, preferred_element_type=F32)
        l_pow[q] = jnp.where(strict, ab[:ck], 0.0)
        l_ak[q] = jnp.where(strict, ak[:ck], 0.0)
        m_s[q[0], q[1], :, 0:LANES] = jnp.where(incl, ab[ck:], 0.0)
        m_s[q[0], q[1], :, LANES:2 * LANES] = jnp.where(incl, ak[ck:], 0.0)
        t_inv[q] = eye_p + l_pow[q]
    n = 1
    while 2 * n < ck:
        for q in probs:
            l_pow[q] = _bdg(l_pow[q], bd(l_pow[q]), _NN)
        for q in probs:
            t_inv[q] = t_inv[q] + _bdg(t_inv[q], bd(l_pow[q]), _NN)
        n *= 2
    lak_v = {}
    for q in probs:
        rs, ls = sl[q]
        lak_v[q] = _bdg(l_ak[q], bd(v[rs, ls]), _NN)
    for q in probs:
        rs, ls = sl[q]
        wu = _bdg(t_inv[q], jnp.concatenate([bd(a_t[rs, ls]), bd(lak_v[q])], axis=1), _NN)
        wu_s[q[0], q[1]] = wu
        i, p = q
        half = jnp.zeros((ck, LANES), F32)
        place = (lambda x: jnp.concatenate([x, half], axis=0)) if i % 2 == 0 else \
                (lambda x: jnp.concatenate([half, x], axis=0))
        tcols = slice((i // 2) * LANES, (i // 2 + 1) * LANES)
        b_tr = bk_t[p * LANES:(p + 1) * LANES, tcols]
        k_tr = bk_t[D_BRANCH + p * LANES:D_BRANCH + (p + 1) * LANES, tcols]
        z_t = _bdg(b_tr, place(wu[:, 0:LANES]), _NN)
        y_t = _bdg(jnp.concatenate([b_tr, k_tr], axis=1),
                   jnp.concatenate([place(wu[:, LANES:2 * LANES]), place(v[rs, ls])], axis=0), _NN)
        zy_s[i, p, :, 0:LANES] = jnp.where(blockmask, z_t, 0.0)
        zy_s[i, p, :, LANES:2 * LANES] = jnp.where(blockmask, y_t, 0.0)

    for i in range(nck):
        rs = slice(i * ck, (i + 1) * ck)
        for p in range(RW_PAIRS):
            ls = slice(p * LANES, (p + 1) * LANES)
            h_p = g_s[p]
            h_bf = h_p.astype(BF16)
            gcol = jnp.sum(jnp.where(eye128, gam[(i + 1) * ck - 1:(i + 1) * ck, ls], 0.0), axis=1, keepdims=True)
            g_s[p] = (h_p * gcol + _dot(zy_s[i, p, :, 0:LANES].astype(BF16), h_bf)) + zy_s[i, p, :, LANES:2 * LANES]
            wu = wu_s[i, p]
            u = _dot(wu[:, 0:LANES].astype(BF16), h_bf) + wu[:, LANES:2 * LANES]
            o_s[rs, ls] = _dot(r_t[rs, ls].astype(BF16), h_bf) \
                + _bdg(m_s[i, p], jnp.concatenate([bd(u), bd(v[rs, ls])], axis=0), _NN)

    o_ref[...] = _rwkv_post(o_s[...], r, k2, v, g, rk_ref, lng_ref, lnb_ref, ones).astype(o_ref.dtype)
    gout_ref[...] = g_s[...]


def _rwkv_chunked(pa, shift0, s0, prm, batch, seq, nck):
    lb = nck * RW_CHUNK
    nstep = seq // lb
    s0p = s0.reshape(batch, RW_PAIRS, 2, A_HEAD, A_HEAD)
    eye2 = jnp.eye(2, dtype=s0.dtype)
    g0 = jnp.einsum("bpjvk,ji->bpjkiv", s0p, eye2).reshape(batch, RW_PAIRS, LANES, LANES)
    tri = np.kron(np.eye(nck), np.tril(np.ones((RW_CHUNK, RW_CHUNK)))).astype(np.float32)
    params = [prm[n] for n in ("mu", "w0", "w2", "a0", "a2", "g2", "kk", "ka", "rk", "lng", "lnb", "ones")]
    params.append(jnp.asarray(tri, BF16))

    def full(arr):
        nd = arr.ndim
        return pl.BlockSpec(arr.shape, lambda i, c: (0,) * nd)

    o, g_out = pl.pallas_call(
        functools.partial(_rwkv_chunk_kernel, nck),
        out_shape=(jax.ShapeDtypeStruct((batch * seq, D_BRANCH), BF16),
                   jax.ShapeDtypeStruct((batch, RW_PAIRS, LANES, LANES), F32)),
        grid=(batch, nstep),
        in_specs=[pl.BlockSpec((lb, A_PROJ), lambda i, c: (i * nstep + c, 0)),
                  pl.BlockSpec((None, 1, A_PROJ), lambda i, c: (i, 0, 0)),
                  pl.BlockSpec((None, RW_PAIRS, LANES, LANES), lambda i, c: (i, 0, 0, 0))]
                 + [full(p) for p in params],
        out_specs=(pl.BlockSpec((lb, D_BRANCH), lambda i, c: (i * nstep + c, 0)),
                   pl.BlockSpec((None, RW_PAIRS, LANES, LANES), lambda i, c: (i, 0, 0, 0))),
        scratch_shapes=[pltpu.VMEM((1, A_PROJ), F32),
                        pltpu.VMEM((lb, A_PROJ), F32),
                        pltpu.VMEM((RW_PAIRS, LANES, LANES), F32),
                        pltpu.VMEM((nck, RW_PAIRS, RW_CHUNK, 2 * LANES), F32),
                        pltpu.VMEM((nck, RW_PAIRS, RW_CHUNK, 2 * LANES), F32),
                        pltpu.VMEM((nck, RW_PAIRS, LANES, 2 * LANES), F32),
                        pltpu.VMEM((lb, D_BRANCH), F32)],
        compiler_params=_cparams(("arbitrary", "arbitrary")),
    )(pa, shift0.reshape(batch, 1, A_PROJ), g0, *params)
    g5 = g_out.reshape(batch, RW_PAIRS, 2, A_HEAD, 2, A_HEAD)
    s_new = jnp.stack([g5[:, :, 0, :, 0, :], g5[:, :, 1, :, 1, :]], axis=2).swapaxes(-1, -2)
    return o, s_new.reshape(batch, A_HEADS, A_HEAD, A_HEAD)


def _rwkv_step_kernel(nb, lb,
                      pa_ref, sh0_ref, s0_ref, mu_ref, w0_ref, w2_ref, a0_ref, a2_ref, g2_ref,
                      kk_ref, ka_ref, rk_ref, lng_ref, lnb_ref, ones_ref,
                      o_ref, sout_ref):
    ones = ones_ref[...]
    pa = pa_ref[...].reshape(lb * nb, A_PROJ)
    prev = jnp.concatenate([sh0_ref[...], pa[:(lb - 1) * nb, :]], axis=0)
    r, logw, k2, v, an, bn, g = _rwkv_pre(pa, prev, mu_ref, w0_ref, w2_ref, a0_ref, a2_ref,
                                          g2_ref, kk_ref, ka_ref, ones)
    w = jnp.exp(logw)
    srows = nb * A_HEAD
    rowi = lax.broadcasted_iota(jnp.int32, (srows, D_BRANCH), 0)
    lane = lax.broadcasted_iota(jnp.int32, (srows, D_BRANCH), 1)
    eye = (rowi & (A_HEAD - 1)) == (lane & (A_HEAD - 1))

    def per_seq(x, t):
        xt = x[t * nb:(t + 1) * nb, :]
        return jnp.concatenate([jnp.broadcast_to(xt[b:b + 1, :], (A_HEAD, D_BRANCH)) for b in range(nb)],
                               axis=0)

    s0 = s0_ref[...]
    s = jnp.concatenate([s0[:, h] for h in range(A_HEADS)], axis=-1).reshape(srows, D_BRANCH)
    outs = []
    for t in range(lb):
        sa = _segsum(s * per_seq(an, t), ones)
        vcol = _segsum(jnp.where(eye, per_seq(v, t), 0.0), ones)
        s = s * per_seq(w, t) + sa * per_seq(bn, t) + vcol * per_seq(k2, t)
        out = _segsum(s * per_seq(r, t), ones)
        outs.append(jnp.sum(jnp.where(eye, out, 0.0).reshape(nb, A_HEAD, D_BRANCH), axis=1))
    o = jnp.concatenate(outs, axis=0)
    res = _rwkv_post(o, r, k2, v, g, rk_ref, lng_ref, lnb_ref, ones).astype(o_ref.dtype)
    o_ref[...] = res.reshape(lb, nb, D_BRANCH)
    s3 = s.reshape(nb, A_HEAD, D_BRANCH)
    for h in range(A_HEADS):
        sout_ref[:, h] = s3[:, :, h * A_HEAD:(h + 1) * A_HEAD]


def _rwkv_steps(pa, shift0, s0, prm, batch, seq, nb):
    pa_tm = pa.reshape(batch, seq, pa.shape[1]).transpose(1, 0, 2)
    st_spec = pl.BlockSpec((nb, A_HEADS, A_HEAD, A_HEAD), lambda i: (i, 0, 0, 0))

    def full(arr):
        nd = arr.ndim
        return pl.BlockSpec(arr.shape, lambda i: (0,) * nd)

    params = [prm[n] for n in ("mu", "w0", "w2", "a0", "a2", "g2", "kk", "ka", "rk", "lng", "lnb", "ones")]
    o, s_out = pl.pallas_call(
        functools.partial(_rwkv_step_kernel, nb, seq),
        out_shape=(jax.ShapeDtypeStruct((seq, batch, D_BRANCH), BF16),
                   jax.ShapeDtypeStruct((batch, A_HEADS, A_HEAD, A_HEAD), F32)),
        grid=(batch // nb,),
        in_specs=[pl.BlockSpec((seq, nb, A_PROJ), lambda i: (0, i, 0)),
                  pl.BlockSpec((nb, A_PROJ), lambda i: (i, 0)),
                  st_spec]
                 + [full(p) for p in params],
        out_specs=(pl.BlockSpec((seq, nb, D_BRANCH), lambda i: (0, i, 0)), st_spec),
        compiler_params=_cparams(("arbitrary",)),
    )(pa_tm, shift0, s0, *params)
    return o.transpose(1, 0, 2).reshape(batch * seq, D_BRANCH), s_out


def _ret_kernel(nb, lb, nck,
                rq_ref, cos_ref, sin_ref, dm_ref, kd_ref, qd_ref, cd_ref, s0_ref,
                o_ref, sout_ref, s_ref):
    c = pl.program_id(1)

    @pl.when(c == 0)
    def _():
        s_ref[...] = s0_ref[...]

    scale = R_HEAD ** -0.5
    half = R_HEAD // 2
    row8 = lax.broadcasted_iota(jnp.int32, (8, R_HEAD), 0)
    pre = {}
    for ci in range(nck):
        rs = slice(ci * RET_ROWS, (ci + 1) * RET_ROWS)
        cos = cos_ref[rs, :]
        sin = sin_ref[rs, :]
        for h in range(R_HEADS):
            lo, hi = h * R_HEAD, (h + 1) * R_HEAD
            q = rq_ref[rs, lo:hi]
            k = rq_ref[rs, 512 + lo:512 + hi]
            v = rq_ref[rs, 1024 + lo:1024 + hi]
            qh = q * cos + pltpu.roll(q, half, axis=1) * sin
            kh = (k * cos + pltpu.roll(k, half, axis=1) * sin) * scale
            scores = lax.dot_general(qh.astype(BF16), kh.astype(BF16), (((1,), (1,)), ((), ())),
                                     preferred_element_type=F32) * dm_ref[h]
            pre[ci, h] = (_bdot(scores, v), qh * qd_ref[:, lo:hi], kh * kd_ref[:, lo:hi], v)
    for ci, h in sorted(pre):
        rs = slice(ci * RET_ROWS, (ci + 1) * RET_ROWS)
        lo, hi = h * R_HEAD, (h + 1) * R_HEAD
        o, qd, ku, v = pre[ci, h]
        gate = rq_ref[rs, 1536 + lo:1536 + hi]
        cd = cd_ref[h]
        if nb == 1:
            s = s_ref[0, h]
            o = o + _bdot(qd, s)
            upd = lax.dot_general(ku.astype(BF16), v.astype(BF16), (((0,), (0,)), ((), ())),
                                  preferred_element_type=F32)
            s_ref[0, h] = s * cd + upd
        else:
            per_tile = 8 // lb
            inter = []
            for i in range(RET_ROWS // 8):
                qd_t = qd[i * 8:(i + 1) * 8, :]
                ku_t = ku[i * 8:(i + 1) * 8, :]
                v_t = v[i * 8:(i + 1) * 8, :].astype(BF16)
                acc = None
                for j in range(per_tile):
                    b = i * per_tile + j
                    m = (row8 >= j * lb) & (row8 < (j + 1) * lb)
                    s = s_ref[b, h]
                    part = _bdot(jnp.where(m, qd_t, 0.0), s)
                    acc = part if acc is None else acc + part
                    upd = lax.dot_general(jnp.where(m, ku_t, 0.0).astype(BF16), v_t,
                                          (((0,), (0,)), ((), ())), preferred_element_type=F32)
                    s_ref[b, h] = s * cd + upd
                inter.append(acc)
            o = o + jnp.concatenate(inter, axis=0)
        oc = o - jnp.mean(o, axis=-1, keepdims=True)
        on = oc * lax.rsqrt(jnp.mean(oc * oc, axis=-1, keepdims=True) + R_NORM_EPS)
        o_ref[rs, lo:hi] = (gate * jax.nn.sigmoid(gate) * on).astype(o_ref.dtype)
    sout_ref[...] = s_ref[...]


def _retention(rq, s0_all, l, tabs, batch, seq):
    lb = min(R_CHUNK, seq)
    nb = RET_ROWS // lb
    nck = RET_CHUNKS_PER_STEP if (nb == 1 and (seq // lb) % RET_CHUNKS_PER_STEP == 0) else 1
    rows = RET_ROWS * nck
    nchunk = seq // (lb * nck)
    cos, sin, dm, kd, qd, cd = tabs
    ntab = cos.shape[0] // rows

    def const(arr):
        nd = arr.ndim
        return pl.BlockSpec(arr.shape, lambda i, c: (0,) * nd)

    tab_idx = (lambda i, c: (c, 0)) if ntab > 1 else (lambda i, c: (0, 0))
    return pl.pallas_call(
        functools.partial(_ret_kernel, nb, lb, nck),
        out_shape=(jax.ShapeDtypeStruct((batch * seq, D_BRANCH), BF16),
                   jax.ShapeDtypeStruct((batch, R_HEADS, R_HEAD, R_HEAD), F32)),
        grid=(batch // nb, nchunk),
        in_specs=[pl.BlockSpec((rows, 4 * D_BRANCH), lambda i, c: (i * nchunk + c, 0)),
                  pl.BlockSpec((rows, R_HEAD), tab_idx),
                  pl.BlockSpec((rows, R_HEAD), tab_idx),
                  const(dm), const(kd), const(qd), const(cd),
                  pl.BlockSpec((None, nb, R_HEADS, R_HEAD, R_HEAD), lambda i, c: (l, i, 0, 0, 0))],
        out_specs=(pl.BlockSpec((rows, D_BRANCH), lambda i, c: (i * nchunk + c, 0)),
                   pl.BlockSpec((nb, R_HEADS, R_HEAD, R_HEAD), lambda i, c: (i, 0, 0, 0))),
        scratch_shapes=[pltpu.VMEM((nb, R_HEADS, R_HEAD, R_HEAD), F32)],
        compiler_params=_cparams(("arbitrary", "arbitrary")),
    )(rq, cos, sin, dm, kd, qd, cd, s0_all)


def _ret_tables(seq, pos0):
    lb = min(R_CHUNK, seq)
    nb = RET_ROWS // lb
    half = R_HEAD // 2
    pos = pos0 + jnp.arange(seq, dtype=jnp.int32)
    inv = ROPE_BASE ** (-jnp.arange(half, dtype=F32) / half)
    ang = pos.astype(F32)[:, None] * inv[None, :]
    cos, sin = jnp.cos(ang), jnp.sin(ang)
    cos2 = jnp.concatenate([cos, cos], axis=-1)
    sin2 = jnp.concatenate([-sin, sin], axis=-1)
    if nb > 1:
        cos2 = jnp.tile(cos2, (nb, 1))
        sin2 = jnp.tile(sin2, (nb, 1))
    log_g = jnp.log1p(-jnp.exp2(-5.0 - jnp.arange(R_HEADS, dtype=F32)))
    idx = jnp.arange(lb, dtype=F32)
    diff = idx[:, None] - idx[None, :]
    dmask = jnp.where(diff >= 0, jnp.exp(log_g[:, None, None] * jnp.maximum(diff, 0.0)), 0.0)
    if nb > 1:
        dmask = jnp.einsum("ab,hij->haibj", jnp.eye(nb, dtype=F32), dmask).reshape(
            R_HEADS, RET_ROWS, RET_ROWS)
    k_decay = jnp.exp(log_g[None, :] * (lb - 1.0 - idx)[:, None])
    q_decay = jnp.exp(log_g[None, :] * (idx + 1.0)[:, None])
    kd = jnp.tile(jnp.repeat(k_decay, R_HEAD, axis=1), (nb, 1))
    qd = jnp.tile(jnp.repeat(q_decay, R_HEAD, axis=1), (nb, 1))
    cd = jnp.broadcast_to(jnp.exp(log_g * lb)[:, None, None], (R_HEADS, 1, R_HEAD))
    return cos2, sin2, dmask, kd, qd, cd


def _lru_kernel(nb, lb, three_d,
                cx_ref, conv0_ref, h0_ref, cw_ref, cb_ref, wri_ref, br_ref, bi_ref, lam_ref,
                o_ref, hout_ref,
                cc_ref, hc_ref, x1_ref, x2_ref, x3_ref, a_ref, b_ref):
    c = pl.program_id(1)
    rows = nb * lb

    @pl.when(c == 0)
    def _():
        cc_ref[...] = conv0_ref[...]
        hc_ref[...] = h0_ref[...]

    cx = cx_ref[...]
    if three_d:
        cx = cx.reshape(rows, 2 * D_BRANCH)
    xb = cx[:, :D_BRANCH]
    gb = cx[:, D_BRANCH:]
    x1_ref[...] = pltpu.roll(xb, 1, axis=0)
    x2_ref[...] = pltpu.roll(xb, 2, axis=0)
    x3_ref[...] = pltpu.roll(xb, 3, axis=0)
    for b in range(nb):
        r0 = b * lb
        c0 = cc_ref[b, 0:1, :]
        c1 = cc_ref[b, 1:2, :]
        c2 = cc_ref[b, 2:3, :]
        x1_ref[pl.ds(r0, 1), :] = c2
        x2_ref[pl.ds(r0, 1), :] = c1
        x2_ref[pl.ds(r0 + 1, 1), :] = c2
        x3_ref[pl.ds(r0, 1), :] = c0
        x3_ref[pl.ds(r0 + 1, 1), :] = c1
        x3_ref[pl.ds(r0 + 2, 1), :] = c2
        cc_ref[b] = xb[r0 + lb - 3:r0 + lb, :]
    cw = cw_ref[...]
    xc = cb_ref[...] + (((x3_ref[...] * cw[0:1, :] + x2_ref[...] * cw[1:2, :]) + x1_ref[...] * cw[2:3, :])
                        + xb * cw[3:4, :])
    ri = _bdot(xc, wri_ref[...])
    r = jax.nn.sigmoid(ri[:, :D_BRANCH] + br_ref[...])
    i = jax.nn.sigmoid(ri[:, D_BRANCH:] + bi_ref[...])
    log_a = LRU_C * r * jax.nn.log_sigmoid(lam_ref[...])
    a = jnp.exp(log_a)
    bb = jnp.sqrt(-jnp.tanh(log_a) * (a * a + 1.0)) * (i * xc)
    a_ref[...] = a
    b_ref[...] = bb
    for b in range(nb):
        r0 = b * lb
        b_ref[pl.ds(r0, 1), :] = bb[r0:r0 + 1, :] + a[r0:r0 + 1, :] * hc_ref[b]
    a = a_ref[...]
    bb = b_ref[...]
    t_idx = lax.broadcasted_iota(jnp.int32, (rows, D_BRANCH), 0) % lb
    s = 1
    while s < lb:
        keep = t_idx >= s
        a_sh = jnp.where(keep, pltpu.roll(a, s, axis=0), 1.0)
        b_sh = jnp.where(keep, pltpu.roll(bb, s, axis=0), 0.0)
        bb = a * b_sh + bb
        a = a * a_sh
        s *= 2
    h = bb
    for b in range(nb):
        r0 = b * lb
        hc_ref[b] = h[r0 + lb - 1:r0 + lb, :]
    res = (h * jax.nn.gelu(gb)).astype(o_ref.dtype)
    if three_d:
        res = res.reshape(nb, lb, D_BRANCH)
    o_ref[...] = res
    hout_ref[...] = hc_ref[...]


def _rglru(cx, conv0, h0, prm, batch, seq, nb, lb):
    nchunk = seq // lb
    rows = nb * lb
    three_d = nb > 1 and nchunk > 1
    if three_d:
        cx_in = cx.reshape(batch, seq, 2 * D_BRANCH)
        cx_spec = pl.BlockSpec((nb, lb, 2 * D_BRANCH), lambda i, c: (i, c, 0))
        o_shape = jax.ShapeDtypeStruct((batch, seq, D_BRANCH), BF16)
        o_spec = pl.BlockSpec((nb, lb, D_BRANCH), lambda i, c: (i, c, 0))
    else:
        cx_in = cx
        cx_spec = pl.BlockSpec((rows, 2 * D_BRANCH), lambda i, c: (i * nchunk + c, 0))
        o_shape = jax.ShapeDtypeStruct((batch * seq, D_BRANCH), BF16)
        o_spec = pl.BlockSpec((rows, D_BRANCH), lambda i, c: (i * nchunk + c, 0))

    def full(arr):
        nd = arr.ndim
        return pl.BlockSpec(arr.shape, lambda i, c: (0,) * nd)

    params = [prm[n] for n in ("cw", "cb", "wri", "br", "bi", "lam")]
    o, h_out = pl.pallas_call(
        functools.partial(_lru_kernel, nb, lb, three_d),
        out_shape=(o_shape, jax.ShapeDtypeStruct((batch, 1, D_BRANCH), F32)),
        grid=(batch // nb, nchunk),
        in_specs=[cx_spec,
                  pl.BlockSpec((nb, CONV_W - 1, D_BRANCH), lambda i, c: (i, 0, 0)),
                  pl.BlockSpec((nb, 1, D_BRANCH), lambda i, c: (i, 0, 0))]
                 + [full(p) for p in params],
        out_specs=(o_spec, pl.BlockSpec((nb, 1, D_BRANCH), lambda i, c: (i, 0, 0))),
        scratch_shapes=[pltpu.VMEM((nb, CONV_W - 1, D_BRANCH), F32),
                        pltpu.VMEM((nb, 1, D_BRANCH), F32)]
                       + [pltpu.VMEM((rows, D_BRANCH), F32) for _ in range(5)],
        compiler_params=_cparams(("arbitrary", "arbitrary")),
    )(cx_in, conv0, h0.reshape(batch, 1, D_BRANCH), *params)
    return o.reshape(batch * seq, D_BRANCH), h_out.reshape(batch, D_BRANCH)


def _merge_kernel(h_ref, oa_ref, ob_ref, oc_ref, x_ref, gt_ref, wgm_ref, wb_ref, wout_ref, o_ref):
    gm = _dot(h_ref[...], wgm_ref[...])
    merged = None
    for n, br_ref in enumerate((oa_ref, ob_ref, oc_ref)):
        br = _dot(br_ref[...], wb_ref[n])
        term = jax.nn.sigmoid(gm[:, n * D_MODEL:(n + 1) * D_MODEL]) * br
        merged = term if merged is None else merged + term
    y = _bdot(merged, wout_ref[...])
    o_ref[...] = x_ref[...] + gt_ref[...] * y


def _merge_out(h, oa, ob, oc, x, grp, w3, l, wb, wout, tm):
    row = lambda i: (i, 0)
    return pl.pallas_call(
        _merge_kernel,
        out_shape=jax.ShapeDtypeStruct((grp.rows, D_MODEL), F32),
        grid=(grp.rows // tm,),
        in_specs=[pl.BlockSpec((tm, D_MODEL), row),
                  pl.BlockSpec((tm, D_BRANCH), row),
                  pl.BlockSpec((tm, D_BRANCH), row),
                  pl.BlockSpec((tm, D_BRANCH), row),
                  pl.BlockSpec((tm, D_MODEL), row),
                  grp.mod_spec(2, tm),
                  pl.BlockSpec((None, D_MODEL, 3 * D_MODEL), lambda i: (l, 0, W_GM // (3 * D_MODEL))),
                  pl.BlockSpec(wb.shape, lambda i: (0, 0, 0)),
                  pl.BlockSpec(wout.shape, lambda i: (0, 0))],
        out_specs=pl.BlockSpec((tm, D_MODEL), row),
        compiler_params=_cparams(("arbitrary",)),
    )(h, oa, ob, oc, x, grp.mod4, w3, wb, wout)


def _group_member(x, k, lane):
    ge = EXPERTS_PER_GROUP
    pos = lane & (ge - 1)
    return jnp.where(pos + k < ge, pltpu.roll(x, LANES - k, axis=1), pltpu.roll(x, ge - k, axis=1))


def _in_group_top2(sel, lane):
    ge = EXPERTS_PER_GROUP
    pos = lane & (ge - 1)
    n_ahead = jnp.zeros(sel.shape, jnp.int32)
    for k in range(1, ge):
        other = _group_member(sel, k, lane)
        lower_index = pos + k >= ge
        n_ahead = n_ahead + ((other > sel) | ((other == sel) & lower_index)).astype(jnp.int32)
    return n_ahead < TOP_K


def _best_group(sel, in_top2, lane):
    ge = EXPERTS_PER_GROUP
    kept = jnp.where(in_top2, sel, 0.0)
    score = kept
    for k in range(1, ge):
        score = score + _group_member(kept, k, lane)
    best = lane < N_EXPERTS
    for d in range(1, N_GROUPS):
        later = pltpu.roll(score, LANES - ge * d, axis=1)
        earlier = pltpu.roll(score, ge * d, axis=1)
        best = best & (later <= score) & (earlier < score)
    return best


def _router_probs(h_bf16, wr_ref, rb_ref, lane):
    logits = _dot(h_bf16, wr_ref[...])
    logits = jnp.where(lane < N_EXPERTS, logits, -jnp.inf)
    mx = jnp.max(logits, axis=-1, keepdims=True)
    ex = jnp.exp(logits - mx)
    probs = ex / jnp.sum(ex, axis=-1, keepdims=True)
    sel = jnp.where(lane < N_EXPERTS, probs + rb_ref[...], -jnp.inf)
    return probs, sel


def _route(probs, sel, lane):
    in_top2 = _in_group_top2(sel, lane)
    pk = jnp.where(_best_group(sel, in_top2, lane) & in_top2, probs, 0.0)
    return pk / jnp.sum(pk, axis=-1, keepdims=True)


def _moe_kernel(final, x_ref, g_ref, sc_ref, sh_ref, gt_ref, gf_ref, wr_ref, rb_ref, wg_ref, wu_ref, wd_ref,
                o_ref, h_s, gate_s):
    e = pl.program_id(1)
    tm = x_ref.shape[0]
    lane = lax.broadcasted_iota(jnp.int32, (tm, LANES), 1)

    @pl.when(e == 0)
    def _():
        y = _rms(x_ref[...], g_ref[...])
        h = (y * (1.0 + sc_ref[...]) + sh_ref[...]).astype(BF16)
        h_s[...] = h
        probs, selv = _router_probs(h, wr_ref, rb_ref, lane)
        gate_s[...] = _route(probs, selv, lane)
        o_ref[...] = jnp.zeros_like(o_ref)

    h = h_s[...]
    gcol = jnp.sum(jnp.where(lane == e, gate_s[...], 0.0), axis=-1, keepdims=True)
    hg = _bdot(h, wg_ref[...])
    hu = _bdot(h, wu_ref[...])
    act = hg * jax.nn.sigmoid(hg) * hu * gcol
    o_ref[...] += _bdot(act, wd_ref[...])

    @pl.when(e == N_EXPERTS - 1)
    def _():
        res = x_ref[...] + gt_ref[...] * o_ref[...]
        o_ref[...] = _rms(res, gf_ref[...]) if final else res


def _moe(x, g, grp, wr, rb, wg, wu, wd, l, tm, gf=None):
    row = lambda i, e: (i, 0)
    final = gf is not None
    gain = (gf if final else g).reshape(1, D_MODEL)
    return pl.pallas_call(
        functools.partial(_moe_kernel, final),
        out_shape=jax.ShapeDtypeStruct((grp.rows, D_MODEL), F32),
        grid=(grp.rows // tm, N_EXPERTS),
        in_specs=[pl.BlockSpec((tm, D_MODEL), row),
                  pl.BlockSpec((1, D_MODEL), lambda i, e: (0, 0)),
                  grp.mod_spec(4, tm),
                  grp.mod_spec(3, tm),
                  grp.mod_spec(5, tm),
                  pl.BlockSpec((1, D_MODEL), lambda i, e: (0, 0)),
                  pl.BlockSpec((D_MODEL, LANES), lambda i, e: (0, 0)),
                  pl.BlockSpec((1, LANES), lambda i, e: (0, 0)),
                  pl.BlockSpec((None, None, D_MODEL, D_EXPERT), lambda i, e: (l, e, 0, 0)),
                  pl.BlockSpec((None, None, D_MODEL, D_EXPERT), lambda i, e: (l, e, 0, 0)),
                  pl.BlockSpec((None, None, D_EXPERT, D_MODEL), lambda i, e: (l, e, 0, 0))],
        out_specs=pl.BlockSpec((tm, D_MODEL), row),
        scratch_shapes=[pltpu.VMEM((tm, D_MODEL), BF16),
                        pltpu.VMEM((tm, LANES), F32)],
        compiler_params=_cparams(("arbitrary", "arbitrary"), MOE_VMEM_LIMIT),
    )(x, g.reshape(1, D_MODEL), grp.mod4, grp.mod4, grp.mod4, gain, wr, rb, wg, wu, wd)


MOE_SORT_TILE = 1024
MOE_DMA_ROWS = 2048
DMA_ISSUE_GROUP = 8


def _moe_router_kernel(x_ref, g_ref, sc_ref, sh_ref, wrt_ref, rbt_ref, triu_ref,
                       h_ref, gr_ref, cnt_ref, base_s):
    i = pl.program_id(0)
    tm = x_ref.shape[0]
    ge = EXPERTS_PER_GROUP

    @pl.when(i == 0)
    def _():
        base_s[...] = jnp.zeros_like(base_s)

    y = _rms(x_ref[...], g_ref[...])
    h = y * (1.0 + sc_ref[...]) + sh_ref[...]
    h_ref[...] = h
    logits = lax.dot_general(wrt_ref[...], h.astype(BF16), _NT, preferred_element_type=F32)
    ex = jnp.exp(logits - jnp.max(logits, axis=0, keepdims=True))
    probs = ex / jnp.sum(ex, axis=0, keepdims=True)
    sel = probs + rbt_ref[:, 0:1]
    row = lax.broadcasted_iota(jnp.int32, (N_EXPERTS, tm), 0)
    pos = row & (ge - 1)
    grp_of = row >> 2

    def member(x, k):
        return jnp.where(pos + k < ge, pltpu.roll(x, N_EXPERTS - k, axis=0), pltpu.roll(x, ge - k, axis=0))

    n_ahead = jnp.zeros(sel.shape, jnp.int32)
    for k in range(1, ge):
        other = member(sel, k)
        n_ahead = n_ahead + ((other > sel) | ((other == sel) & (pos + k >= ge))).astype(jnp.int32)
    kept = jnp.where(n_ahead < TOP_K, sel, 0.0)
    score = kept
    for k in range(1, ge):
        score = score + member(kept, k)
    best = pos == 0
    for d in range(1, N_GROUPS):
        later = pltpu.roll(score, N_EXPERTS - ge * d, axis=0)
        earlier = pltpu.roll(score, ge * d, axis=0)
        best = best & ((grp_of + d >= N_GROUPS) | (later <= score)) & ((grp_of < d) | (earlier < score))
    gid = jnp.sum(jnp.where(best, grp_of, 0), axis=0, keepdims=True)
    onehot = (row == gid).astype(BF16)
    incl = _dot(onehot, triu_ref[...]) + base_s[:, 0:1]
    rank = jnp.sum(jnp.where(row == gid, incl - 1.0, 0.0), axis=0, keepdims=True).astype(jnp.int32)
    row8 = lax.broadcasted_iota(jnp.int32, (8, tm), 0)
    gr_ref[...] = jnp.where(row8 == 0, gid, jnp.where(row8 == 1, rank, 0))
    total = jnp.broadcast_to(incl[:, tm - 1:tm], (N_EXPERTS, LANES))
    base_s[...] = total
    cnt_ref[...] = total.astype(jnp.int32)


def _scatter_rows_kernel(dest_ref, src_ref, zeros_hbm, dst_hbm, sem):
    del zeros_hbm
    base = pl.program_id(0) * MOE_DMA_ROWS

    def issue(t8, carry):
        for k in range(DMA_ISSUE_GROUP):
            t = t8 * DMA_ISSUE_GROUP + k
            pltpu.make_async_copy(src_ref.at[pl.ds(t, 1)], dst_hbm.at[pl.ds(dest_ref[base + t], 1)],
                                  sem).start(priority=k % 2)
        return carry

    def drain(t, carry):
        pltpu.make_async_copy(src_ref.at[pl.ds(0, 1)], dst_hbm.at[pl.ds(0, 1)], sem).wait()
        return carry

    lax.fori_loop(0, MOE_DMA_ROWS // DMA_ISSUE_GROUP, issue, 0)
    lax.fori_loop(0, MOE_DMA_ROWS, drain, 0, unroll=8)


def _moe_group_kernel(tg_ref, nv_ref, xs_ref, wr_ref, rb_ref, wg_ref, wu_ref, wd_ref, ys_ref, h_s, gate_s):
    i = pl.program_id(0)
    j = pl.program_id(1)
    tm = xs_ref.shape[0]
    lane = lax.broadcasted_iota(jnp.int32, (tm, LANES), 1)
    grp = tg_ref[i]

    @pl.when((i >= nv_ref[0]) & (j == 0))
    def _():
        ys_ref[...] = jnp.zeros_like(ys_ref)

    @pl.when(i < nv_ref[0])
    def _():
        @pl.when(j == 0)
        def _():
            h = xs_ref[...].astype(BF16)
            h_s[...] = h
            probs, sel = _router_probs(h, wr_ref, rb_ref, lane)
            pk = jnp.where(_in_group_top2(sel, lane) & ((lane >> 2) == grp), probs, 0.0)
            psum = jnp.sum(pk, axis=-1, keepdims=True)
            gate_s[...] = pk / jnp.where(psum > 0.0, psum, 1.0)
            ys_ref[...] = jnp.zeros_like(ys_ref)

        h = h_s[...]
        gcol = jnp.sum(jnp.where(lane == grp * EXPERTS_PER_GROUP + j, gate_s[...], 0.0), axis=-1, keepdims=True)
        hg = _bdot(h, wg_ref[...])
        hu = _bdot(h, wu_ref[...])
        act = hg * jax.nn.sigmoid(hg) * hu * gcol
        ys_ref[...] += _bdot(act, wd_ref[...])


def _gather_residual_kernel(final, dest_ref, x_ref, gt_ref, gf_ref, ys_hbm, o_ref, buf, sem):
    i = pl.program_id(0)
    n = pl.num_programs(0)
    tm = x_ref.shape[0]

    def start_tile(tile, slot):
        def issue(t8, carry):
            for k in range(DMA_ISSUE_GROUP):
                t = t8 * DMA_ISSUE_GROUP + k
                pltpu.make_async_copy(ys_hbm.at[pl.ds(dest_ref[tile * tm + t], 1)], buf.at[slot, pl.ds(t, 1)],
                                      sem.at[slot]).start(priority=k % 2)
            return carry
        lax.fori_loop(0, tm // DMA_ISSUE_GROUP, issue, 0)

    @pl.when(i == 0)
    def _():
        start_tile(0, 0)

    @pl.when(i + 1 < n)
    def _():
        start_tile(i + 1, (i + 1) % 2)

    slot = i % 2

    def drain(t, carry):
        pltpu.make_async_copy(ys_hbm.at[pl.ds(0, 1)], buf.at[slot, pl.ds(0, 1)], sem.at[slot]).wait()
        return carry

    lax.fori_loop(0, tm, drain, 0, unroll=8)
    res = x_ref[...] + gt_ref[...] * buf[slot]
    if final:
        res = _rms(res, gf_ref[...])
    o_ref[...] = res


def _moe_sorted(x, g, grp, wr, rb, wg, wu, wd, l, tm, gf=None):
    rows = grp.rows
    te = MOE_SORT_TILE
    ntile = rows // te + N_GROUPS
    cap = ntile * te
    triu = jnp.asarray(np.triu(np.ones((tm, tm), np.float32)), BF16)
    wrt = wr[:, :N_EXPERTS].T
    rbt = jnp.broadcast_to(rb[0, :N_EXPERTS, None], (N_EXPERTS, LANES))
    row = lambda i: (i, 0)
    h, gr, cnt = pl.pallas_call(
        _moe_router_kernel,
        out_shape=(jax.ShapeDtypeStruct((rows, D_MODEL), F32),
                   jax.ShapeDtypeStruct((rows // tm, 8, tm), jnp.int32),
                   jax.ShapeDtypeStruct((N_EXPERTS, LANES), jnp.int32)),
        grid=(rows // tm,),
        in_specs=[pl.BlockSpec((tm, D_MODEL), row),
                  pl.BlockSpec((1, D_MODEL), lambda i: (0, 0)),
                  grp.mod_spec(4, tm),
                  grp.mod_spec(3, tm),
                  pl.BlockSpec((N_EXPERTS, D_MODEL), lambda i: (0, 0)),
                  pl.BlockSpec((N_EXPERTS, LANES), lambda i: (0, 0)),
                  pl.BlockSpec((tm, tm), lambda i: (0, 0))],
        out_specs=(pl.BlockSpec((tm, D_MODEL), row),
                   pl.BlockSpec((None, 8, tm), lambda i: (i, 0, 0)),
                   pl.BlockSpec((N_EXPERTS, LANES), lambda i: (0, 0))),
        scratch_shapes=[pltpu.VMEM((N_EXPERTS, LANES), F32)],
        compiler_params=_cparams(("arbitrary",)),
    )(x, g.reshape(1, D_MODEL), grp.mod4, grp.mod4, wrt, rbt, triu)

    counts = cnt[:N_GROUPS, 0]
    padded = ((counts + te - 1) // te) * te
    ends = jnp.cumsum(padded)
    dest = (ends - padded)[gr[:, 0, :].reshape(rows)] + gr[:, 1, :].reshape(rows)
    tile_group = jnp.minimum(jnp.searchsorted(ends, jnp.arange(ntile, dtype=jnp.int32) * te, side="right"),
                             N_GROUPS - 1).astype(jnp.int32)
    n_valid = (ends[N_GROUPS - 1] // te).astype(jnp.int32).reshape(1)

    xs = pl.pallas_call(
        _scatter_rows_kernel,
        out_shape=jax.ShapeDtypeStruct((cap, D_MODEL), F32),
        grid_spec=pltpu.PrefetchScalarGridSpec(
            num_scalar_prefetch=1,
            grid=(rows // MOE_DMA_ROWS,),
            in_specs=[pl.BlockSpec((MOE_DMA_ROWS, D_MODEL), lambda i, d: (i, 0)),
                      pl.BlockSpec(memory_space=pl.ANY)],
            out_specs=pl.BlockSpec(memory_space=pl.ANY),
            scratch_shapes=[pltpu.SemaphoreType.DMA(())]),
        input_output_aliases={2: 0},
        compiler_params=_cparams(("arbitrary",)),
    )(dest, h, jnp.zeros((cap, D_MODEL), F32))

    ys = pl.pallas_call(
        _moe_group_kernel,
        out_shape=jax.ShapeDtypeStruct((cap, D_MODEL), F32),
        grid_spec=pltpu.PrefetchScalarGridSpec(
            num_scalar_prefetch=2,
            grid=(ntile, EXPERTS_PER_GROUP),
            in_specs=[pl.BlockSpec((te, D_MODEL), lambda i, j, tg, nv: (i, 0)),
                      pl.BlockSpec((D_MODEL, LANES), lambda i, j, tg, nv: (0, 0)),
                      pl.BlockSpec((1, LANES), lambda i, j, tg, nv: (0, 0)),
                      pl.BlockSpec((None, None, D_MODEL, D_EXPERT),
                                   lambda i, j, tg, nv: (l, tg[i] * EXPERTS_PER_GROUP + j, 0, 0)),
                      pl.BlockSpec((None, None, D_MODEL, D_EXPERT),
                                   lambda i, j, tg, nv: (l, tg[i] * EXPERTS_PER_GROUP + j, 0, 0)),
                      pl.BlockSpec((None, None, D_EXPERT, D_MODEL),
                                   lambda i, j, tg, nv: (l, tg[i] * EXPERTS_PER_GROUP + j, 0, 0))],
            out_specs=pl.BlockSpec((te, D_MODEL), lambda i, j, tg, nv: (i, 0)),
            scratch_shapes=[pltpu.VMEM((te, D_MODEL), BF16), pltpu.VMEM((te, LANES), F32)]),
        compiler_params=_cparams(("arbitrary", "arbitrary"), MOE_VMEM_LIMIT),
    )(tile_group, n_valid, xs, wr, rb, wg, wu, wd)

    gt_spec = grp.mod_spec(5, tm)
    final = gf is not None
    gain = (gf if final else g).reshape(1, D_MODEL)
    return pl.pallas_call(
        functools.partial(_gather_residual_kernel, final),
        out_shape=jax.ShapeDtypeStruct((rows, D_MODEL), F32),
        grid_spec=pltpu.PrefetchScalarGridSpec(
            num_scalar_prefetch=1,
            grid=(rows // tm,),
            in_specs=[pl.BlockSpec((tm, D_MODEL), lambda i, d: (i, 0)),
                      pl.BlockSpec(gt_spec.block_shape, lambda i, d: gt_spec.index_map(i)),
                      pl.BlockSpec((1, D_MODEL), lambda i, d: (0, 0)),
                      pl.BlockSpec(memory_space=pl.ANY)],
            out_specs=pl.BlockSpec((tm, D_MODEL), lambda i, d: (i, 0)),
            scratch_shapes=[pltpu.VMEM((2, tm, D_MODEL), F32), pltpu.SemaphoreType.DMA((2,))]),
        compiler_params=_cparams(("arbitrary",)),
    )(dest, x, grp.mod4, gain, ys)


def _block_diag(w):
    eye = jnp.eye(C_BLOCKS, dtype=w.dtype)
    return jnp.einsum("hg,hij->higj", eye, w).reshape(D_BRANCH, D_BRANCH)


def _layer_params(l, p):
    row = lambda a: a[l].reshape(1, -1)
    ones = jnp.asarray(np.kron(np.eye(4), np.ones((A_HEAD, A_HEAD))), BF16)
    pad_lo = lambda w: jnp.pad(w, ((0, 64), (0, 0)))
    pad_hi = lambda w: jnp.pad(w, ((64, 0), (0, 0)))
    rw = dict(mu=row(p["a_mu"]), w0=row(p["a_w0"]), w2=pad_lo(p["a_w2"][l]).astype(BF16),
              a0=row(p["a_a0"]), a2=pad_hi(p["a_a2"][l]).astype(BF16), g2=p["a_g2"][l].astype(BF16),
              kk=row(p["a_kk"]), ka=row(p["a_ka"]), rk=row(p["a_rk"]), lng=row(p["a_ln_g"]),
              lnb=row(p["a_ln_b"]), ones=ones)
    lru = dict(cw=p["c_conv_w"][l], cb=row(p["c_conv_b"]),
               wri=jnp.concatenate([_block_diag(p["c_wr"][l]), _block_diag(p["c_wi"][l])], axis=1).astype(BF16),
               br=row(p["c_br"]), bi=row(p["c_bi"]), lam=row(p["c_lam"]))
    return dict(
        rw=rw, lru=lru,
        w_branch=p["w_branch"][l].astype(BF16),
        w_out=p["w_out"][l].astype(BF16),
        wg=p["moe_wg"], wu=p["moe_wu"], wd=p["moe_wd"],
        norm_mix=p["norm_mix"][l], norm_ffn=p["norm_ffn"][l])


def kernel(x_prompt, x_sample, c_prompt, c_sample, state_rwkv_shift, state_rwkv_wkv, state_ret, state_lru_h, state_lru_conv, norm_mix, norm_ffn, norm_final, ada_w, ada_b, w_in, a_mu, a_w0, a_w2, a_a0, a_a2, a_g2, a_kk, a_ka, a_rk, a_ln_g, a_ln_b, c_conv_w, c_conv_b, c_wr, c_br, c_wi, c_bi, c_lam, w_branch, w_out, w_router, router_bias, moe_wg, moe_wu, moe_wd):
    p = dict(norm_mix=norm_mix, norm_ffn=norm_ffn, w_in=w_in, a_mu=a_mu, a_w0=a_w0, a_w2=a_w2,
             a_a0=a_a0, a_a2=a_a2, a_g2=a_g2, a_kk=a_kk, a_ka=a_ka, a_rk=a_rk, a_ln_g=a_ln_g,
             a_ln_b=a_ln_b, c_conv_w=c_conv_w, c_conv_b=c_conv_b, c_wr=c_wr, c_br=c_br, c_wi=c_wi,
             c_bi=c_bi, c_lam=c_lam, w_branch=w_branch, w_out=w_out, moe_wg=moe_wg, moe_wu=moe_wu,
             moe_wd=moe_wd)
    bp, lp_, _ = x_prompt.shape
    bs, ls, _ = x_sample.shape
    layers = [_layer_params(l, p) for l in range(DEPTH)]
    w3 = _arrange_w_in(w_in)
    router = (jnp.pad(w_router, ((0, 0), (0, LANES - N_EXPERTS))).astype(BF16),
              jnp.pad(router_bias, (0, LANES - N_EXPERTS)).reshape(1, LANES))

    n_c = bp + bs
    pad_c = (-n_c) % 16
    c_all = jnp.pad(jnp.concatenate([c_prompt, c_sample], axis=0), ((0, pad_c), (0, 0)))
    mods = _ada(c_all, ada_w, ada_b)

    def run(x, batch, seq, mod, states, pos0, cfg):
        xs = x.reshape(batch * seq, D_MODEL)
        grps = []
        for l in range(DEPTH):
            if cfg["per_token"]:
                m = jnp.repeat(mod[l], seq, axis=0).reshape(batch * seq, 6, D_MODEL)
                m4 = m.transpose(1, 0, 2)[None]
            else:
                m4 = mod[l].reshape(batch, 6, 1, D_MODEL)
            grps.append(_Group(batch, seq, m4))
        return _trunk_layers(xs, grps, states, layers, w3, router, norm_final, pos0, cfg)

    zeros = lambda s: jnp.zeros((DEPTH, bp) + s.shape[2:], x_prompt.dtype)
    st_prompt = (zeros(state_rwkv_shift), zeros(state_rwkv_wkv), zeros(state_ret),
                 zeros(state_lru_h), zeros(state_lru_conv))
    st_sample = (state_rwkv_shift, state_rwkv_wkv, state_ret, state_lru_h, state_lru_conv)
    cfg_p = dict(per_token=False, tm=min(512, lp_), tm_mg=min(256, lp_), tm_mm=min(1024, lp_),
                 tm_moe=min(1024, lp_), moe_sorted=(bp * lp_) % MOE_DMA_ROWS == 0, rwkv_nb=8, lru_nb=1, lru_lb=min(512, lp_))
    cfg_s = dict(per_token=True, tm=min(512, bs * ls), tm_mg=min(256, bs * ls), tm_mm=min(512, bs * ls),
                 tm_moe=min(512, bs * ls), moe_sorted=False, rwkv_nb=16, lru_nb=16, lru_lb=ls)
    y_p, new_p = run(x_prompt, bp, lp_, mods[:, :bp], st_prompt, 0, cfg_p)
    y_s, new_s = run(x_sample, bs, ls, mods[:, bp:bp + bs], st_sample, PAST_LEN, cfg_s)
    return (y_p, y_s) + new_p + new_s


def _trunk_layers(x, grps, states, layers, w3, router, norm_final, pos0, cfg):
    batch, seq = grps[0].batch, grps[0].seq
    tabs = _ret_tables(seq, pos0)
    wr, rb = router
    outs = [[] for _ in range(5)]
    for l, lp in enumerate(layers):
        grp = grps[l]
        h = _norm_mod(x, lp["norm_mix"], grp, 1, 0, cfg["tm"])
        pa = _matmul(h, w3, l, W_PA, W_PA_PAD, cfg["tm_mm"], 1024)
        rq = _matmul(h, w3, l, W_RQ, 4 * D_BRANCH, cfg["tm_mm"], 1024)
        cx = _matmul(h, w3, l, W_CX, 2 * D_BRANCH, cfg["tm_mm"], 1024)
        if seq % RW_CHUNK == 0:
            o_a, wkv = _rwkv_chunked(pa, states[0][l], states[1][l], lp["rw"], batch, seq,
                                     min(4, seq // RW_CHUNK))
        else:
            o_a, wkv = _rwkv_steps(pa, states[0][l], states[1][l], lp["rw"], batch, seq, cfg["rwkv_nb"])
        o_b, ret = _retention(rq, states[2], l, tabs, batch, seq)
        o_c, lru_h = _rglru(cx, states[4][l], states[3][l], lp["lru"], batch, seq,
                            cfg["lru_nb"], cfg["lru_lb"])
        x = _merge_out(h, o_a, o_b, o_c, x, grp, w3, l, lp["w_branch"], lp["w_out"], cfg["tm_mg"])
        moe = _moe_sorted if cfg["moe_sorted"] else _moe
        x = moe(x, lp["norm_ffn"], grp, wr, rb, lp["wg"], lp["wu"], lp["wd"], l, cfg["tm_moe"],
                gf=norm_final if l == DEPTH - 1 else None)
        outs[0].append(pa.reshape(batch, seq, W_PA_PAD)[:, -1, :A_PROJ])
        outs[1].append(wkv)
        outs[2].append(ret)
        outs[3].append(lru_h)
        outs[4].append(cx.reshape(batch, seq, 2 * D_BRANCH)[:, seq - (CONV_W - 1):, :D_BRANCH])
    return x.reshape(batch, seq, D_MODEL), tuple(jnp.stack(o) for o in outs)
```

```python
import functools
import math

import numpy as np
import jax
import jax.numpy as jnp
from jax import lax
from jax.experimental import pallas as pl
from jax.experimental.pallas import tpu as pltpu

F32 = jnp.float32
BF16 = jnp.bfloat16

D_MODEL = 1024
DEPTH = 2
PAST_LEN = 16384
D_BRANCH = 512
A_HEAD = 64
A_HEADS = 8
A_PROJ = 1792
A_NORM_EPS = 64e-5
A_KK_EPS = 1e-12
R_HEAD = 128
R_HEADS = 4
R_CHUNK = 128
R_NORM_EPS = 1e-6
ROPE_BASE = 10000.0
C_BLOCK = 64
C_BLOCKS = 8
CONV_W = 4
LRU_C = 8.0
N_EXPERTS = 16
N_GROUPS = 4
EXPERTS_PER_GROUP = 4
TOP_K = 2
D_EXPERT = 512
NORM_EPS = 1e-6

LANES = 128
RET_ROWS = 128
RET_CHUNKS_PER_STEP = 4
VMEM_LIMIT = 48 * 1024 * 1024
MOE_VMEM_LIMIT = 56 * 1024 * 1024


def _cparams(sem, vmem=VMEM_LIMIT):
    return pltpu.CompilerParams(dimension_semantics=sem, vmem_limit_bytes=vmem)


def _dot(a, b):
    return jnp.dot(a, b, preferred_element_type=F32)


def _bdot(a, b):
    return jnp.dot(a.astype(BF16), b.astype(BF16), preferred_element_type=F32)


def _split3(x):
    hi = x.astype(BF16)
    r1 = x - hi.astype(F32)
    mid = r1.astype(BF16)
    lo = (r1 - mid.astype(F32)).astype(BF16)
    return hi, mid, lo


def _segsum(x, ones_bf16):
    w = ones_bf16.shape[0]
    hi, mid, lo = _split3(x)
    parts = []
    for j in range(x.shape[1] // w):
        c = slice(j * w, (j + 1) * w)
        parts.append((_dot(lo[:, c], ones_bf16) + _dot(mid[:, c], ones_bf16)) + _dot(hi[:, c], ones_bf16))
    return jnp.concatenate(parts, axis=1)


def _ada_kernel(c_ref, w_ref, b_ref, o_ref):
    c = c_ref[...]
    s = c * jax.nn.sigmoid(c)
    o_ref[...] = _bdot(s, w_ref[...]) + b_ref[...]


def _ada(c, ada_w, ada_b):
    rows = c.shape[0]
    tn = 1536
    return pl.pallas_call(
        _ada_kernel,
        out_shape=jax.ShapeDtypeStruct((DEPTH, rows, 6 * D_MODEL), F32),
        grid=(DEPTH, 6 * D_MODEL // tn),
        in_specs=[
            pl.BlockSpec((rows, D_MODEL), lambda l, j: (0, 0)),
            pl.BlockSpec((None, D_MODEL, tn), lambda l, j: (l, 0, j)),
            pl.BlockSpec((None, 1, tn), lambda l, j: (l, 0, j)),
        ],
        out_specs=pl.BlockSpec((None, rows, tn), lambda l, j: (l, 0, j)),
        compiler_params=_cparams(("arbitrary", "arbitrary")),
    )(c, ada_w, ada_b.reshape(DEPTH, 1, 6 * D_MODEL))


class _Group:
    def __init__(self, batch, seq, mod4):
        self.batch = batch
        self.seq = seq
        self.rows = batch * seq
        self.mod4 = mod4
        self.per_token = mod4.shape[0] == 1 and mod4.shape[2] != 1

    def mod_spec(self, k, tm):
        if self.per_token:
            return pl.BlockSpec((None, None, tm, D_MODEL), lambda i, *_: (0, k, i, 0))
        seq = self.seq
        return pl.BlockSpec((None, None, 1, D_MODEL), lambda i, *_: ((i * tm) // seq, k, 0, 0))


def _rms(x, g):
    return x * lax.rsqrt(jnp.mean(x * x, axis=-1, keepdims=True) + NORM_EPS) * g


def _norm_mod_kernel(x_ref, g_ref, sc_ref, sh_ref, o_ref):
    y = _rms(x_ref[...], g_ref[...])
    o_ref[...] = (y * (1.0 + sc_ref[...]) + sh_ref[...]).astype(o_ref.dtype)


def _norm_mod(x, g, grp, k_sc, k_sh, tm):
    return pl.pallas_call(
        _norm_mod_kernel,
        out_shape=jax.ShapeDtypeStruct((grp.rows, D_MODEL), BF16),
        grid=(grp.rows // tm,),
        in_specs=[
            pl.BlockSpec((tm, D_MODEL), lambda i: (i, 0)),
            pl.BlockSpec((1, D_MODEL), lambda i: (0, 0)),
            grp.mod_spec(k_sc, tm),
            grp.mod_spec(k_sh, tm),
        ],
        out_specs=pl.BlockSpec((tm, D_MODEL), lambda i: (i, 0)),
        compiler_params=_cparams(("arbitrary",)),
    )(x, g.reshape(1, D_MODEL), grp.mod4, grp.mod4)


def _mm_kernel(a_ref, w_ref, o_ref):
    o_ref[...] = _dot(a_ref[...], w_ref[...])


def _matmul(a, w3, l, col0, n, tm, tn):
    rows, k = a.shape
    c0 = col0 // tn
    return pl.pallas_call(
        _mm_kernel,
        out_shape=jax.ShapeDtypeStruct((rows, n), F32),
        grid=(n // tn, rows // tm),
        in_specs=[
            pl.BlockSpec((tm, k), lambda j, i: (i, 0)),
            pl.BlockSpec((None, k, tn), lambda j, i: (l, 0, c0 + j)),
        ],
        out_specs=pl.BlockSpec((tm, tn), lambda j, i: (i, j)),
        compiler_params=_cparams(("arbitrary", "arbitrary")),
    )(a, w3)


W_RQ, W_CX, W_GM, W_PA, W_PA_PAD = 0, 2048, 3072, 6144, 2048


W_BLK = 256


def _arrange_kernel(src_ref, w_ref, o_ref):
    del src_ref
    j = pl.program_id(1)
    real = (W_PA + A_PROJ) // W_BLK

    @pl.when(j < real)
    def _():
        o_ref[...] = w_ref[...].astype(BF16)

    @pl.when(j >= real)
    def _():
        o_ref[...] = jnp.zeros_like(o_ref)


def _arrange_w_in(w_in):
    depth, k, n = w_in.shape
    nblk = n // W_BLK
    first = A_PROJ // W_BLK
    src = list(range(first, nblk)) + list(range(first))
    nout = (W_PA + W_PA_PAD) // W_BLK
    src = jnp.asarray(src + [0] * (nout - len(src)), jnp.int32)
    return pl.pallas_call(
        _arrange_kernel,
        out_shape=jax.ShapeDtypeStruct((depth, k, nout * W_BLK), BF16),
        grid_spec=pltpu.PrefetchScalarGridSpec(
            num_scalar_prefetch=1,
            grid=(depth, nout),
            in_specs=[pl.BlockSpec((None, k, W_BLK), lambda l, j, s: (l, 0, s[j]))],
            out_specs=pl.BlockSpec((None, k, W_BLK), lambda l, j, s: (l, 0, j))),
        compiler_params=_cparams(("arbitrary", "arbitrary")),
    )(src, w_in)


def _rwkv_pre(pa, prev, mu_ref, w0_ref, w2_ref, a0_ref, a2_ref, g2_ref, kk_ref, ka_ref, ones):
    pm = pa + (prev - pa) * mu_ref[...]
    r = pm[:, 0:512]
    k = pm[:, 512:1024]
    v = pm[:, 1024:1536]
    xwa = pm[:, 1536:1664]
    xg = pm[:, 1664:1792]
    w_log = -jax.nn.softplus(-(w0_ref[...] + _bdot(jnp.tanh(xwa), w2_ref[...]))) - 0.5
    logw = -jnp.exp(w_log)
    a = jax.nn.sigmoid(a0_ref[...] + _bdot(xwa, a2_ref[...]))
    g = _bdot(jax.nn.sigmoid(xg), g2_ref[...])
    kk = k * kk_ref[...]
    kk = kk * lax.rsqrt(_segsum(kk * kk, ones) + A_KK_EPS)
    k2 = k * (1.0 + (a - 1.0) * ka_ref[...])
    return r, logw, k2, v, -kk, kk * a, g


def _rwkv_post(o, r, k2, v, g, rk_ref, lng_ref, lnb_ref, ones):
    mean = _segsum(o, ones) * (1.0 / A_HEAD)
    oc = o - mean
    var = _segsum(oc * oc, ones) * (1.0 / A_HEAD)
    o = oc * lax.rsqrt(var + A_NORM_EPS) * lng_ref[...] + lnb_ref[...]
    bonus = _segsum(r * k2 * rk_ref[...], ones) * v
    return (o + bonus) * g


_NN = (((1,), (0,)), ((), ()))
_NT = (((1,), (1,)), ((), ()))
_TN = (((0,), (0,)), ((), ()))
RW_CHUNK = 64
RW_PAIRS = A_HEADS // 2


def _bdg(a, b, dims):
    return lax.dot_general(a.astype(BF16), b.astype(BF16), dims, preferred_element_type=F32)


def _rwkv_chunk_kernel(nck,
                       pa_ref, sh0_ref, g0_ref, mu_ref, w0_ref, w2_ref, a0_ref, a2_ref, g2_ref,
                       kk_ref, ka_ref, rk_ref, lng_ref, lnb_ref, ones_ref, tri_ref,
                       o_ref, gout_ref,
                       carry_ref, prev_ref, g_s, wu_s, m_s, zy_s, o_s):
    c = pl.program_id(1)
    ck = RW_CHUNK
    ones = ones_ref[...]

    @pl.when(c == 0)
    def _():
        carry_ref[...] = sh0_ref[...]
        g_s[...] = g0_ref[...]

    pa = pa_ref[...]
    rows = pa.shape[0]
    prev_ref[...] = pltpu.roll(pa, 1, axis=0)
    prev_ref[pl.ds(0, 1), :] = carry_ref[...]
    carry_ref[...] = pa[rows - 1:rows, :]
    r, logw, k2, v, an, bn, g = _rwkv_pre(pa, prev_ref[...], mu_ref, w0_ref, w2_ref, a0_ref, a2_ref,
                                          g2_ref, kk_ref, ka_ref, ones)

    tri = tri_ref[...]
    hi, mid, lo = _split3(logw)
    cum =(_dot(tri, lo) + _dot(tri, mid)) + _dot(tri, hi)
    cum_last = jnp.concatenate(
        [jnp.broadcast_to(cum[(i + 1) * ck - 1:(i + 1) * ck, :], (ck, D_BRANCH)) for i in range(nck)], axis=0)
    gam = jnp.exp(cum)
    inv = jnp.exp(-cum)
    to_end = jnp.exp(cum_last - cum)
    a_t = an * jnp.exp(cum - logw)
    b_t = bn * inv
    k_t = k2 * inv
    r_t = r * gam
    b_e = bn * to_end
    k_e = k2 * to_end

    lane = lax.broadcasted_iota(jnp.int32, (ck, LANES), 1)
    rowi = lax.broadcasted_iota(jnp.int32, (ck, LANES), 0)
    m0 = lane < A_HEAD
    coli = lane & (A_HEAD - 1)
    strict = rowi > coli
    incl = rowi >= coli
    eye_p = (rowi == coli).astype(F32)
    r128 = lax.broadcasted_iota(jnp.int32, (LANES, LANES), 0)
    c128 = lax.broadcasted_iota(jnp.int32, (LANES, LANES), 1)
    blockmask = (r128 < A_HEAD) == (c128 < A_HEAD)
    eye128 = r128 == c128
    bk_t = jnp.concatenate([b_e, k_e], axis=1).T

    def bd(q):
        q = q.astype(BF16)
        z = jnp.zeros_like(q)
        return jnp.concatenate([jnp.where(m0, q, z), jnp.where(m0, z, q)], axis=0)

    probs = [(i, p) for i in range(nck) for p in range(RW_PAIRS)]
    sl = {(i, p): (slice(i * ck, (i + 1) * ck), slice(p * LANES, (p + 1) * LANES)) for i, p in probs}
    l_pow, l_ak, t_inv = {}, {}, {}
    for q in probs:
        rs, ls = sl[q]
        lhs = jnp.concatenate([a_t[rs, ls], r_t[rs, ls]], axis=0).astype(BF16)
        ab = lax.dot_general(lhs, bd(b_t[rs, ls]), _NT, preferred_element_type=F32)
        ak = lax.dot_general(lhs, bd(k_t[rs, ls]), _NT, preferred_element_type=F32)
        l_pow[q] = jnp.where(strict, ab[:ck], 0.0)
        l_ak[q] = jnp.where(strict, ak[:ck], 0.0)
        m_s[q[0], q[1], :, 0:LANES] = jnp.where(incl, ab[ck:], 0.0)
        m_s[q[0], q[1], :, LANES:2 * LANES] = jnp.where(incl, ak[ck:], 0.0)
        t_inv[q] = eye_p + l_pow[q]
    n = 1
    while 2 * n < ck:
        for q in probs:
            l_pow[q] = _bdg(l_pow[q], bd(l_pow[q]), _NN)
        for q in probs:
            t_inv[q] = t_inv[q] + _bdg(t_inv[q], bd(l_pow[q]), _NN)
        n *= 2
    lak_v = {}
    for q in probs:
        rs, ls = sl[q]
        lak_v[q] = _bdg(l_ak[q], bd(v[rs, ls]), _NN)
    for q in probs:
        rs, ls = sl[q]
        wu = _bdg(t_inv[q], jnp.concatenate([bd(a_t[rs, ls]), bd(lak_v[q])], axis=1), _NN)
        wu_s[q[0], q[1]] = wu
        i, p = q
        half = jnp.zeros((ck, LANES), F32)
        place = (lambda x: jnp.concatenate([x, half], axis=0)) if i % 2 == 0 else \
                (lambda x: jnp.concatenate([half, x], axis=0))
        tcols = slice((i // 2) * LANES, (i // 2 + 1) * LANES)
        b_tr = bk_t[p * LANES:(p + 1) * LANES, tcols]
        k_tr = bk_t[D_BRANCH + p * LANES:D_BRANCH + (p + 1) * LANES, tcols]
        z_t = _bdg(b_tr, place(wu[:, 0:LANES]), _NN)
        y_t = _bdg(jnp.concatenate([b_tr, k_tr], axis=1),
                   jnp.concatenate([place(wu[:, LANES:2 * LANES]), place(v[rs, ls])], axis=0), _NN)
        zy_s[i, p, :, 0:LANES] = jnp.where(blockmask, z_t, 0.0)
        zy_s[i, p, :, LANES:2 * LANES] = jnp.where(blockmask, y_t, 0.0)

    for i in range(nck):
        rs = slice(i * ck, (i + 1) * ck)
        for p in range(RW_PAIRS):
            ls = slice(p * LANES, (p + 1) * LANES)
            h_p = g_s[p]
            h_bf = h_p.astype(BF16)
            gcol = jnp.sum(jnp.where(eye128, gam[(i + 1) * ck - 1:(i + 1) * ck, ls], 0.0), axis=1, keepdims=True)
            g_s[p] = (h_p * gcol + _dot(zy_s[i, p, :, 0:LANES].astype(BF16), h_bf)) + zy_s[i, p, :, LANES:2 * LANES]
            wu = wu_s[i, p]
            u = _dot(wu[:, 0:LANES].astype(BF16), h_bf) + wu[:, LANES:2 * LANES]
            o_s[rs, ls] = _dot(r_t[rs, ls].astype(BF16), h_bf) \
                + _bdg(m_s[i, p], jnp.concatenate([bd(u), bd(v[rs, ls])], axis=0), _NN)

    o_ref[...] = _rwkv_post(o_s[...], r, k2, v, g, rk_ref, lng_ref, lnb_ref, ones).astype(o_ref.dtype)
    gout_ref[...] = g_s[...]


def _rwkv_chunked(pa, shift0, s0, prm, batch, seq, nck):
    lb = nck * RW_CHUNK
    nstep = seq // lb
    s0p = s0.reshape(batch, RW_PAIRS, 2, A_HEAD, A_HEAD)
    eye2 = jnp.eye(2, dtype=s0.dtype)
    g0 = jnp.einsum("bpjvk,ji->bpjkiv", s0p, eye2).reshape(batch, RW_PAIRS, LANES, LANES)
    tri = np.kron(np.eye(nck), np.tril(np.ones((RW_CHUNK, RW_CHUNK)))).astype(np.float32)
    params = [prm[n] for n in ("mu", "w0", "w2", "a0", "a2", "g2", "kk", "ka", "rk", "lng", "lnb", "ones")]
    params.append(jnp.asarray(tri, BF16))

    def full(arr):
        nd = arr.ndim
        return pl.BlockSpec(arr.shape, lambda i, c: (0,) * nd)

    o, g_out = pl.pallas_call(
        functools.partial(_rwkv_chunk_kernel, nck),
        out_shape=(jax.ShapeDtypeStruct((batch * seq, D_BRANCH), BF16),
                   jax.ShapeDtypeStruct((batch, RW_PAIRS, LANES, LANES), F32)),
        grid=(batch, nstep),
        in_specs=[pl.BlockSpec((lb, A_PROJ), lambda i, c: (i * nstep + c, 0)),
                  pl.BlockSpec((None, 1, A_PROJ), lambda i, c: (i, 0, 0)),
                  pl.BlockSpec((None, RW_PAIRS, LANES, LANES), lambda i, c: (i, 0, 0, 0))]
                 + [full(p) for p in params],
        out_specs=(pl.BlockSpec((lb, D_BRANCH), lambda i, c: (i * nstep + c, 0)),
                   pl.BlockSpec((None, RW_PAIRS, LANES, LANES), lambda i, c: (i, 0, 0, 0))),
        scratch_shapes=[pltpu.VMEM((1, A_PROJ), F32),
                        pltpu.VMEM((lb, A_PROJ), F32),
                        pltpu.VMEM((RW_PAIRS, LANES, LANES), F32),
                        pltpu.VMEM((nck, RW_PAIRS, RW_CHUNK, 2 * LANES), F32),
                        pltpu.VMEM((nck, RW_PAIRS, RW_CHUNK, 2 * LANES), F32),
                        pltpu.VMEM((nck, RW_PAIRS, LANES, 2 * LANES), F32),
                        pltpu.VMEM((lb, D_BRANCH), F32)],
        compiler_params=_cparams(("arbitrary", "arbitrary")),
    )(pa, shift0.reshape(batch, 1, A_PROJ), g0, *params)
    g5 = g_out.reshape(batch, RW_PAIRS, 2, A_HEAD, 2, A_HEAD)
    s_new = jnp.stack([g5[:, :, 0, :, 0, :], g5[:, :, 1, :, 1, :]], axis=2).swapaxes(-1, -2)
    return o, s_new.reshape(batch, A_HEADS, A_HEAD, A_HEAD)


def _rwkv_step_kernel(nb, lb,
                      pa_ref, sh0_ref, s0_ref, mu_ref, w0_ref, w2_ref, a0_ref, a2_ref, g2_ref,
                      kk_ref, ka_ref, rk_ref, lng_ref, lnb_ref, ones_ref,
                      o_ref, sout_ref):
    ones = ones_ref[...]
    pa = pa_ref[...].reshape(lb * nb, A_PROJ)
    prev = jnp.concatenate([sh0_ref[...], pa[:(lb - 1) * nb, :]], axis=0)
    r, logw, k2, v, an, bn, g = _rwkv_pre(pa, prev, mu_ref, w0_ref, w2_ref, a0_ref, a2_ref,
                                          g2_ref, kk_ref, ka_ref, ones)
    w = jnp.exp(logw)
    srows = nb * A_HEAD
    rowi = lax.broadcasted_iota(jnp.int32, (srows, D_BRANCH), 0)
    lane = lax.broadcasted_iota(jnp.int32, (srows, D_BRANCH), 1)
    eye = (rowi & (A_HEAD - 1)) == (lane & (A_HEAD - 1))

    def per_seq(x, t):
        xt = x[t * nb:(t + 1) * nb, :]
        return jnp.concatenate([jnp.broadcast_to(xt[b:b + 1, :], (A_HEAD, D_BRANCH)) for b in range(nb)],
                               axis=0)

    s0 = s0_ref[...]
    s = jnp.concatenate([s0[:, h] for h in range(A_HEADS)], axis=-1).reshape(srows, D_BRANCH)
    outs = []
    for t in range(lb):
        sa = _segsum(s * per_seq(an, t), ones)
        vcol = _segsum(jnp.where(eye, per_seq(v, t), 0.0), ones)
        s = s * per_seq(w, t) + sa * per_seq(bn, t) + vcol * per_seq(k2, t)
        out = _segsum(s * per_seq(r, t), ones)
        outs.append(jnp.sum(jnp.where(eye, out, 0.0).reshape(nb, A_HEAD, D_BRANCH), axis=1))
    o = jnp.concatenate(outs, axis=0)
    res = _rwkv_post(o, r, k2, v, g, rk_ref, lng_ref, lnb_ref, ones).astype(o_ref.dtype)
    o_ref[...] = res.reshape(lb, nb, D_BRANCH)
    s3 = s.reshape(nb, A_HEAD, D_BRANCH)
    for h in range(A_HEADS):
        sout_ref[:, h] = s3[:, :, h * A_HEAD:(h + 1) * A_HEAD]


def _rwkv_steps(pa, shift0, s0, prm, batch, seq, nb):
    pa_tm = pa.reshape(batch, seq, pa.shape[1]).transpose(1, 0, 2)
    st_spec = pl.BlockSpec((nb, A_HEADS, A_HEAD, A_HEAD), lambda i: (i, 0, 0, 0))

    def full(arr):
        nd = arr.ndim
        return pl.BlockSpec(arr.shape, lambda i: (0,) * nd)

    params = [prm[n] for n in ("mu", "w0", "w2", "a0", "a2", "g2", "kk", "ka", "rk", "lng", "lnb", "ones")]
    o, s_out = pl.pallas_call(
        functools.partial(_rwkv_step_kernel, nb, seq),
        out_shape=(jax.ShapeDtypeStruct((seq, batch, D_BRANCH), BF16),
                   jax.ShapeDtypeStruct((batch, A_HEADS, A_HEAD, A_HEAD), F32)),
        grid=(batch // nb,),
        in_specs=[pl.BlockSpec((seq, nb, A_PROJ), lambda i: (0, i, 0)),
                  pl.BlockSpec((nb, A_PROJ), lambda i: (i, 0)),
                  st_spec]
                 + [full(p) for p in params],
        out_specs=(pl.BlockSpec((seq, nb, D_BRANCH), lambda i: (0, i, 0)), st_spec),
        compiler_params=_cparams(("arbitrary",)),
    )(pa_tm, shift0, s0, *params)
    return o.transpose(1, 0, 2).reshape(batch * seq, D_BRANCH), s_out


def _ret_kernel(nb, lb, nck,
                rq_ref, cos_ref, sin_ref, dm_ref, kd_ref, qd_ref, cd_ref, s0_ref,
                o_ref, sout_ref, s_ref):
    c = pl.program_id(1)

    @pl.when(c == 0)
    def _():
        s_ref[...] = s0_ref[...]

    scale = R_HEAD ** -0.5
    half = R_HEAD // 2
    row8 = lax.broadcasted_iota(jnp.int32, (8, R_HEAD), 0)
    pre = {}
    for ci in range(nck):
        rs = slice(ci * RET_ROWS, (ci + 1) * RET_ROWS)
        cos = cos_ref[rs, :]
        sin = sin_ref[rs, :]
        for h in range(R_HEADS):
            lo, hi = h * R_HEAD, (h + 1) * R_HEAD
            q = rq_ref[rs, lo:hi]
            k = rq_ref[rs, 512 + lo:512 + hi]
            v = rq_ref[rs, 1024 + lo:1024 + hi]
            qh = q * cos + pltpu.roll(q, half, axis=1) * sin
            kh = (k * cos + pltpu.roll(k, half, axis=1) * sin) * scale
            scores = lax.dot_general(qh.astype(BF16), kh.astype(BF16), (((1,), (1,)), ((), ())),
                                     preferred_element_type=F32) * dm_ref[h]
            pre[ci, h] = (_bdot(scores, v), qh * qd_ref[:, lo:hi], kh * kd_ref[:, lo:hi], v)
    for ci, h in sorted(pre):
        rs = slice(ci * RET_ROWS, (ci + 1) * RET_ROWS)
        lo, hi = h * R_HEAD, (h + 1) * R_HEAD
        o, qd, ku, v = pre[ci, h]
        gate = rq_ref[rs, 1536 + lo:1536 + hi]
        cd = cd_ref[h]
        if nb == 1:
            s = s_ref[0, h]
            o = o + _bdot(qd, s)
            upd = lax.dot_general(ku.astype(BF16), v.astype(BF16), (((0,), (0,)), ((), ())),
                                  preferred_element_type=F32)
            s_ref[0, h] = s * cd + upd
        else:
            per_tile = 8 // lb
            inter = []
            for i in range(RET_ROWS // 8):
                qd_t = qd[i * 8:(i + 1) * 8, :]
                ku_t = ku[i * 8:(i + 1) * 8, :]
                v_t = v[i * 8:(i + 1) * 8, :].astype(BF16)
                acc = None
                for j in range(per_tile):
                    b = i * per_tile + j
                    m = (row8 >= j * lb) & (row8 < (j + 1) * lb)
                    s = s_ref[b, h]
                    part = _bdot(jnp.where(m, qd_t, 0.0), s)
                    acc = part if acc is None else acc + part
                    upd = lax.dot_general(jnp.where(m, ku_t, 0.0).astype(BF16), v_t,
                                          (((0,), (0,)), ((), ())), preferred_element_type=F32)
                    s_ref[b, h] = s * cd + upd
                inter.append(acc)
            o = o + jnp.concatenate(inter, axis=0)
        oc = o - jnp.mean(o, axis=-1, keepdims=True)
        on = oc * lax.rsqrt(jnp.mean(oc * oc, axis=-1, keepdims=True) + R_NORM_EPS)
        o_ref[rs, lo:hi] = (gate * jax.nn.sigmoid(gate) * on).astype(o_ref.dtype)
    sout_ref[...] = s_ref[...]


def _retention(rq, s0_all, l, tabs, batch, seq):
    lb = min(R_CHUNK, seq)
    nb = RET_ROWS // lb
    nck = RET_CHUNKS_PER_STEP if (nb == 1 and (seq // lb) % RET_CHUNKS_PER_STEP == 0) else 1
    rows = RET_ROWS * nck
    nchunk = seq // (lb * nck)
    cos, sin, dm, kd, qd, cd = tabs
    ntab = cos.shape[0] // rows

    def const(arr):
        nd = arr.ndim
        return pl.BlockSpec(arr.shape, lambda i, c: (0,) * nd)

    tab_idx = (lambda i, c: (c, 0)) if ntab > 1 else (lambda i, c: (0, 0))
    return pl.pallas_call(
        functools.partial(_ret_kernel, nb, lb, nck),
        out_shape=(jax.ShapeDtypeStruct((batch * seq, D_BRANCH), BF16),
                   jax.ShapeDtypeStruct((batch, R_HEADS, R_HEAD, R_HEAD), F32)),
        grid=(batch // nb, nchunk),
        in_specs=[pl.BlockSpec((rows, 4 * D_BRANCH), lambda i, c: (i * nchunk + c, 0)),
                  pl.BlockSpec((rows, R_HEAD), tab_idx),
                  pl.BlockSpec((rows, R_HEAD), tab_idx),
                  const(dm), const(kd), const(qd), const(cd),
                  pl.BlockSpec((None, nb, R_HEADS, R_HEAD, R_HEAD), lambda i, c: (l, i, 0, 0, 0))],
        out_specs=(pl.BlockSpec((rows, D_BRANCH), lambda i, c: (i * nchunk + c, 0)),
                   pl.BlockSpec((nb, R_HEADS, R_HEAD, R_HEAD), lambda i, c: (i, 0, 0, 0))),
        scratch_shapes=[pltpu.VMEM((nb, R_HEADS, R_HEAD, R_HEAD), F32)],
        compiler_params=_cparams(("arbitrary", "arbitrary")),
    )(rq, cos, sin, dm, kd, qd, cd, s0_all)


def _ret_tables(seq, pos0):
    lb = min(R_CHUNK, seq)
    nb = RET_ROWS // lb
    half = R_HEAD // 2
    pos = pos0 + jnp.arange(seq, dtype=jnp.int32)
    inv = ROPE_BASE ** (-jnp.arange(half, dtype=F32) / half)
    ang = pos.astype(F32)[:, None] * inv[None, :]
    cos, sin = jnp.cos(ang), jnp.sin(ang)
    cos2 = jnp.concatenate([cos, cos], axis=-1)
    sin2 = jnp.concatenate([-sin, sin], axis=-1)
    if nb > 1:
        cos2 = jnp.tile(cos2, (nb, 1))
        sin2 = jnp.tile(sin2, (nb, 1))
    log_g = jnp.log1p(-jnp.exp2(-5.0 - jnp.arange(R_HEADS, dtype=F32)))
    idx = jnp.arange(lb, dtype=F32)
    diff = idx[:, None] - idx[None, :]
    dmask = jnp.where(diff >= 0, jnp.exp(log_g[:, None, None] * jnp.maximum(diff, 0.0)), 0.0)
    if nb > 1:
        dmask = jnp.einsum("ab,hij->haibj", jnp.eye(nb, dtype=F32), dmask).reshape(
            R_HEADS, RET_ROWS, RET_ROWS)
    k_decay = jnp.exp(log_g[None, :] * (lb - 1.0 - idx)[:, None])
    q_decay = jnp.exp(log_g[None, :] * (idx + 1.0)[:, None])
    kd = jnp.tile(jnp.repeat(k_decay, R_HEAD, axis=1), (nb, 1))
    qd = jnp.tile(jnp.repeat(q_decay, R_HEAD, axis=1), (nb, 1))
    cd = jnp.broadcast_to(jnp.exp(log_g * lb)[:, None, None], (R_HEADS, 1, R_HEAD))
    return cos2, sin2, dmask, kd, qd, cd


def _lru_kernel(nb, lb, three_d,
                cx_ref, conv0_ref, h0_ref, cw_ref, cb_ref, wri_ref, br_ref, bi_ref, lam_ref,
                o_ref, hout_ref,
                cc_ref, hc_ref, x1_ref, x2_ref, x3_ref, a_ref, b_ref):
    c = pl.program_id(1)
    rows = nb * lb

    @pl.when(c == 0)
    def _():
        cc_ref[...] = conv0_ref[...]
        hc_ref[...] = h0_ref[...]

    cx = cx_ref[...]
    if three_d:
        cx = cx.reshape(rows, 2 * D_BRANCH)
    xb = cx[:, :D_BRANCH]
    gb = cx[:, D_BRANCH:]
    x1_ref[...] = pltpu.roll(xb, 1, axis=0)
    x2_ref[...] = pltpu.roll(xb, 2, axis=0)
    x3_ref[...] = pltpu.roll(xb, 3, axis=0)
    for b in range(nb):
        r0 = b * lb
        c0 = cc_ref[b, 0:1, :]
        c1 = cc_ref[b, 1:2, :]
        c2 = cc_ref[b, 2:3, :]
        x1_ref[pl.ds(r0, 1), :] = c2
        x2_ref[pl.ds(r0, 1), :] = c1
        x2_ref[pl.ds(r0 + 1, 1), :] = c2
        x3_ref[pl.ds(r0, 1), :] = c0
        x3_ref[pl.ds(r0 + 1, 1), :] = c1
        x3_ref[pl.ds(r0 + 2, 1), :] = c2
        cc_ref[b] = xb[r0 + lb - 3:r0 + lb, :]
    cw = cw_ref[...]
    xc = cb_ref[...] + (((x3_ref[...] * cw[0:1, :] + x2_ref[...] * cw[1:2, :]) + x1_ref[...] * cw[2:3, :])
                        + xb * cw[3:4, :])
    ri = _bdot(xc, wri_ref[...])
    r = jax.nn.sigmoid(ri[:, :D_BRANCH] + br_ref[...])
    i = jax.nn.sigmoid(ri[:, D_BRANCH:] + bi_ref[...])
    log_a = LRU_C * r * jax.nn.log_sigmoid(lam_ref[...])
    a = jnp.exp(log_a)
    bb = jnp.sqrt(-jnp.tanh(log_a) * (a * a + 1.0)) * (i * xc)
    a_ref[...] = a
    b_ref[...] = bb
    for b in range(nb):
        r0 = b * lb
        b_ref[pl.ds(r0, 1), :] = bb[r0:r0 + 1, :] + a[r0:r0 + 1, :] * hc_ref[b]
    a = a_ref[...]
    bb = b_ref[...]
    t_idx = lax.broadcasted_iota(jnp.int32, (rows, D_BRANCH), 0) % lb
    s = 1
    while s < lb:
        keep = t_idx >= s
        a_sh = jnp.where(keep, pltpu.roll(a, s, axis=0), 1.0)
        b_sh = jnp.where(keep, pltpu.roll(bb, s, axis=0), 0.0)
        bb = a * b_sh + bb
        a = a * a_sh
        s *= 2
    h = bb
    for b in range(nb):
        r0 = b * lb
        hc_ref[b] = h[r0 + lb - 1:r0 + lb, :]
    res = (h * jax.nn.gelu(gb)).astype(o_ref.dtype)
    if three_d:
        res = res.reshape(nb, lb, D_BRANCH)
    o_ref[...] = res
    hout_ref[...] = hc_ref[...]


def _rglru(cx, conv0, h0, prm, batch, seq, nb, lb):
    nchunk = seq // lb
    rows = nb * lb
    three_d = nb > 1 and nchunk > 1
    if three_d:
        cx_in = cx.reshape(batch, seq, 2 * D_BRANCH)
        cx_spec = pl.BlockSpec((nb, lb, 2 * D_BRANCH), lambda i, c: (i, c, 0))
        o_shape = jax.ShapeDtypeStruct((batch, seq, D_BRANCH), BF16)
        o_spec = pl.BlockSpec((nb, lb, D_BRANCH), lambda i, c: (i, c, 0))
    else:
        cx_in = cx
        cx_spec = pl.BlockSpec((rows, 2 * D_BRANCH), lambda i, c: (i * nchunk + c, 0))
        o_shape = jax.ShapeDtypeStruct((batch * seq, D_BRANCH), BF16)
        o_spec = pl.BlockSpec((rows, D_BRANCH), lambda i, c: (i * nchunk + c, 0))

    def full(arr):
        nd = arr.ndim
        return pl.BlockSpec(arr.shape, lambda i, c: (0,) * nd)

    params = [prm[n] for n in ("cw", "cb", "wri", "br", "bi", "lam")]
    o, h_out = pl.pallas_call(
        functools.partial(_lru_kernel, nb, lb, three_d),
        out_shape=(o_shape, jax.ShapeDtypeStruct((batch, 1, D_BRANCH), F32)),
        grid=(batch // nb, nchunk),
        in_specs=[cx_spec,
                  pl.BlockSpec((nb, CONV_W - 1, D_BRANCH), lambda i, c: (i, 0, 0)),
                  pl.BlockSpec((nb, 1, D_BRANCH), lambda i, c: (i, 0, 0))]
                 + [full(p) for p in params],
        out_specs=(o_spec, pl.BlockSpec((nb, 1, D_BRANCH), lambda i, c: (i, 0, 0))),
        scratch_shapes=[pltpu.VMEM((nb, CONV_W - 1, D_BRANCH), F32),
                        pltpu.VMEM((nb, 1, D_BRANCH), F32)]
                       + [pltpu.VMEM((rows, D_BRANCH), F32) for _ in range(5)],
        compiler_params=_cparams(("arbitrary", "arbitrary")),
    )(cx_in, conv0, h0.reshape(batch, 1, D_BRANCH), *params)
    return o.reshape(batch * seq, D_BRANCH), h_out.reshape(batch, D_BRANCH)


def _merge_kernel(h_ref, oa_ref, ob_ref, oc_ref, x_ref, gt_ref, wgm_ref, wb_ref, wout_ref, o_ref):
    gm = _dot(h_ref[...], wgm_ref[...])
    merged = None
    for n, br_ref in enumerate((oa_ref, ob_ref, oc_ref)):
        br = _dot(br_ref[...], wb_ref[n])
        term = jax.nn.sigmoid(gm[:, n * D_MODEL:(n + 1) * D_MODEL]) * br
        merged = term if merged is None else merged + term
    y = _bdot(merged, wout_ref[...])
    o_ref[...] = x_ref[...] + gt_ref[...] * y


def _merge_out(h, oa, ob, oc, x, grp, w3, l, wb, wout, tm):
    row = lambda i: (i, 0)
    return pl.pallas_call(
        _merge_kernel,
        out_shape=jax.ShapeDtypeStruct((grp.rows, D_MODEL), F32),
        grid=(grp.rows // tm,),
        in_specs=[pl.BlockSpec((tm, D_MODEL), row),
                  pl.BlockSpec((tm, D_BRANCH), row),
                  pl.BlockSpec((tm, D_BRANCH), row),
                  pl.BlockSpec((tm, D_BRANCH), row),
                  pl.BlockSpec((tm, D_MODEL), row),
                  grp.mod_spec(2, tm),
                  pl.BlockSpec((None, D_MODEL, 3 * D_MODEL), lambda i: (l, 0, W_GM // (3 * D_MODEL))),
                  pl.BlockSpec(wb.shape, lambda i: (0, 0, 0)),
                  pl.BlockSpec(wout.shape, lambda i: (0, 0))],
        out_specs=pl.BlockSpec((tm, D_MODEL), row),
        compiler_params=_cparams(("arbitrary",), MOE_VMEM_LIMIT),
    )(h, oa, ob, oc, x, grp.mod4, w3, wb, wout)


def _group_member(x, k, lane):
    ge = EXPERTS_PER_GROUP
    pos = lane & (ge - 1)
    return jnp.where(pos + k < ge, pltpu.roll(x, LANES - k, axis=1), pltpu.roll(x, ge - k, axis=1))


def _in_group_top2(sel, lane):
    ge = EXPERTS_PER_GROUP
    pos = lane & (ge - 1)
    n_ahead = jnp.zeros(sel.shape, jnp.int32)
    for k in range(1, ge):
        other = _group_member(sel, k, lane)
        lower_index = pos + k >= ge
        n_ahead = n_ahead + ((other > sel) | ((other == sel) & lower_index)).astype(jnp.int32)
    return n_ahead < TOP_K


def _best_group(sel, in_top2, lane):
    ge = EXPERTS_PER_GROUP
    kept = jnp.where(in_top2, sel, 0.0)
    score = kept
    for k in range(1, ge):
        score = score + _group_member(kept, k, lane)
    best = lane < N_EXPERTS
    for d in range(1, N_GROUPS):
        later = pltpu.roll(score, LANES - ge * d, axis=1)
        earlier = pltpu.roll(score, ge * d, axis=1)
        best = best & (later <= score) & (earlier < score)
    return best


def _router_probs(h_bf16, wr_ref, rb_ref, lane):
    logits = _dot(h_bf16, wr_ref[...])
    logits = jnp.where(lane < N_EXPERTS, logits, -jnp.inf)
    mx = jnp.max(logits, axis=-1, keepdims=True)
    ex = jnp.exp(logits - mx)
    probs = ex / jnp.sum(ex, axis=-1, keepdims=True)
    sel = jnp.where(lane < N_EXPERTS, probs + rb_ref[...], -jnp.inf)
    return probs, sel


def _route(probs, sel, lane):
    in_top2 = _in_group_top2(sel, lane)
    pk = jnp.where(_best_group(sel, in_top2, lane) & in_top2, probs, 0.0)
    return pk / jnp.sum(pk, axis=-1, keepdims=True)


def _moe_kernel(final, x_ref, g_ref, sc_ref, sh_ref, gt_ref, gf_ref, wr_ref, rb_ref, wg_ref, wu_ref, wd_ref,
                o_ref, h_s, gate_s):
    e = pl.program_id(1)
    tm = x_ref.shape[0]
    lane = lax.broadcasted_iota(jnp.int32, (tm, LANES), 1)

    @pl.when(e == 0)
    def _():
        y = _rms(x_ref[...], g_ref[...])
        h = (y * (1.0 + sc_ref[...]) + sh_ref[...]).astype(BF16)
        h_s[...] = h
        probs, selv = _router_probs(h, wr_ref, rb_ref, lane)
        gate_s[...] = _route(probs, selv, lane)
        o_ref[...] = jnp.zeros_like(o_ref)

    h = h_s[...]
    gcol = jnp.sum(jnp.where(lane == e, gate_s[...], 0.0), axis=-1, keepdims=True)
    hg = _bdot(h, wg_ref[...])
    hu = _bdot(h, wu_ref[...])
    act = hg * jax.nn.sigmoid(hg) * hu * gcol
    o_ref[...] += _bdot(act, wd_ref[...])

    @pl.when(e == N_EXPERTS - 1)
    def _():
        res = x_ref[...] + gt_ref[...] * o_ref[...]
        o_ref[...] = _rms(res, gf_ref[...]) if final else res


def _moe(x, g, grp, wr, rb, wg, wu, wd, l, tm, gf=None):
    row = lambda i, e: (i, 0)
    final = gf is not None
    gain = (gf if final else g).reshape(1, D_MODEL)
    return pl.pallas_call(
        functools.partial(_moe_kernel, final),
        out_shape=jax.ShapeDtypeStruct((grp.rows, D_MODEL), F32),
        grid=(grp.rows // tm, N_EXPERTS),
        in_specs=[pl.BlockSpec((tm, D_MODEL), row),
                  pl.BlockSpec((1, D_MODEL), lambda i, e: (0, 0)),
                  grp.mod_spec(4, tm),
                  grp.mod_spec(3, tm),
                  grp.mod_spec(5, tm),
                  pl.BlockSpec((1, D_MODEL), lambda i, e: (0, 0)),
                  pl.BlockSpec((D_MODEL, LANES), lambda i, e: (0, 0)),
                  pl.BlockSpec((1, LANES), lambda i, e: (0, 0)),
                  pl.BlockSpec((None, None, D_MODEL, D_EXPERT), lambda i, e: (l, e, 0, 0)),
                  pl.BlockSpec((None, None, D_MODEL, D_EXPERT), lambda i, e: (l, e, 0, 0)),
                  pl.BlockSpec((None, None, D_EXPERT, D_MODEL), lambda i, e: (l, e, 0, 0))],
        out_specs=pl.BlockSpec((tm, D_MODEL), row),
        scratch_shapes=[pltpu.VMEM((tm, D_MODEL), BF16),
                        pltpu.VMEM((tm, LANES), F32)],
        compiler_params=_cparams(("arbitrary", "arbitrary"), MOE_VMEM_LIMIT),
    )(x, g.reshape(1, D_MODEL), grp.mod4, grp.mod4, grp.mod4, gain, wr, rb, wg, wu, wd)


MOE_SORT_TILE = 1024
MOE_DMA_ROWS = 2048
DMA_ISSUE_GROUP = 8


def _moe_router_kernel(x_ref, g_ref, sc_ref, sh_ref, wrt_ref, rbt_ref, triu_ref,
                       h_ref, gr_ref, cnt_ref, base_s):
    i = pl.program_id(0)
    tm = x_ref.shape[0]
    ge = EXPERTS_PER_GROUP

    @pl.when(i == 0)
    def _():
        base_s[...] = jnp.zeros_like(base_s)

    y = _rms(x_ref[...], g_ref[...])
    h = y * (1.0 + sc_ref[...]) + sh_ref[...]
    h_ref[...] = h
    logits = lax.dot_general(wrt_ref[...], h.astype(BF16), _NT, preferred_element_type=F32)
    ex = jnp.exp(logits - jnp.max(logits, axis=0, keepdims=True))
    probs = ex / jnp.sum(ex, axis=0, keepdims=True)
    sel = probs + rbt_ref[:, 0:1]
    row = lax.broadcasted_iota(jnp.int32, (N_EXPERTS, tm), 0)
    pos = row & (ge - 1)
    grp_of = row >> 2

    def member(x, k):
        return jnp.where(pos + k < ge, pltpu.roll(x, N_EXPERTS - k, axis=0), pltpu.roll(x, ge - k, axis=0))

    n_ahead = jnp.zeros(sel.shape, jnp.int32)
    for k in range(1, ge):
        other = member(sel, k)
        n_ahead = n_ahead + ((other > sel) | ((other == sel) & (pos + k >= ge))).astype(jnp.int32)
    kept = jnp.where(n_ahead < TOP_K, sel, 0.0)
    score = kept
    for k in range(1, ge):
        score = score + member(kept, k)
    best = pos == 0
    for d in range(1, N_GROUPS):
        later = pltpu.roll(score, N_EXPERTS - ge * d, axis=0)
        earlier = pltpu.roll(score, ge * d, axis=0)
        best = best & ((grp_of + d >= N_GROUPS) | (later <= score)) & ((grp_of < d) | (earlier < score))
    gid = jnp.sum(jnp.where(best, grp_of, 0), axis=0, keepdims=True)
    onehot = (row == gid).astype(BF16)
    incl = _dot(onehot, triu_ref[...]) + base_s[:, 0:1]
    rank = jnp.sum(jnp.where(row == gid, incl - 1.0, 0.0), axis=0, keepdims=True).astype(jnp.int32)
    row8 = lax.broadcasted_iota(jnp.int32, (8, tm), 0)
    gr_ref[...] = jnp.where(row8 == 0, gid, jnp.where(row8 == 1, rank, 0))
    total = jnp.broadcast_to(incl[:, tm - 1:tm], (N_EXPERTS, LANES))
    base_s[...] = total
    cnt_ref[...] = total.astype(jnp.int32)


def _scatter_rows_kernel(dest_ref, src_ref, zeros_hbm, dst_hbm, sem):
    del zeros_hbm
    base = pl.program_id(0) * MOE_DMA_ROWS

    def issue(t8, carry):
        for k in range(DMA_ISSUE_GROUP):
            t = t8 * DMA_ISSUE_GROUP + k
            pltpu.make_async_copy(src_ref.at[pl.ds(t, 1)], dst_hbm.at[pl.ds(dest_ref[base + t], 1)],
                                  sem).start(priority=k % 2)
        return carry

    def drain(t, carry):
        pltpu.make_async_copy(src_ref.at[pl.ds(0, 1)], dst_hbm.at[pl.ds(0, 1)], sem).wait()
        return carry

    lax.fori_loop(0, MOE_DMA_ROWS // DMA_ISSUE_GROUP, issue, 0)
    lax.fori_loop(0, MOE_DMA_ROWS, drain, 0, unroll=8)


def _moe_group_kernel(tg_ref, nv_ref, xs_ref, wr_ref, rb_ref, wg_ref, wu_ref, wd_ref, ys_ref, h_s, gate_s):
    i = pl.program_id(0)
    j = pl.program_id(1)
    tm = xs_ref.shape[0]
    lane = lax.broadcasted_iota(jnp.int32, (tm, LANES), 1)
    grp = tg_ref[i]

    @pl.when((i >= nv_ref[0]) & (j == 0))
    def _():
        ys_ref[...] = jnp.zeros_like(ys_ref)

    @pl.when(i < nv_ref[0])
    def _():
        @pl.when(j == 0)
        def _():
            h = xs_ref[...].astype(BF16)
            h_s[...] = h
            probs, sel = _router_probs(h, wr_ref, rb_ref, lane)
            pk = jnp.where(_in_group_top2(sel, lane) & ((lane >> 2) == grp), probs, 0.0)
            psum = jnp.sum(pk, axis=-1, keepdims=True)
            gate_s[...] = pk / jnp.where(psum > 0.0, psum, 1.0)
            ys_ref[...] = jnp.zeros_like(ys_ref)

        h = h_s[...]
        gcol = jnp.sum(jnp.where(lane == grp * EXPERTS_PER_GROUP + j, gate_s[...], 0.0), axis=-1, keepdims=True)
        hg = _bdot(h, wg_ref[...])
        hu = _bdot(h, wu_ref[...])
        act = hg * jax.nn.sigmoid(hg) * hu * gcol
        ys_ref[...] += _bdot(act, wd_ref[...])


def _gather_residual_kernel(final, dest_ref, x_ref, gt_ref, gf_ref, ys_hbm, o_ref, buf, sem):
    i = pl.program_id(0)
    n = pl.num_programs(0)
    tm = x_ref.shape[0]

    def start_tile(tile, slot):
        def issue(t8, carry):
            for k in range(DMA_ISSUE_GROUP):
                t = t8 * DMA_ISSUE_GROUP + k
                pltpu.make_async_copy(ys_hbm.at[pl.ds(dest_ref[tile * tm + t], 1)], buf.at[slot, pl.ds(t, 1)],
                                      sem.at[slot]).start(priority=k % 2)
            return carry
        lax.fori_loop(0, tm // DMA_ISSUE_GROUP, issue, 0)

    @pl.when(i == 0)
    def _():
        start_tile(0, 0)

    @pl.when(i + 1 < n)
    def _():
        start_tile(i + 1, (i + 1) % 2)

    slot = i % 2

    def drain(t, carry):
        pltpu.make_async_copy(ys_hbm.at[pl.ds(0, 1)], buf.at[slot, pl.ds(0, 1)], sem.at[slot]).wait()
        return carry

    lax.fori_loop(0, tm, drain, 0, unroll=8)
    res = x_ref[...] + gt_ref[...] * buf[slot]
    if final:
        res = _rms(res, gf_ref[...])
    o_ref[...] = res


def _moe_sorted(x, g, grp, wr, rb, wg, wu, wd, l, tm, gf=None):
    rows = grp.rows
    te = MOE_SORT_TILE
    ntile = rows // te + N_GROUPS
    cap = ntile * te
    triu = jnp.asarray(np.triu(np.ones((tm, tm), np.float32)), BF16)
    wrt = wr[:, :N_EXPERTS].T
    rbt = jnp.broadcast_to(rb[0, :N_EXPERTS, None], (N_EXPERTS, LANES))
    row = lambda i: (i, 0)
    h, gr, cnt = pl.pallas_call(
        _moe_router_kernel,
        out_shape=(jax.ShapeDtypeStruct((rows, D_MODEL), F32),
                   jax.ShapeDtypeStruct((rows // tm, 8, tm), jnp.int32),
                   jax.ShapeDtypeStruct((N_EXPERTS, LANES), jnp.int32)),
        grid=(rows // tm,),
        in_specs=[pl.BlockSpec((tm, D_MODEL), row),
                  pl.BlockSpec((1, D_MODEL), lambda i: (0, 0)),
                  grp.mod_spec(4, tm),
                  grp.mod_spec(3, tm),
                  pl.BlockSpec((N_EXPERTS, D_MODEL), lambda i: (0, 0)),
                  pl.BlockSpec((N_EXPERTS, LANES), lambda i: (0, 0)),
                  pl.BlockSpec((tm, tm), lambda i: (0, 0))],
        out_specs=(pl.BlockSpec((tm, D_MODEL), row),
                   pl.BlockSpec((None, 8, tm), lambda i: (i, 0, 0)),
                   pl.BlockSpec((N_EXPERTS, LANES), lambda i: (0, 0))),
        scratch_shapes=[pltpu.VMEM((N_EXPERTS, LANES), F32)],
        compiler_params=_cparams(("arbitrary",)),
    )(x, g.reshape(1, D_MODEL), grp.mod4, grp.mod4, wrt, rbt, triu)

    counts = cnt[:N_GROUPS, 0]
    padded = ((counts + te - 1) // te) * te
    ends = jnp.cumsum(padded)
    dest = (ends - padded)[gr[:, 0, :].reshape(rows)] + gr[:, 1, :].reshape(rows)
    tile_group = jnp.minimum(jnp.searchsorted(ends, jnp.arange(ntile, dtype=jnp.int32) * te, side="right"),
                             N_GROUPS - 1).astype(jnp.int32)
    n_valid = (ends[N_GROUPS - 1] // te).astype(jnp.int32).reshape(1)

    xs = pl.pallas_call(
        _scatter_rows_kernel,
        out_shape=jax.ShapeDtypeStruct((cap, D_MODEL), F32),
        grid_spec=pltpu.PrefetchScalarGridSpec(
            num_scalar_prefetch=1,
            grid=(rows // MOE_DMA_ROWS,),
            in_specs=[pl.BlockSpec((MOE_DMA_ROWS, D_MODEL), lambda i, d: (i, 0)),
                      pl.BlockSpec(memory_space=pl.ANY)],
            out_specs=pl.BlockSpec(memory_space=pl.ANY),
            scratch_shapes=[pltpu.SemaphoreType.DMA(())]),
        input_output_aliases={2: 0},
        compiler_params=_cparams(("arbitrary",)),
    )(dest, h, jnp.zeros((cap, D_MODEL), F32))

    ys = pl.pallas_call(
        _moe_group_kernel,
        out_shape=jax.ShapeDtypeStruct((cap, D_MODEL), F32),
        grid_spec=pltpu.PrefetchScalarGridSpec(
            num_scalar_prefetch=2,
            grid=(ntile, EXPERTS_PER_GROUP),
            in_specs=[pl.BlockSpec((te, D_MODEL), lambda i, j, tg, nv: (i, 0)),
                      pl.BlockSpec((D_MODEL, LANES), lambda i, j, tg, nv: (0, 0)),
                      pl.BlockSpec((1, LANES), lambda i, j, tg, nv: (0, 0)),
                      pl.BlockSpec((None, None, D_MODEL, D_EXPERT),
                                   lambda i, j, tg, nv: (l, tg[i] * EXPERTS_PER_GROUP + j, 0, 0)),
                      pl.BlockSpec((None, None, D_MODEL, D_EXPERT),
                                   lambda i, j, tg, nv: (l, tg[i] * EXPERTS_PER_GROUP + j, 0, 0)),
                      pl.BlockSpec((None, None, D_EXPERT, D_MODEL),
                                   lambda i, j, tg, nv: (l, tg[i] * EXPERTS_PER_GROUP + j, 0, 0))],
            out_specs=pl.BlockSpec((te, D_MODEL), lambda i, j, tg, nv: (i, 0)),
            scratch_shapes=[pltpu.VMEM((te, D_MODEL), BF16), pltpu.VMEM((te, LANES), F32)]),
        compiler_params=_cparams(("arbitrary", "arbitrary"), MOE_VMEM_LIMIT),
    )(tile_group, n_valid, xs, wr, rb, wg, wu, wd)

    gt_spec = grp.mod_spec(5, tm)
    final = gf is not None
    gain = (gf if final else g).reshape(1, D_MODEL)
    return pl.pallas_call(
        functools.partial(_gather_residual_kernel, final),
        out_shape=jax.ShapeDtypeStruct((rows, D_MODEL), F32),
        grid_spec=pltpu.PrefetchScalarGridSpec(
            num_scalar_prefetch=1,
            grid=(rows // tm,),
            in_specs=[pl.BlockSpec((tm, D_MODEL), lambda i, d: (i, 0)),
                      pl.BlockSpec(gt_spec.block_shape, lambda i, d: gt_spec.index_map(i)),
                      pl.BlockSpec((1, D_MODEL), lambda i, d: (0, 0)),
                      pl.BlockSpec(memory_space=pl.ANY)],
            out_specs=pl.BlockSpec((tm, D_MODEL), lambda i, d: (i, 0)),
            scratch_shapes=[pltpu.VMEM((2, tm, D_MODEL), F32), pltpu.SemaphoreType.DMA((2,))]),
        compiler_params=_cparams(("arbitrary",)),
    )(dest, x, grp.mod4, gain, ys)


def _block_diag(w):
    eye = jnp.eye(C_BLOCKS, dtype=w.dtype)
    return jnp.einsum("hg,hij->higj", eye, w).reshape(D_BRANCH, D_BRANCH)


def _layer_params(l, p):
    row = lambda a: a[l].reshape(1, -1)
    ones = jnp.asarray(np.kron(np.eye(4), np.ones((A_HEAD, A_HEAD))), BF16)
    pad_lo = lambda w: jnp.pad(w, ((0, 64), (0, 0)))
    pad_hi = lambda w: jnp.pad(w, ((64, 0), (0, 0)))
    rw = dict(mu=row(p["a_mu"]), w0=row(p["a_w0"]), w2=pad_lo(p["a_w2"][l]).astype(BF16),
              a0=row(p["a_a0"]), a2=pad_hi(p["a_a2"][l]).astype(BF16), g2=p["a_g2"][l].astype(BF16),
              kk=row(p["a_kk"]), ka=row(p["a_ka"]), rk=row(p["a_rk"]), lng=row(p["a_ln_g"]),
              lnb=row(p["a_ln_b"]), ones=ones)
    lru = dict(cw=p["c_conv_w"][l], cb=row(p["c_conv_b"]),
               wri=jnp.concatenate([_block_diag(p["c_wr"][l]), _block_diag(p["c_wi"][l])], axis=1).astype(BF16),
               br=row(p["c_br"]), bi=row(p["c_bi"]), lam=row(p["c_lam"]))
    return dict(
        rw=rw, lru=lru,
        w_branch=p["w_branch"][l].astype(BF16),
        w_out=p["w_out"][l].astype(BF16),
        wg=p["moe_wg"], wu=p["moe_wu"], wd=p["moe_wd"],
        norm_mix=p["norm_mix"][l], norm_ffn=p["norm_ffn"][l])


def kernel(x_prompt, x_sample, c_prompt, c_sample, state_rwkv_shift, state_rwkv_wkv, state_ret, state_lru_h, state_lru_conv, norm_mix, norm_ffn, norm_final, ada_w, ada_b, w_in, a_mu, a_w0, a_w2, a_a0, a_a2, a_g2, a_kk, a_ka, a_rk, a_ln_g, a_ln_b, c_conv_w, c_conv_b, c_wr, c_br, c_wi, c_bi, c_lam, w_branch, w_out, w_router, router_bias, moe_wg, moe_wu, moe_wd):
    p = dict(norm_mix=norm_mix, norm_ffn=norm_ffn, w_in=w_in, a_mu=a_mu, a_w0=a_w0, a_w2=a_w2,
             a_a0=a_a0, a_a2=a_a2, a_g2=a_g2, a_kk=a_kk, a_ka=a_ka, a_rk=a_rk, a_ln_g=a_ln_g,
             a_ln_b=a_ln_b, c_conv_w=c_conv_w, c_conv_b=c_conv_b, c_wr=c_wr, c_br=c_br, c_wi=c_wi,
             c_bi=c_bi, c_lam=c_lam, w_branch=w_branch, w_out=w_out, moe_wg=moe_wg, moe_wu=moe_wu,
             moe_wd=moe_wd)
    bp, lp_, _ = x_prompt.shape
    bs, ls, _ = x_sample.shape
    layers = [_layer_params(l, p) for l in range(DEPTH)]
    w3 = _arrange_w_in(w_in)
    router = (jnp.pad(w_router, ((0, 0), (0, LANES - N_EXPERTS))).astype(BF16),
              jnp.pad(router_bias, (0, LANES - N_EXPERTS)).reshape(1, LANES))

    n_c = bp + bs
    pad_c = (-n_c) % 16
    c_all = jnp.pad(jnp.concatenate([c_prompt, c_sample], axis=0), ((0, pad_c), (0, 0)))
    mods = _ada(c_all, ada_w, ada_b)

    def run(x, batch, seq, mod, states, pos0, cfg):
        xs = x.reshape(batch * seq, D_MODEL)
        grps = []
        for l in range(DEPTH):
            if cfg["per_token"]:
                m = jnp.repeat(mod[l], seq, axis=0).reshape(batch * seq, 6, D_MODEL)
                m4 = m.transpose(1, 0, 2)[None]
            else:
                m4 = mod[l].reshape(batch, 6, 1, D_MODEL)
            grps.append(_Group(batch, seq, m4))
        return _trunk_layers(xs, grps, states, layers, w3, router, norm_final, pos0, cfg)

    zeros = lambda s: jnp.zeros((DEPTH, bp) + s.shape[2:], x_prompt.dtype)
    st_prompt = (zeros(state_rwkv_shift), zeros(state_rwkv_wkv), zeros(state_ret),
                 zeros(state_lru_h), zeros(state_lru_conv))
    st_sample = (state_rwkv_shift, state_rwkv_wkv, state_ret, state_lru_h, state_lru_conv)
    cfg_p = dict(per_token=False, tm=min(1024, lp_), tm_mg=min(512, lp_), tm_mm=min(1024, lp_),
                 tm_moe=min(1024, lp_), moe_sorted=(bp * lp_) % MOE_DMA_ROWS == 0, rwkv_nb=8, lru_nb=1, lru_lb=min(512, lp_))
    cfg_s = dict(per_token=True, tm=min(512, bs * ls), tm_mg=min(256, bs * ls), tm_mm=min(512, bs * ls),
                 tm_moe=min(512, bs * ls), moe_sorted=False, rwkv_nb=16, lru_nb=16, lru_lb=ls)
    y_p, new_p = run(x_prompt, bp, lp_, mods[:, :bp], st_prompt, 0, cfg_p)
    y_s, new_s = run(x_sample, bs, ls, mods[:, bp:bp + bs], st_sample, PAST_LEN, cfg_s)
    return (y_p, y_s) + new_p + new_s


def _trunk_layers(x, grps, states, layers, w3, router, norm_final, pos0, cfg):
    batch, seq = grps[0].batch, grps[0].seq
    tabs = _ret_tables(seq, pos0)
    wr, rb = router
    outs = [[] for _ in range(5)]
    for l, lp in enumerate(layers):
        grp = grps[l]
        h = _norm_mod(x, lp["norm_mix"], grp, 1, 0, cfg["tm"])
        pa = _matmul(h, w3, l, W_PA, W_PA_PAD, cfg["tm_mm"], 1024)
        rq = _matmul(h, w3, l, W_RQ, 4 * D_BRANCH, cfg["tm_mm"], 1024)
        cx = _matmul(h, w3, l, W_CX, 2 * D_BRANCH, cfg["tm_mm"], 1024)
        if seq % RW_CHUNK == 0:
            o_a, wkv = _rwkv_chunked(pa, states[0][l], states[1][l], lp["rw"], batch, seq,
                                     min(8, seq // RW_CHUNK))
        else:
            o_a, wkv = _rwkv_steps(pa, states[0][l], states[1][l], lp["rw"], batch, seq, cfg["rwkv_nb"])
        o_b, ret = _retention(rq, states[2], l, tabs, batch, seq)
        o_c, lru_h = _rglru(cx, states[4][l], states[3][l], lp["lru"], batch, seq,
                            cfg["lru_nb"], cfg["lru_lb"])
        x = _merge_out(h, o_a, o_b, o_c, x, grp, w3, l, lp["w_branch"], lp["w_out"], cfg["tm_mg"])
        moe = _moe_sorted if cfg["moe_sorted"] else _moe
        x = moe(x, lp["norm_ffn"], grp, wr, rb, lp["wg"], lp["wu"], lp["wd"], l, cfg["tm_moe"],
                gf=norm_final if l == DEPTH - 1 else None)
        outs[0].append(pa.reshape(batch, seq, W_PA_PAD)[:, -1, :A_PROJ])
        outs[1].append(wkv)
        outs[2].append(ret)
        outs[3].append(lru_h)
        outs[4].append(cx.reshape(batch, seq, 2 * D_BRANCH)[:, seq - (CONV_W - 1):, :D_BRANCH])
    return x.reshape(batch, seq, D_MODEL), tuple(jnp.stack(o) for o in outs)
```
